```python
import math
import jax
import jax.numpy as jnp
from jax import lax
import numpy as np

D_MODEL = 1024
BATCH = 16
SEQ = 2048
DEPTH = 2

GRID_W = 64
CTX_LEN = 256
Q_BLOCK = 128
ROPE_THETA = 10000.0
EPS = 1e-6
N_MOD = 6

DIFF_HEADS = 4
DIFF_HEAD_DIM = 64
DIFF_V_DIM = 2 * DIFF_HEAD_DIM
GQA_Q_HEADS = 8
GQA_KV_HEADS = 2
GQA_GROUP = GQA_Q_HEADS // GQA_KV_HEADS
GQA_HEAD_DIM = 64
MLA_HEADS = 8
MLA_Q_LORA = 384
MLA_KV_LORA = 256
MLA_NOPE_DIM = 64
MLA_ROPE_DIM = 32
MLA_QK_DIM = MLA_NOPE_DIM + MLA_ROPE_DIM
MLA_V_DIM = 64

N_BRANCHES = 3
DIFF_WIDTH = DIFF_HEADS * DIFF_V_DIM
GQA_WIDTH = GQA_Q_HEADS * GQA_HEAD_DIM
MLA_WIDTH = MLA_HEADS * MLA_V_DIM
FFN_HIDDEN = -((-8 * D_MODEL) // (3 * 256)) * 256

IN_SIZES = (
    DIFF_HEADS * 2 * DIFF_HEAD_DIM,
    DIFF_HEADS * 2 * DIFF_HEAD_DIM,
    DIFF_WIDTH,
    GQA_Q_HEADS * GQA_HEAD_DIM,
    GQA_KV_HEADS * GQA_HEAD_DIM,
    GQA_KV_HEADS * GQA_HEAD_DIM,
    MLA_Q_LORA,
    MLA_KV_LORA,
    MLA_ROPE_DIM,
    N_BRANCHES * D_MODEL,
)
IN_WIDTH = sum(IN_SIZES)
IN_SPLITS = [sum(IN_SIZES[:i + 1]) for i in range(len(IN_SIZES) - 1)]

kernel_name = 'hybrid_diff_gqa_mla_prefix_dit_block'


def _rmsnorm(x, g):
    xf = x.astype(jnp.float32)
    y = xf * lax.rsqrt(jnp.mean(xf * xf, axis=-1, keepdims=True) + EPS)
    return (y * g.astype(jnp.float32)).astype(x.dtype)


def _modulate(h, shift, scale):
    return h * (1.0 + scale) + shift


def _rope_angles(row, col, dim):
    a = dim // 2
    freqs = ROPE_THETA ** (-jnp.arange(0, a, 2, dtype=jnp.float32) / a)
    return (row.astype(jnp.float32)[:, None] * freqs, col.astype(jnp.float32)[:, None] * freqs)


def _rope1d(x, ang):
    ang = ang.reshape(ang.shape[:1] + (1,) * (x.ndim - 3) + ang.shape[1:])
    cos, sin = jnp.cos(ang), jnp.sin(ang)
    x1, x2 = jnp.split(x.astype(jnp.float32), 2, axis=-1)
    return jnp.concatenate([x1 * cos - x2 * sin, x1 * sin + x2 * cos], axis=-1).astype(x.dtype)


def _rope2d(x, angs):
    ang_row, ang_col = angs
    a = x.shape[-1] // 2
    return jnp.concatenate([_rope1d(x[..., :a], ang_row), _rope1d(x[..., a:], ang_col)], axis=-1)


def _sweep_queries(fn, q):
    B, L = q.shape[:2]
    nb = L // Q_BLOCK
    qb = jnp.moveaxis(q.reshape((B, nb, Q_BLOCK) + q.shape[2:]), 1, 0)
    out = jnp.moveaxis(lax.map(fn, qb), 0, 1)
    return out.reshape((B, L) + out.shape[3:])


def _diff_attn_block(q, k, v, lam):
    s = jnp.einsum('bqhmd,bkhmd->bhmqk', q, k).astype(jnp.float32) * (DIFF_HEAD_DIM ** -0.5)
    p = jax.nn.softmax(s, axis=-1)
    w = (p[:, :, 0] - lam * p[:, :, 1]).astype(v.dtype)
    return jnp.einsum('bhqk,bkhe->bqhe', w, v)


def _gqa_block(q, k, v, scale):
    s = jnp.einsum('bqhgd,bkhd->bhgqk', q, k).astype(jnp.float32) * scale
    p = jax.nn.softmax(s, axis=-1).astype(v.dtype)
    return jnp.einsum('bhgqk,bkhe->bqhge', p, v)


def _branch_inputs(h, angs, p):
    B, L = h.shape[:2]
    z = h @ p['w_in']
    dq, dk, dv, gq, gk, gv, mcq, mckv, mkr, gt = jnp.split(z, IN_SPLITS, axis=-1)
    dq = dq.reshape(B, L, DIFF_HEADS, 2, DIFF_HEAD_DIM)
    dk = dk.reshape(B, L, DIFF_HEADS, 2, DIFF_HEAD_DIM)
    dv = dv.reshape(B, L, DIFF_HEADS, DIFF_V_DIM)
    gq = _rmsnorm(gq.reshape(B, L, GQA_KV_HEADS, GQA_GROUP, GQA_HEAD_DIM), p['g_gqa_q'])
    gk = _rmsnorm(gk.reshape(B, L, GQA_KV_HEADS, GQA_HEAD_DIM), p['g_gqa_k'])
    gv = gv.reshape(B, L, GQA_KV_HEADS, GQA_HEAD_DIM)
    mq = (_rmsnorm(mcq, p['g_mla_q']) @ p['w_mla_uq']).reshape(B, L, MLA_HEADS, MLA_QK_DIM)
    mkv = (_rmsnorm(mckv, p['g_mla_kv']) @ p['w_mla_ukv']).reshape(B, L, MLA_HEADS, MLA_NOPE_DIM + MLA_V_DIM)
    mq_nope, mq_rope = jnp.split(mq, [MLA_NOPE_DIM], axis=-1)
    mk_nope, mv = jnp.split(mkv, [MLA_NOPE_DIM], axis=-1)
    if angs is not None:
        angs_qk, angs_mla = angs
        dq = _rope2d(dq, angs_qk)
        dk = _rope2d(dk, angs_qk)
        gq = _rope2d(gq, angs_qk)
        gk = _rope2d(gk, angs_qk)
        mq_rope = _rope2d(mq_rope, angs_mla)
        mkr = _rope2d(mkr, angs_mla)
    mq = jnp.concatenate([mq_nope, mq_rope], axis=-1)[:, :, :, None, :]
    mk = jnp.concatenate(
        [mk_nope, jnp.broadcast_to(mkr[:, :, None, :], (B, L, MLA_HEADS, MLA_ROPE_DIM))], axis=-1)
    return (dq, gq, mq), (dk, dv, gk, gv, mk, mv), gt


def _mix_branches(queries, keys, gates_pre, lam, lam_init, p):
    dq, gq, mq = queries
    dk, dv, gk, gv, mk, mv = keys
    B, L = dq.shape[:2]
    o_diff = _sweep_queries(lambda qb: _diff_attn_block(qb, dk, dv, lam), dq)
    o_diff = _rmsnorm(o_diff, p['g_diff_out']) * (1.0 - lam_init)
    o_gqa = _sweep_queries(lambda qb: _gqa_block(qb, gk, gv, GQA_HEAD_DIM ** -0.5), gq)
    o_mla = _sweep_queries(lambda qb: _gqa_block(qb, mk, mv, MLA_QK_DIM ** -0.5), mq)
    gate = jax.nn.sigmoid((gates_pre + p['b_gate']).astype(jnp.float32)).astype(gates_pre.dtype)
    gate = gate.reshape(B, L, N_BRANCHES, D_MODEL)
    y = (gate[:, :, 0] * (o_diff.reshape(B, L, DIFF_WIDTH) @ p['w_br_diff'])
         + gate[:, :, 1] * (o_gqa.reshape(B, L, GQA_WIDTH) @ p['w_br_gqa'])
         + gate[:, :, 2] * (o_mla.reshape(B, L, MLA_WIDTH) @ p['w_br_mla']))
    return y @ p['w_out']


def _swiglu(h, w_in, w_out):
    g, u = jnp.split(h @ w_in, 2, axis=-1)
    return (jax.nn.silu(g) * u) @ w_out


def _lambda_init(layer):
    return 0.8 - 0.6 * math.exp(-0.3 * layer)


def setup_inputs(seed: int = 0) -> dict:
    key = jax.random.key(seed)
    ks = jax.random.split(key, 28)
    f32 = jnp.float32

    def nrm(k, shape):
        return jax.random.normal(k, shape, f32)

    return {
        'x': nrm(ks[0], (BATCH, SEQ, D_MODEL)),
        'c': nrm(ks[1], (BATCH, D_MODEL)),
        'ctx': nrm(ks[2], (BATCH, CTX_LEN, D_MODEL)),
        'c_ctx': nrm(ks[3], (D_MODEL,)),
        'w_mod': nrm(ks[4], (DEPTH, D_MODEL, N_MOD * D_MODEL)) * (0.5 * D_MODEL ** -0.5),
        'b_mod': 0.02 * nrm(ks[5], (DEPTH, N_MOD * D_MODEL)),
        'g_norm1': 1.0 + 0.02 * nrm(ks[6], (DEPTH, D_MODEL)),
        'w_in': nrm(ks[7], (DEPTH, D_MODEL, IN_WIDTH)) * D_MODEL ** -0.5,
        'b_gate': 0.02 * nrm(ks[8], (DEPTH, N_BRANCHES * D_MODEL)),
        'lam_q1': 0.1 * nrm(ks[9], (DEPTH, DIFF_HEAD_DIM)),
        'lam_k1': 0.1 * nrm(ks[10], (DEPTH, DIFF_HEAD_DIM)),
        'lam_q2': 0.1 * nrm(ks[11], (DEPTH, DIFF_HEAD_DIM)),
        'lam_k2': 0.1 * nrm(ks[12], (DEPTH, DIFF_HEAD_DIM)),
        'g_diff_out': 1.0 + 0.02 * nrm(ks[13], (DEPTH, DIFF_V_DIM)),
        'g_gqa_q': 1.0 + 0.02 * nrm(ks[14], (DEPTH, GQA_HEAD_DIM)),
        'g_gqa_k': 1.0 + 0.02 * nrm(ks[15], (DEPTH, GQA_HEAD_DIM)),
        'g_mla_q': 1.0 + 0.02 * nrm(ks[16], (DEPTH, MLA_Q_LORA)),
        'w_mla_uq': nrm(ks[17], (DEPTH, MLA_Q_LORA, MLA_HEADS * MLA_QK_DIM)) * MLA_Q_LORA ** -0.5,
        'g_mla_kv': 1.0 + 0.02 * nrm(ks[18], (DEPTH, MLA_KV_LORA)),
        'w_mla_ukv': nrm(ks[19], (DEPTH, MLA_KV_LORA, MLA_HEADS * (MLA_NOPE_DIM + MLA_V_DIM))) * MLA_KV_LORA ** -0.5,
        'w_br_diff': nrm(ks[20], (DEPTH, DIFF_WIDTH, D_MODEL)) * DIFF_WIDTH ** -0.5,
        'w_br_gqa': nrm(ks[21], (DEPTH, GQA_WIDTH, D_MODEL)) * GQA_WIDTH ** -0.5,
        'w_br_mla': nrm(ks[22], (DEPTH, MLA_WIDTH, D_MODEL)) * MLA_WIDTH ** -0.5,
        'w_out': nrm(ks[23], (DEPTH, D_MODEL, D_MODEL)) * D_MODEL ** -0.5,
        'g_norm2': 1.0 + 0.02 * nrm(ks[24], (DEPTH, D_MODEL)),
        'w_ffn_in': nrm(ks[25], (DEPTH, D_MODEL, 2 * FFN_HIDDEN)) * D_MODEL ** -0.5,
        'w_ffn_out': nrm(ks[26], (DEPTH, FFN_HIDDEN, D_MODEL)) * FFN_HIDDEN ** -0.5,
        'g_final': 1.0 + 0.02 * nrm(ks[27], (D_MODEL,)),
    }


def reference(x, c, ctx, c_ctx, w_mod, b_mod, g_norm1, w_in, b_gate, lam_q1, lam_k1, lam_q2, lam_k2,
              g_diff_out, g_gqa_q, g_gqa_k, g_mla_q, w_mla_uq, g_mla_kv, w_mla_ukv,
              w_br_diff, w_br_gqa, w_br_mla, w_out, g_norm2, w_ffn_in, w_ffn_out, g_final):
    f32 = jnp.float32
    L = x.shape[1]
    rows = L // GRID_W
    row = jnp.repeat(jnp.arange(rows, dtype=jnp.int32), GRID_W)
    col = jnp.tile(jnp.arange(GRID_W, dtype=jnp.int32), rows)
    angs_qk = _rope_angles(row, col, DIFF_HEAD_DIM)
    angs_mla = _rope_angles(row, col, MLA_ROPE_DIM)
    xc = ctx
    for l in range(DEPTH):
        p = {
            'w_in': w_in[l], 'b_gate': b_gate[l], 'g_diff_out': g_diff_out[l],
            'g_gqa_q': g_gqa_q[l], 'g_gqa_k': g_gqa_k[l],
            'g_mla_q': g_mla_q[l], 'w_mla_uq': w_mla_uq[l], 'g_mla_kv': g_mla_kv[l], 'w_mla_ukv': w_mla_ukv[l],
            'w_br_diff': w_br_diff[l], 'w_br_gqa': w_br_gqa[l], 'w_br_mla': w_br_mla[l], 'w_out': w_out[l],
        }
        lam_init = _lambda_init(l)
        lam = (jnp.exp(jnp.sum(lam_q1[l].astype(f32) * lam_k1[l].astype(f32)))
               - jnp.exp(jnp.sum(lam_q2[l].astype(f32) * lam_k2[l].astype(f32))) + lam_init)
        mod = jnp.split((jax.nn.silu(c) @ w_mod[l] + b_mod[l])[:, None, :], N_MOD, axis=-1)
        mod_c = jnp.split((jax.nn.silu(c_ctx) @ w_mod[l] + b_mod[l])[None, None, :], N_MOD, axis=-1)
        h = _modulate(_rmsnorm(x, g_norm1[l]), mod[0], mod[1])
        hc = _modulate(_rmsnorm(xc, g_norm1[l]), mod_c[0], mod_c[1])
        q_lat, k_lat, gt_lat = _branch_inputs(h, (angs_qk, angs_mla), p)
        q_ctx, k_ctx, gt_ctx = _branch_inputs(hc, None, p)
        k_all = tuple(jnp.concatenate([kc, kl], axis=1) for kc, kl in zip(k_ctx, k_lat))
        x = x + mod[2] * _mix_branches(q_lat, k_all, gt_lat, lam, lam_init, p)
        x = x + mod[5] * _swiglu(_modulate(_rmsnorm(x, g_norm2[l]), mod[3], mod[4]), w_ffn_in[l], w_ffn_out[l])
        if l < DEPTH - 1:
            xc = xc + mod_c[2] * _mix_branches(q_ctx, k_ctx, gt_ctx, lam, lam_init, p)
            xc = xc + mod_c[5] * _swiglu(_modulate(_rmsnorm(xc, g_norm2[l]), mod_c[3], mod_c[4]),
                                         w_ffn_in[l], w_ffn_out[l])
    return _rmsnorm(x, g_final)
```

```python
import functools
import math

import numpy as np
import jax
import jax.numpy as jnp
from jax import lax
from jax.experimental import pallas as pl
from jax.experimental.pallas import tpu as pltpu

F32 = jnp.float32
BF16 = jnp.bfloat16

GRID_W = 64
ROPE_THETA = 10000.0
EPS = 1e-6
N_MOD = 6

DIFF_HEADS = 4
HEAD_DIM = 64
GQA_KV_HEADS = 2
GQA_GROUP = 4
MLA_HEADS = 8
MLA_Q_LORA = 384
MLA_KV_LORA = 256
MLA_NOPE = 64
MLA_ROPE = 32
MLA_QK = MLA_NOPE + MLA_ROPE
MLA_V = 64
BRANCH_W = 512
N_UNITS = 4

LANES = 128
TOKEN_TILE = 256
VMEM_LIMIT = 56 * 1024 * 1024

OFF_DQ, OFF_DK, OFF_DV, OFF_GQ, OFF_GK, OFF_GV = 0, 512, 1024, 1536, 2048, 2176
OFF_MCQ, OFF_MCKV, OFF_MKR, W1_WIDTH = 2304, 2688, 2944, 3072


def _lambda_init(layer):
    return 0.8 - 0.6 * math.exp(-0.3 * layer)


def _rope_perm(dim):
    q = dim // 4
    return np.concatenate([np.arange(0, q), np.arange(2 * q, 3 * q), np.arange(q, 2 * q), np.arange(3 * q, 4 * q)])


_PERM64 = _rope_perm(HEAD_DIM)
_PERM32 = _rope_perm(MLA_ROPE)


def _main_columns():
    diff_qk = np.concatenate([h * 128 + m * 64 + _PERM64 for h in range(DIFF_HEADS) for m in range(2)])
    gq = np.concatenate([hkv * 256 + i * 64 + _PERM64 for i in range(GQA_GROUP) for hkv in range(GQA_KV_HEADS)])
    gk = np.concatenate([hkv * 64 + _PERM64 for hkv in range(GQA_KV_HEADS)])
    return np.concatenate([
        0 + diff_qk, 512 + diff_qk, 1024 + np.arange(512),
        1536 + gq, 2048 + gk, 2176 + np.arange(128),
        2304 + np.arange(MLA_Q_LORA), 2688 + np.arange(MLA_KV_LORA)])


_MAIN_COLS = _main_columns()
_GQA_OUT_ROWS = np.concatenate(
    [hkv * 256 + i * 64 + np.arange(64) for i in range(GQA_GROUP) for hkv in range(GQA_KV_HEADS)])
_IN_MKR = 2944
_IN_GATE = 2976


def _prep_layer_weights(w_in, w_mla_uq, w_mla_ukv, w_br_gqa):
    d = w_in.shape[0]
    mkr = jnp.pad(w_in[:, _IN_MKR + _PERM32], ((0, 0), (MLA_NOPE, LANES - MLA_QK)))
    w1 = jnp.concatenate([w_in[:, _MAIN_COLS], mkr], axis=1).astype(BF16)
    wg = w_in[:, _IN_GATE:].astype(BF16)
    uq = w_mla_uq.reshape(MLA_Q_LORA, MLA_HEADS, MLA_QK)
    uq = uq[:, :, np.concatenate([np.arange(MLA_NOPE), MLA_NOPE + _PERM32])]
    uq = jnp.pad(uq, ((0, 0), (0, 0), (0, LANES - MLA_QK))).reshape(MLA_Q_LORA, MLA_HEADS * LANES).astype(BF16)
    ukv = w_mla_ukv.reshape(MLA_KV_LORA, MLA_HEADS, MLA_NOPE + MLA_V)
    ukn = jnp.pad(ukv[:, :, :MLA_NOPE], ((0, 0), (0, 0), (0, LANES - MLA_NOPE))).reshape(MLA_KV_LORA, -1)
    uv = ukv[:, :, MLA_NOPE:].reshape(MLA_KV_LORA, -1)
    ukv2 = jnp.concatenate([ukn, uv], axis=1).astype(BF16)
    wbg = w_br_gqa[_GQA_OUT_ROWS].astype(BF16)
    del d
    return w1, wg, uq, ukv2, wbg


def _rope_tables(seq, n_ctx):
    rows = seq // GRID_W
    row = jnp.repeat(jnp.arange(rows, dtype=jnp.int32), GRID_W).astype(F32)
    col = jnp.tile(jnp.arange(GRID_W, dtype=jnp.int32), rows).astype(F32)

    def table(dim):
        a = dim // 2
        freqs = ROPE_THETA ** (-jnp.arange(0, a, 2, dtype=F32) / a)
        ang_r, ang_c = row[:, None] * freqs, col[:, None] * freqs
        ang = jnp.concatenate([ang_r, ang_c], axis=-1)
        cos = jnp.concatenate([jnp.cos(ang), jnp.cos(ang)], axis=-1)
        sin = jnp.concatenate([-jnp.sin(ang), jnp.sin(ang)], axis=-1)
        return cos, sin

    c64, s64 = table(HEAD_DIM)
    c64, s64 = jnp.tile(c64, (1, 2)), jnp.tile(s64, (1, 2))
    c32, s32 = table(MLA_ROPE)
    pad = ((0, 0), (MLA_NOPE, LANES - MLA_QK))
    c32 = jnp.pad(c32, pad, constant_values=1.0)
    s32 = jnp.pad(s32, pad)
    ctx_pad = ((n_ctx, 0), (0, 0))
    return (jnp.pad(c64, ctx_pad, constant_values=1.0), jnp.pad(s64, ctx_pad),
            jnp.pad(c32, ctx_pad, constant_values=1.0), jnp.pad(s32, ctx_pad))


def _rms(x):
    return x * lax.rsqrt(jnp.mean(x * x, axis=-1, keepdims=True) + EPS)


def _adaln(x, g, shift, scale):
    return (_rms(x) * g) * (1.0 + scale) + shift


def _rope(z, cos, sin, is_x1, shift):
    partner = jnp.where(is_x1, pltpu.roll(z, LANES - shift, 1), pltpu.roll(z, shift, 1))
    return z * cos + partner * sin


def _half_rms(z, lo, g):
    sq = z * z
    zero = jnp.zeros_like(sq)
    s_lo = jnp.sum(jnp.where(lo, sq, zero), axis=-1, keepdims=True)
    s_hi = jnp.sum(jnp.where(lo, zero, sq), axis=-1, keepdims=True)
    ms = jnp.where(lo, s_lo, s_hi) * (1.0 / HEAD_DIM)
    return z * lax.rsqrt(ms + EPS) * g


def _dot(a, b):
    return jnp.dot(a, b, preferred_element_type=F32)


def _mod_kernel(c_ref, w_ref, b_ref, o_ref):
    cc = c_ref[...]
    a = (cc * jax.nn.sigmoid(cc)).astype(BF16)
    o_ref[...] = _dot(a, w_ref[...].astype(BF16)) + b_ref[...]


def _mod_call(cc, w_mod, b_mod):
    depth, d, n = w_mod.shape
    tn = 1536
    rows = cc.shape[0]
    return pl.pallas_call(
        _mod_kernel,
        grid=(depth, n // tn),
        in_specs=[
            pl.BlockSpec((rows, d), lambda l, j: (0, 0)),
            pl.BlockSpec((None, d, tn), lambda l, j: (l, 0, j)),
            pl.BlockSpec((None, 1, tn), lambda l, j: (l, 0, j)),
        ],
        out_specs=pl.BlockSpec((None, rows, tn), lambda l, j: (l, 0, j)),
        out_shape=jax.ShapeDtypeStruct((depth, rows, n), F32),
        compiler_params=pltpu.CompilerParams(vmem_limit_bytes=VMEM_LIMIT),
        name="mod_vectors",
    )(cc, w_mod, b_mod.reshape(depth, 1, n))


def _proj_kernel(x_ref, mod_ref, g1_ref, w1_ref, wuq_ref, wukv_ref, gq_g_ref, gk_g_ref, gmq_ref, gmkv_ref,
                 c64_ref, s64_ref, c32_ref, s32_ref,
                 dq_o, dk_o, dv_o, gq_o, gk_o, gv_o, mq_o, mk_o, mv_o):
    x = x_ref[...]
    hb = _adaln(x, g1_ref[...], mod_ref[0:1, :], mod_ref[1:2, :]).astype(BF16)
    tm = x.shape[0]
    lane = lax.broadcasted_iota(jnp.int32, (tm, LANES), 1)
    lo = lane < HEAD_DIM
    x1_64 = (lane & (HEAD_DIM // 2)) == 0
    x1_32 = (lane >= MLA_NOPE) & (lane < MLA_NOPE + MLA_ROPE // 2)
    c64, s64 = c64_ref[...], s64_ref[...]
    c32, s32 = c32_ref[...], s32_ref[...]
    qk_scale = HEAD_DIM ** -0.5

    def seg(off, width):
        return _dot(hb, w1_ref[:, off:off + width])

    for i in range(N_UNITS):
        sl = slice(i * LANES, (i + 1) * LANES)
        dq_o[:, sl] = (_rope(seg(OFF_DQ + i * LANES, LANES), c64, s64, x1_64, 32) * qk_scale).astype(BF16)
        dk_o[:, sl] = _rope(seg(OFF_DK + i * LANES, LANES), c64, s64, x1_64, 32).astype(BF16)
        gq = _half_rms(seg(OFF_GQ + i * LANES, LANES), lo, gq_g_ref[...])
        gq_o[:, sl] = (_rope(gq, c64, s64, x1_64, 32) * qk_scale).astype(BF16)
    dv_o[...] = seg(OFF_DV, BRANCH_W).astype(BF16)
    gk = _half_rms(seg(OFF_GK, LANES), lo, gk_g_ref[...])
    gk_o[...] = _rope(gk, c64, s64, x1_64, 32).astype(BF16)
    gv_o[...] = seg(OFF_GV, LANES).astype(BF16)

    mcq = (_rms(seg(OFF_MCQ, MLA_Q_LORA)) * gmq_ref[...]).astype(BF16)
    mckv = (_rms(seg(OFF_MCKV, MLA_KV_LORA)) * gmkv_ref[...]).astype(BF16)
    mkr = _rope(seg(OFF_MKR, LANES), c32, s32, x1_32, 16)
    mla_scale = MLA_QK ** -0.5
    for h in range(MLA_HEADS):
        sl = slice(h * LANES, (h + 1) * LANES)
        mq = _dot(mcq, wuq_ref[:, sl])
        mq_o[:, sl] = (_rope(mq, c32, s32, x1_32, 16) * mla_scale).astype(BF16)
        mk_o[:, sl] = (_dot(mckv, wukv_ref[:, sl]) + mkr).astype(BF16)
    mv_o[...] = _dot(mckv, wukv_ref[:, MLA_HEADS * LANES:]).astype(BF16)


def _const_spec(shape):
    return pl.BlockSpec(shape, lambda *_: (0,) * len(shape))


def _proj_call(xall, modarr, g1, w1, wuq, wukv, gq_g, gk_g, gmq, gmkv, tables, n_ctx_tiles):
    b, t, d = xall.shape
    tm = TOKEN_TILE
    tok = lambda w: pl.BlockSpec((None, tm, w), lambda bi, ti: (bi, ti, 0))
    tab = pl.BlockSpec((tm, LANES), lambda bi, ti: (ti, 0))
    widths = (BRANCH_W, BRANCH_W, BRANCH_W, BRANCH_W, LANES, LANES, MLA_HEADS * LANES, MLA_HEADS * LANES, BRANCH_W)
    return pl.pallas_call(
        _proj_kernel,
        grid=(b, t // tm),
        in_specs=[
            tok(d),
            pl.BlockSpec((None, None, N_MOD, d), lambda bi, ti: (bi, (ti >= n_ctx_tiles).astype(jnp.int32), 0, 0)),
            _const_spec((1, d)), _const_spec(w1.shape), _const_spec(wuq.shape), _const_spec(wukv.shape),
            _const_spec((1, LANES)), _const_spec((1, LANES)),
            _const_spec((1, MLA_Q_LORA)), _const_spec((1, MLA_KV_LORA)),
            tab, tab, tab, tab,
        ],
        out_specs=[tok(w) for w in widths],
        out_shape=[jax.ShapeDtypeStruct((b, t, w), BF16) for w in widths],
        compiler_params=pltpu.CompilerParams(
            dimension_semantics=("parallel", "arbitrary"), vmem_limit_bytes=VMEM_LIMIT),
        name="input_projection",
    )(xall, modarr, g1, w1, wuq, wukv, gq_g, gk_g, gmq, gmkv, *tables)


def _attn_kernel(*refs, kind, n_ctx, lam_init, first_tile_is_ctx):
    if kind == "diff":
        q_ref, k_ref, v_ref, lam_ref, gdo_ref, o_ref = refs
    else:
        q_ref, k_ref, v_ref, o_ref = refs
    tq = q_ref.shape[0]
    lane = lax.broadcasted_iota(jnp.int32, (tq, LANES), 1)
    lo = lane < HEAD_DIM

    def compute(nk):
        q = q_ref[...]
        k = k_ref[0:nk, :]
        v = v_ref[0:nk, :]
        if kind == "mla":
            qa, qb = q[:, :LANES], q[:, LANES:]
            ka, kb = k[:, :LANES], k[:, LANES:]
        else:
            zero = jnp.zeros_like(q)
            qa, qb = jnp.where(lo, q, zero), jnp.where(lo, zero, q)
            ka = kb = k

        def softmax_pv(qx, kx):
            s = lax.dot_general(qx, kx, (((1,), (1,)), ((), ())), preferred_element_type=F32)
            p = jnp.exp(s - jnp.max(s, axis=-1, keepdims=True))
            denom = jnp.sum(p, axis=-1, keepdims=True)
            return _dot(p.astype(BF16), v) / denom

        oa, ob = softmax_pv(qa, ka), softmax_pv(qb, kb)
        if kind == "diff":
            o = oa - lam_ref[...] * ob
            o = _rms(o) * gdo_ref[...] * (1.0 - lam_init)
        else:
            o = jnp.where(lo, oa, ob)
        o_ref[...] = o.astype(BF16)

    n_keys = k_ref.shape[0]
    if first_tile_is_ctx:
        j = pl.program_id(2)
        pl.when(j == 0)(lambda: compute(n_ctx))
        pl.when(j > 0)(lambda: compute(n_keys))
    else:
        compute(n_keys)


def _attn_call(kind, q, k, v, extra, *, n_ctx, lam_init, include_ctx):
    b, t, _ = k.shape
    tq = TOKEN_TILE
    n_ctx_tiles = n_ctx // tq
    q_off = 0 if include_ctx else n_ctx_tiles
    nq = t // tq - q_off
    qw = q.shape[-1] // N_UNITS
    kw = LANES if kind == "gqa" else k.shape[-1] // N_UNITS
    shared_kv = kind == "gqa"
    kv_map = (lambda bi, u, j: (bi, 0, 0)) if shared_kv else (lambda bi, u, j: (bi, 0, u))
    in_specs = [
        pl.BlockSpec((None, tq, qw), lambda bi, u, j: (bi, j + q_off, u)),
        pl.BlockSpec((None, t, kw), kv_map),
        pl.BlockSpec((None, t, LANES), kv_map),
    ]
    if kind == "diff":
        in_specs += [_const_spec((1, LANES)), _const_spec((1, LANES))]
    return pl.pallas_call(
        functools.partial(_attn_kernel, kind=kind, n_ctx=n_ctx, lam_init=lam_init,
                          first_tile_is_ctx=include_ctx),
        grid=(b, N_UNITS, nq),
        in_specs=in_specs,
        out_specs=pl.BlockSpec((None, tq, LANES), lambda bi, u, j: (bi, j, u)),
        out_shape=jax.ShapeDtypeStruct((b, nq * tq, BRANCH_W), BF16),
        compiler_params=pltpu.CompilerParams(
            dimension_semantics=("parallel", "arbitrary", "arbitrary"), vmem_limit_bytes=VMEM_LIMIT),
        name="attention_" + kind,
    )(q, k, v, *extra)


def _merge_kernel(x_ref, mod_ref, g1_ref, wg_ref, bg_ref, od_ref, og_ref, om_ref,
                  wbd_ref, wbg_ref, wbm_ref, wout_ref, xo_ref):
    x = x_ref[...]
    d = x.shape[-1]
    hb = _adaln(x, g1_ref[...], mod_ref[0:1, :], mod_ref[1:2, :]).astype(BF16)
    y = None
    for i, (o_ref, wb_ref) in enumerate(((od_ref, wbd_ref), (og_ref, wbg_ref), (om_ref, wbm_ref))):
        sl = slice(i * d, (i + 1) * d)
        gate = jax.nn.sigmoid(_dot(hb, wg_ref[:, sl]) + bg_ref[:, sl])
        term = gate * _dot(o_ref[...], wb_ref[...])
        y = term if y is None else y + term
    out = _dot(y.astype(BF16), wout_ref[...])
    xo_ref[...] = x + mod_ref[2:3, :] * out


def _mod_spec(d, pos_off, n_ctx_tiles):
    return pl.BlockSpec((None, None, N_MOD, d),
                        lambda bi, ti: (bi, (ti + pos_off >= n_ctx_tiles).astype(jnp.int32), 0, 0))


def _merge_call(xall, modarr, g1, wg, bg, od, og, om, wbd, wbg, wbm, wout, *, x_off, n_ctx_tiles):
    b, _, d = xall.shape
    tm = TOKEN_TILE
    nt = od.shape[1] // tm
    tok = lambda w: pl.BlockSpec((None, tm, w), lambda bi, ti: (bi, ti, 0))
    return pl.pallas_call(
        _merge_kernel,
        grid=(b, nt),
        in_specs=[
            pl.BlockSpec((None, tm, d), lambda bi, ti: (bi, ti + x_off, 0)),
            _mod_spec(d, x_off, n_ctx_tiles),
            _const_spec((1, d)), _const_spec(wg.shape), _const_spec(bg.shape),
            tok(BRANCH_W), tok(BRANCH_W), tok(BRANCH_W),
            _const_spec(wbd.shape), _const_spec(wbg.shape), _const_spec(wbm.shape), _const_spec(wout.shape),
        ],
        out_specs=tok(d),
        out_shape=jax.ShapeDtypeStruct((b, nt * tm, d), F32),
        compiler_params=pltpu.CompilerParams(
            dimension_semantics=("parallel", "arbitrary"), vmem_limit_bytes=VMEM_LIMIT),
        name="branch_merge",
    )(xall, modarr, g1, wg, bg, od, og, om, wbd, wbg, wbm, wout)


def _ffn_kernel(*refs, final):
    if final:
        x_ref, mod_ref, g2_ref, win_ref, wout_ref, gf_ref, o_ref = refs
    else:
        x_ref, mod_ref, g2_ref, win_ref, wout_ref, o_ref = refs
    x = x_ref[...]
    hidden = wout_ref.shape[0]
    hb = _adaln(x, g2_ref[...], mod_ref[3:4, :], mod_ref[4:5, :]).astype(BF16)
    g = _dot(hb, win_ref[:, :hidden])
    u = _dot(hb, win_ref[:, hidden:])
    a = (g * jax.nn.sigmoid(g) * u).astype(BF16)
    xn = x + mod_ref[5:6, :] * _dot(a, wout_ref[...])
    if final:
        xn = _rms(xn) * gf_ref[...]
    o_ref[...] = xn


def _ffn_call(xin, modarr, g2, win, wout, gf, *, pos_off, n_ctx_tiles):
    b, t, d = xin.shape
    tm = TOKEN_TILE
    tok = pl.BlockSpec((None, tm, d), lambda bi, ti: (bi, ti, 0))
    final = gf is not None
    in_specs = [tok, _mod_spec(d, pos_off, n_ctx_tiles), _const_spec((1, d)),
                _const_spec(win.shape), _const_spec(wout.shape)]
    args = [xin, modarr, g2, win, wout]
    if final:
        in_specs.append(_const_spec((1, d)))
        args.append(gf)
    return pl.pallas_call(
        functools.partial(_ffn_kernel, final=final),
        grid=(b, t // tm),
        in_specs=in_specs,
        out_specs=tok,
        out_shape=jax.ShapeDtypeStruct((b, t, d), F32),
        compiler_params=pltpu.CompilerParams(
            dimension_semantics=("parallel", "arbitrary"), vmem_limit_bytes=VMEM_LIMIT),
        name="swiglu",
    )(*args)


def kernel(x, c, ctx, c_ctx, w_mod, b_mod, g_norm1, w_in, b_gate, lam_q1, lam_k1, lam_q2, lam_k2, g_diff_out, g_gqa_q, g_gqa_k, g_mla_q, w_mla_uq, g_mla_kv, w_mla_ukv, w_br_diff, w_br_gqa, w_br_mla, w_out, g_norm2, w_ffn_in, w_ffn_out, g_final):
    b, seq, d = x.shape
    n_ctx = ctx.shape[1]
    depth = w_in.shape[0]
    assert n_ctx % TOKEN_TILE == 0 and seq % TOKEN_TILE == 0 and seq % GRID_W == 0
    n_ctx_tiles = n_ctx // TOKEN_TILE

    mod_rows = 32
    cc = jnp.zeros((mod_rows, d), F32).at[:b].set(c).at[b].set(c_ctx)
    mods = _mod_call(cc, w_mod, b_mod)
    tables = _rope_tables(seq, n_ctx)
    xall = jnp.concatenate([ctx, x], axis=1)

    for l in range(depth):
        last = l == depth - 1
        lam_init = _lambda_init(l)
        lam = (jnp.exp(jnp.sum(lam_q1[l] * lam_k1[l])) - jnp.exp(jnp.sum(lam_q2[l] * lam_k2[l])) + lam_init)
        lam_vec = jnp.full((1, LANES), lam, F32)
        mod_lat = mods[l, :b].reshape(b, N_MOD, d)
        mod_ctx = jnp.broadcast_to(mods[l, b].reshape(1, N_MOD, d), (b, N_MOD, d))
        modarr = jnp.stack([mod_ctx, mod_lat], axis=1)

        w1, wg, wuq, wukv, wbg = _prep_layer_weights(w_in[l], w_mla_uq[l], w_mla_ukv[l], w_br_gqa[l])
        g1 = g_norm1[l].reshape(1, d)
        gq_g = jnp.tile(g_gqa_q[l][_PERM64], 2).reshape(1, LANES)
        gk_g = jnp.tile(g_gqa_k[l][_PERM64], 2).reshape(1, LANES)

        dq, dk, dv, gq, gk, gv, mq, mk, mv = _proj_call(
            xall, modarr, g1, w1, wuq, wukv, gq_g, gk_g,
            g_mla_q[l].reshape(1, -1), g_mla_kv[l].reshape(1, -1), tables, n_ctx_tiles)

        attn = functools.partial(_attn_call, n_ctx=n_ctx, lam_init=lam_init, include_ctx=not last)
        od = attn("diff", dq, dk, dv, (lam_vec, g_diff_out[l].reshape(1, LANES)))
        og = attn("gqa", gq, gk, gv, ())
        om = attn("mla", mq, mk, mv, ())

        x_off = n_ctx_tiles if last else 0
        xmid = _merge_call(
            xall, modarr, g1, wg, b_gate[l].reshape(1, -1), od, og, om,
            w_br_diff[l].astype(BF16), wbg, w_br_mla[l].astype(BF16), w_out[l].astype(BF16),
            x_off=x_off, n_ctx_tiles=n_ctx_tiles)
        xall = _ffn_call(
            xmid, modarr, g_norm2[l].reshape(1, d), w_ffn_in[l].astype(BF16), w_ffn_out[l].astype(BF16),
            g_final.reshape(1, d) if last else None, pos_off=x_off, n_ctx_tiles=n_ctx_tiles)
    return xall
```

```python
import functools
import math

import numpy as np
import jax
import jax.numpy as jnp
from jax import lax
from jax.experimental import pallas as pl
from jax.experimental.pallas import tpu as pltpu

F32 = jnp.float32
BF16 = jnp.bfloat16

GRID_W = 64
ROPE_THETA = 10000.0
EPS = 1e-6
N_MOD = 6

DIFF_HEADS = 4
HEAD_DIM = 64
GQA_KV_HEADS = 2
GQA_GROUP = 4
MLA_HEADS = 8
MLA_Q_LORA = 384
MLA_KV_LORA = 256
MLA_NOPE = 64
MLA_ROPE = 32
MLA_QK = MLA_NOPE + MLA_ROPE
MLA_V = 64
BRANCH_W = 512
N_UNITS = 4

LANES = 128
UNITS_PER_STEP = 2
SUM_ROWS = 16
LOG2E = math.log2(math.e)
TOKEN_TILE = 256
VMEM_LIMIT = 56 * 1024 * 1024

OFF_DQ, OFF_DK, OFF_DV, OFF_GQ, OFF_GK, OFF_GV = 0, 512, 1024, 1536, 2048, 2176
OFF_MCQ, OFF_MCKV, OFF_MKR, W1_WIDTH = 2304, 2688, 2944, 3072


def _lambda_init(layer):
    return 0.8 - 0.6 * math.exp(-0.3 * layer)


def _rope_perm(dim):
    q = dim // 4
    return np.concatenate([np.arange(0, q), np.arange(2 * q, 3 * q), np.arange(q, 2 * q), np.arange(3 * q, 4 * q)])


_PERM64 = _rope_perm(HEAD_DIM)
_PERM32 = _rope_perm(MLA_ROPE)


def _main_columns():
    diff_qk = np.concatenate([h * 128 + m * 64 + _PERM64 for h in range(DIFF_HEADS) for m in range(2)])
    gq = np.concatenate([hkv * 256 + i * 64 + _PERM64 for i in range(GQA_GROUP) for hkv in range(GQA_KV_HEADS)])
    gk = np.concatenate([hkv * 64 + _PERM64 for hkv in range(GQA_KV_HEADS)])
    return np.concatenate([
        0 + diff_qk, 512 + diff_qk, 1024 + np.arange(512),
        1536 + gq, 2048 + gk, 2176 + np.arange(128),
        2304 + np.arange(MLA_Q_LORA), 2688 + np.arange(MLA_KV_LORA)])


_MAIN_COLS = _main_columns()
_GQA_OUT_ROWS = np.concatenate(
    [hkv * 256 + i * 64 + np.arange(64) for i in range(GQA_GROUP) for hkv in range(GQA_KV_HEADS)])
_IN_MKR = 2944
_IN_GATE = 2976


def _prep_layer_weights(w_in, w_mla_uq, w_mla_ukv, w_br_gqa):
    d = w_in.shape[0]
    mkr = jnp.pad(w_in[:, _IN_MKR + _PERM32], ((0, 0), (MLA_NOPE, LANES - MLA_QK)))
    w1 = jnp.concatenate([w_in[:, _MAIN_COLS], mkr], axis=1).astype(BF16)
    wg = w_in[:, _IN_GATE:].astype(BF16)
    uq = w_mla_uq.reshape(MLA_Q_LORA, MLA_HEADS, MLA_QK)
    uq = uq[:, :, np.concatenate([np.arange(MLA_NOPE), MLA_NOPE + _PERM32])]
    uq = jnp.pad(uq, ((0, 0), (0, 0), (0, LANES - MLA_QK))).reshape(MLA_Q_LORA, MLA_HEADS * LANES).astype(BF16)
    ukv = w_mla_ukv.reshape(MLA_KV_LORA, MLA_HEADS, MLA_NOPE + MLA_V)
    ukn = jnp.pad(ukv[:, :, :MLA_NOPE], ((0, 0), (0, 0), (0, LANES - MLA_NOPE))).reshape(MLA_KV_LORA, -1)
    uv = ukv[:, :, MLA_NOPE:].reshape(MLA_KV_LORA, -1)
    ukv2 = jnp.concatenate([ukn, uv], axis=1).astype(BF16)
    wbg = w_br_gqa[_GQA_OUT_ROWS].astype(BF16)
    del d
    return w1, wg, uq, ukv2, wbg


def _rope_tables(seq, n_ctx):
    rows = seq // GRID_W
    row = jnp.repeat(jnp.arange(rows, dtype=jnp.int32), GRID_W).astype(F32)
    col = jnp.tile(jnp.arange(GRID_W, dtype=jnp.int32), rows).astype(F32)

    def table(dim):
        a = dim // 2
        freqs = ROPE_THETA ** (-jnp.arange(0, a, 2, dtype=F32) / a)
        ang_r, ang_c = row[:, None] * freqs, col[:, None] * freqs
        ang = jnp.concatenate([ang_r, ang_c], axis=-1)
        cos = jnp.concatenate([jnp.cos(ang), jnp.cos(ang)], axis=-1)
        sin = jnp.concatenate([-jnp.sin(ang), jnp.sin(ang)], axis=-1)
        return cos, sin

    c64, s64 = table(HEAD_DIM)
    c64, s64 = jnp.tile(c64, (1, 2)), jnp.tile(s64, (1, 2))
    c32, s32 = table(MLA_ROPE)
    pad = ((0, 0), (MLA_NOPE, LANES - MLA_QK))
    c32 = jnp.pad(c32, pad, constant_values=1.0)
    s32 = jnp.pad(s32, pad)
    ctx_pad = ((n_ctx, 0), (0, 0))
    return (jnp.pad(c64, ctx_pad, constant_values=1.0), jnp.pad(s64, ctx_pad),
            jnp.pad(c32, ctx_pad, constant_values=1.0), jnp.pad(s32, ctx_pad))


def _rms(x):
    return x * lax.rsqrt(jnp.mean(x * x, axis=-1, keepdims=True) + EPS)


def _adaln(x, g, shift, scale):
    return (_rms(x) * g) * (1.0 + scale) + shift


def _rope(z, cos, sin, is_x1, shift):
    partner = jnp.where(is_x1, pltpu.roll(z, LANES - shift, 1), pltpu.roll(z, shift, 1))
    return z * cos + partner * sin


def _half_rms(z, lo, g):
    sq = z * z
    zero = jnp.zeros_like(sq)
    s_lo = jnp.sum(jnp.where(lo, sq, zero), axis=-1, keepdims=True)
    s_hi = jnp.sum(jnp.where(lo, zero, sq), axis=-1, keepdims=True)
    ms = jnp.where(lo, s_lo, s_hi) * (1.0 / HEAD_DIM)
    return z * lax.rsqrt(ms + EPS) * g


def _dot(a, b):
    return jnp.dot(a, b, preferred_element_type=F32)


def _mod_kernel(c_ref, w_ref, b_ref, o_ref):
    cc = c_ref[...]
    a = (cc * jax.nn.sigmoid(cc)).astype(BF16)
    o_ref[...] = _dot(a, w_ref[...].astype(BF16)) + b_ref[...]


def _mod_call(cc, w_mod, b_mod):
    depth, d, n = w_mod.shape
    tn = 1536
    rows = cc.shape[0]
    return pl.pallas_call(
        _mod_kernel,
        grid=(depth, n // tn),
        in_specs=[
            pl.BlockSpec((rows, d), lambda l, j: (0, 0)),
            pl.BlockSpec((None, d, tn), lambda l, j: (l, 0, j)),
            pl.BlockSpec((None, 1, tn), lambda l, j: (l, 0, j)),
        ],
        out_specs=pl.BlockSpec((None, rows, tn), lambda l, j: (l, 0, j)),
        out_shape=jax.ShapeDtypeStruct((depth, rows, n), F32),
        compiler_params=pltpu.CompilerParams(vmem_limit_bytes=VMEM_LIMIT),
        name="mod_vectors",
    )(cc, w_mod, b_mod.reshape(depth, 1, n))


def _proj_kernel(x_ref, mod_ref, g1_ref, w1_ref, wuq_ref, wukv_ref, gq_g_ref, gk_g_ref, gmq_ref, gmkv_ref,
                 c64_ref, s64_ref, c32_ref, s32_ref,
                 dq_o, dk_o, dv_o, gq_o, gk_o, gv_o, mq_o, mk_o, mv_o):
    x = x_ref[...]
    hb = _adaln(x, g1_ref[...], mod_ref[0:1, :], mod_ref[1:2, :]).astype(BF16)
    tm = x.shape[0]
    lane = lax.broadcasted_iota(jnp.int32, (tm, LANES), 1)
    lo = lane < HEAD_DIM
    x1_64 = (lane & (HEAD_DIM // 2)) == 0
    x1_32 = (lane >= MLA_NOPE) & (lane < MLA_NOPE + MLA_ROPE // 2)
    c64, s64 = c64_ref[...], s64_ref[...]
    c32, s32 = c32_ref[...], s32_ref[...]
    qk_scale = HEAD_DIM ** -0.5 * LOG2E

    def seg(off, width):
        return _dot(hb, w1_ref[:, off:off + width])

    for i in range(N_UNITS):
        sl = slice(i * LANES, (i + 1) * LANES)
        dq_o[:, sl] = (_rope(seg(OFF_DQ + i * LANES, LANES), c64, s64, x1_64, 32) * qk_scale).astype(BF16)
        dk_o[:, sl] = _rope(seg(OFF_DK + i * LANES, LANES), c64, s64, x1_64, 32).astype(BF16)
        gq = _half_rms(seg(OFF_GQ + i * LANES, LANES), lo, gq_g_ref[...])
        gq_o[:, sl] = (_rope(gq, c64, s64, x1_64, 32) * qk_scale).astype(BF16)
    dv_o[...] = seg(OFF_DV, BRANCH_W).astype(BF16)
    gk = _half_rms(seg(OFF_GK, LANES), lo, gk_g_ref[...])
    gk_o[...] = _rope(gk, c64, s64, x1_64, 32).astype(BF16)
    gv_o[...] = seg(OFF_GV, LANES).astype(BF16)

    mcq = (_rms(seg(OFF_MCQ, MLA_Q_LORA)) * gmq_ref[...]).astype(BF16)
    mckv = (_rms(seg(OFF_MCKV, MLA_KV_LORA)) * gmkv_ref[...]).astype(BF16)
    mkr = _rope(seg(OFF_MKR, LANES), c32, s32, x1_32, 16)
    mla_scale = MLA_QK ** -0.5 * LOG2E
    for h in range(MLA_HEADS):
        sl = slice(h * LANES, (h + 1) * LANES)
        mq = _dot(mcq, wuq_ref[:, sl])
        mq_o[:, sl] = (_rope(mq, c32, s32, x1_32, 16) * mla_scale).astype(BF16)
        mk_o[:, sl] = (_dot(mckv, wukv_ref[:, sl]) + mkr).astype(BF16)
    mv_o[...] = _dot(mckv, wukv_ref[:, MLA_HEADS * LANES:]).astype(BF16)


def _const_spec(shape):
    return pl.BlockSpec(shape, lambda *_: (0,) * len(shape))


def _proj_call(xall, modarr, g1, w1, wuq, wukv, gq_g, gk_g, gmq, gmkv, tables, n_ctx_tiles):
    b, t, d = xall.shape
    tm = TOKEN_TILE
    tok = lambda w: pl.BlockSpec((None, tm, w), lambda bi, ti: (bi, ti, 0))
    tab = pl.BlockSpec((tm, LANES), lambda bi, ti: (ti, 0))
    widths = (BRANCH_W, BRANCH_W, BRANCH_W, BRANCH_W, LANES, LANES, MLA_HEADS * LANES, MLA_HEADS * LANES, BRANCH_W)
    return pl.pallas_call(
        _proj_kernel,
        grid=(b, t // tm),
        in_specs=[
            tok(d),
            pl.BlockSpec((None, None, N_MOD, d), lambda bi, ti: (bi, (ti >= n_ctx_tiles).astype(jnp.int32), 0, 0)),
            _const_spec((1, d)), _const_spec(w1.shape), _const_spec(wuq.shape), _const_spec(wukv.shape),
            _const_spec((1, LANES)), _const_spec((1, LANES)),
            _const_spec((1, MLA_Q_LORA)), _const_spec((1, MLA_KV_LORA)),
            tab, tab, tab, tab,
        ],
        out_specs=[tok(w) for w in widths],
        out_shape=[jax.ShapeDtypeStruct((b, t, w), BF16) for w in widths],
        compiler_params=pltpu.CompilerParams(
            dimension_semantics=("parallel", "arbitrary"), vmem_limit_bytes=VMEM_LIMIT),
        name="input_projection",
    )(xall, modarr, g1, w1, wuq, wukv, gq_g, gk_g, gmq, gmkv, *tables)


def _attn_kernel(*refs, kind, n_ctx, lam_init, first_tile_is_ctx, upb):
    if kind == "diff":
        q_ref, k_ref, v_ref, lam_ref, gdo_ref, o_ref, vt_ref = refs
    else:
        q_ref, k_ref, v_ref, o_ref, vt_ref = refs
    tq = q_ref.shape[0]
    lane = lax.broadcasted_iota(jnp.int32, (tq, LANES), 1)
    lo = lane < HEAD_DIM
    n_keys = k_ref.shape[0]
    n_kv = vt_ref.shape[0]
    qw = q_ref.shape[1] // upb
    kw = k_ref.shape[1] // n_kv

    @pl.when(pl.program_id(2) == 0)
    def _():
        for i in range(n_kv):
            vt_ref[i, 0:LANES, :] = v_ref[:, i * LANES:(i + 1) * LANES].astype(F32).T.astype(BF16)
            vt_ref[i, LANES:, :] = jnp.ones((SUM_ROWS, n_keys), BF16)

    def scores(u, nk):
        q = q_ref[:, u * qw:(u + 1) * qw]
        kv = u % n_kv
        k = k_ref[0:nk, kv * kw:(kv + 1) * kw]
        if kind == "mla":
            qa, qb = q[:, :LANES], q[:, LANES:]
            ka, kb = k[:, :LANES], k[:, LANES:]
        else:
            zero = jnp.zeros_like(q)
            qa, qb = jnp.where(lo, q, zero), jnp.where(lo, zero, q)
            ka = kb = k
        nt = (((1,), (1,)), ((), ()))
        return (lax.dot_general(ka, qa, nt, preferred_element_type=F32),
                lax.dot_general(kb, qb, nt, preferred_element_type=F32))

    def softmax_pv(u, nk, st):
        p = jnp.exp2((st - jnp.max(st, axis=0, keepdims=True)).astype(BF16))
        ot = _dot(vt_ref[u % n_kv, :, 0:nk], p)
        return ot[0:LANES] / ot[LANES:LANES + 1]

    def finish(u, nk, sa, sb):
        oa, ob = softmax_pv(u, nk, sa), softmax_pv(u, nk, sb)
        if kind == "diff":
            o = (oa - lam_ref[0:1, 0:1] * ob).T
            o = _rms(o) * gdo_ref[...] * (1.0 - lam_init)
        else:
            row = lax.broadcasted_iota(jnp.int32, oa.shape, 0)
            o = jnp.where(row < HEAD_DIM, oa, ob).T
        o_ref[:, u * LANES:(u + 1) * LANES] = o.astype(BF16)

    def compute(nk):
        pending = scores(0, nk)
        for u in range(upb):
            nxt = scores(u + 1, nk) if u + 1 < upb else None
            finish(u, nk, *pending)
            pending = nxt

    if first_tile_is_ctx:
        j = pl.program_id(2)
        pl.when(j == 0)(lambda: compute(n_ctx))
        pl.when(j > 0)(lambda: compute(n_keys))
    else:
        compute(n_keys)


def _attn_call(kind, q, k, v, extra, *, n_ctx, lam_init, include_ctx):
    b, t, _ = k.shape
    tq = TOKEN_TILE
    upb = UNITS_PER_STEP
    n_ctx_tiles = n_ctx // tq
    q_off = 0 if include_ctx else n_ctx_tiles
    nq = t // tq - q_off
    qw = q.shape[-1] // N_UNITS
    shared_kv = kind == "gqa"
    n_kv = 1 if shared_kv else upb
    kw = LANES if shared_kv else k.shape[-1] // N_UNITS
    kv_map = (lambda bi, g, j: (bi, 0, 0)) if shared_kv else (lambda bi, g, j: (bi, 0, g))
    in_specs = [
        pl.BlockSpec((None, tq, upb * qw), lambda bi, g, j: (bi, j + q_off, g)),
        pl.BlockSpec((None, t, n_kv * kw), kv_map),
        pl.BlockSpec((None, t, n_kv * LANES), kv_map),
    ]
    if kind == "diff":
        in_specs += [_const_spec((1, LANES)), _const_spec((1, LANES))]
    return pl.pallas_call(
        functools.partial(_attn_kernel, kind=kind, n_ctx=n_ctx, lam_init=lam_init,
                          first_tile_is_ctx=include_ctx, upb=upb),
        grid=(b, N_UNITS // upb, nq),
        in_specs=in_specs,
        out_specs=pl.BlockSpec((None, tq, upb * LANES), lambda bi, g, j: (bi, j, g)),
        out_shape=jax.ShapeDtypeStruct((b, nq * tq, BRANCH_W), BF16),
        scratch_shapes=[pltpu.VMEM((n_kv, LANES + SUM_ROWS, t), BF16)],
        compiler_params=pltpu.CompilerParams(
            dimension_semantics=("parallel", "arbitrary", "arbitrary"), vmem_limit_bytes=VMEM_LIMIT),
        name="attention_" + kind,
    )(q, k, v, *extra)


def _merge_kernel(x_ref, mod_ref, g1_ref, wg_ref, bg_ref, od_ref, og_ref, om_ref,
                  wbd_ref, wbg_ref, wbm_ref, wout_ref, xo_ref):
    x = x_ref[...]
    d = x.shape[-1]
    hb = _adaln(x, g1_ref[...], mod_ref[0:1, :], mod_ref[1:2, :]).astype(BF16)
    y = None
    for i, (o_ref, wb_ref) in enumerate(((od_ref, wbd_ref), (og_ref, wbg_ref), (om_ref, wbm_ref))):
        sl = slice(i * d, (i + 1) * d)
        gate = jax.nn.sigmoid(_dot(hb, wg_ref[:, sl]) + bg_ref[:, sl])
        term = gate * _dot(o_ref[...], wb_ref[...])
        y = term if y is None else y + term
    out = _dot(y.astype(BF16), wout_ref[...])
    xo_ref[...] = x + mod_ref[2:3, :] * out


def _mod_spec(d, pos_off, n_ctx_tiles):
    return pl.BlockSpec((None, None, N_MOD, d),
                        lambda bi, ti: (bi, (ti + pos_off >= n_ctx_tiles).astype(jnp.int32), 0, 0))


def _merge_call(xall, modarr, g1, wg, bg, od, og, om, wbd, wbg, wbm, wout, *, x_off, n_ctx_tiles):
    b, _, d = xall.shape
    tm = TOKEN_TILE
    nt = od.shape[1] // tm
    tok = lambda w: pl.BlockSpec((None, tm, w), lambda bi, ti: (bi, ti, 0))
    return pl.pallas_call(
        _merge_kernel,
        grid=(b, nt),
        in_specs=[
            pl.BlockSpec((None, tm, d), lambda bi, ti: (bi, ti + x_off, 0)),
            _mod_spec(d, x_off, n_ctx_tiles),
            _const_spec((1, d)), _const_spec(wg.shape), _const_spec(bg.shape),
            tok(BRANCH_W), tok(BRANCH_W), tok(BRANCH_W),
            _const_spec(wbd.shape), _const_spec(wbg.shape), _const_spec(wbm.shape), _const_spec(wout.shape),
        ],
        out_specs=tok(d),
        out_shape=jax.ShapeDtypeStruct((b, nt * tm, d), F32),
        compiler_params=pltpu.CompilerParams(
            dimension_semantics=("parallel", "arbitrary"), vmem_limit_bytes=VMEM_LIMIT),
        name="branch_merge",
    )(xall, modarr, g1, wg, bg, od, og, om, wbd, wbg, wbm, wout)


def _ffn_kernel(*refs, final):
    if final:
        x_ref, mod_ref, g2_ref, win_ref, wout_ref, gf_ref, o_ref = refs
    else:
        x_ref, mod_ref, g2_ref, win_ref, wout_ref, o_ref = refs
    x = x_ref[...]
    hidden = wout_ref.shape[0]
    hb = _adaln(x, g2_ref[...], mod_ref[3:4, :], mod_ref[4:5, :]).astype(BF16)
    g = _dot(hb, win_ref[:, :hidden])
    u = _dot(hb, win_ref[:, hidden:])
    a = (g * jax.nn.sigmoid(g) * u).astype(BF16)
    xn = x + mod_ref[5:6, :] * _dot(a, wout_ref[...])
    if final:
        xn = _rms(xn) * gf_ref[...]
    o_ref[...] = xn


def _ffn_call(xin, modarr, g2, win, wout, gf, *, pos_off, n_ctx_tiles):
    b, t, d = xin.shape
    tm = TOKEN_TILE
    tok = pl.BlockSpec((None, tm, d), lambda bi, ti: (bi, ti, 0))
    final = gf is not None
    in_specs = [tok, _mod_spec(d, pos_off, n_ctx_tiles), _const_spec((1, d)),
                _const_spec(win.shape), _const_spec(wout.shape)]
    args = [xin, modarr, g2, win, wout]
    if final:
        in_specs.append(_const_spec((1, d)))
        args.append(gf)
    return pl.pallas_call(
        functools.partial(_ffn_kernel, final=final),
        grid=(b, t // tm),
        in_specs=in_specs,
        out_specs=tok,
        out_shape=jax.ShapeDtypeStruct((b, t, d), F32),
        compiler_params=pltpu.CompilerParams(
            dimension_semantics=("parallel", "arbitrary"), vmem_limit_bytes=VMEM_LIMIT),
        name="swiglu",
    )(*args)


def kernel(x, c, ctx, c_ctx, w_mod, b_mod, g_norm1, w_in, b_gate, lam_q1, lam_k1, lam_q2, lam_k2, g_diff_out, g_gqa_q, g_gqa_k, g_mla_q, w_mla_uq, g_mla_kv, w_mla_ukv, w_br_diff, w_br_gqa, w_br_mla, w_out, g_norm2, w_ffn_in, w_ffn_out, g_final):
    b, seq, d = x.shape
    n_ctx = ctx.shape[1]
    depth = w_in.shape[0]
    assert n_ctx % TOKEN_TILE == 0 and seq % TOKEN_TILE == 0 and seq % GRID_W == 0
    n_ctx_tiles = n_ctx // TOKEN_TILE

    mod_rows = 32
    cc = jnp.zeros((mod_rows, d), F32).at[:b].set(c).at[b].set(c_ctx)
    mods = _mod_call(cc, w_mod, b_mod)
    tables = _rope_tables(seq, n_ctx)
    xall = jnp.concatenate([ctx, x], axis=1)

    for l in range(depth):
        last = l == depth - 1
        lam_init = _lambda_init(l)
        lam = (jnp.exp(jnp.sum(lam_q1[l] * lam_k1[l])) - jnp.exp(jnp.sum(lam_q2[l] * lam_k2[l])) + lam_init)
        lam_vec = jnp.full((1, LANES), lam, F32)
        mod_lat = mods[l, :b].reshape(b, N_MOD, d)
        mod_ctx = jnp.broadcast_to(mods[l, b].reshape(1, N_MOD, d), (b, N_MOD, d))
        modarr = jnp.stack([mod_ctx, mod_lat], axis=1)

        w1, wg, wuq, wukv, wbg = _prep_layer_weights(w_in[l], w_mla_uq[l], w_mla_ukv[l], w_br_gqa[l])
        g1 = g_norm1[l].reshape(1, d)
        gq_g = jnp.tile(g_gqa_q[l][_PERM64], 2).reshape(1, LANES)
        gk_g = jnp.tile(g_gqa_k[l][_PERM64], 2).reshape(1, LANES)

        dq, dk, dv, gq, gk, gv, mq, mk, mv = _proj_call(
            xall, modarr, g1, w1, wuq, wukv, gq_g, gk_g,
            g_mla_q[l].reshape(1, -1), g_mla_kv[l].reshape(1, -1), tables, n_ctx_tiles)

        attn = functools.partial(_attn_call, n_ctx=n_ctx, lam_init=lam_init, include_ctx=not last)
        od = attn("diff", dq, dk, dv, (lam_vec, g_diff_out[l].reshape(1, LANES)))
        og = attn("gqa", gq, gk, gv, ())
        om = attn("mla", mq, mk, mv, ())

        x_off = n_ctx_tiles if last else 0
        xmid = _merge_call(
            xall, modarr, g1, wg, b_gate[l].reshape(1, -1), od, og, om,
            w_br_diff[l].astype(BF16), wbg, w_br_mla[l].astype(BF16), w_out[l].astype(BF16),
            x_off=x_off, n_ctx_tiles=n_ctx_tiles)
        xall = _ffn_call(
            xmid, modarr, g_norm2[l].reshape(1, d), w_ffn_in[l].astype(BF16), w_ffn_out[l].astype(BF16),
            g_final.reshape(1, d) if last else None, pos_off=x_off, n_ctx_tiles=n_ctx_tiles)
    return xall
```

```python
import functools
import math

import numpy as np
import jax
import jax.numpy as jnp
from jax import lax
from jax.experimental import pallas as pl
from jax.experimental.pallas import tpu as pltpu

F32 = jnp.float32
BF16 = jnp.bfloat16

GRID_W = 64
ROPE_THETA = 10000.0
EPS = 1e-6
N_MOD = 6

DIFF_HEADS = 4
HEAD_DIM = 64
GQA_KV_HEADS = 2
GQA_GROUP = 4
MLA_HEADS = 8
MLA_Q_LORA = 384
MLA_KV_LORA = 256
MLA_NOPE = 64
MLA_ROPE = 32
MLA_QK = MLA_NOPE + MLA_ROPE
MLA_V = 64
BRANCH_W = 512
N_UNITS = 4

LANES = 128
UNITS_PER_STEP = 4
SUM_ROWS = 16
LOG2E = math.log2(math.e)
TOKEN_TILE = 256
VMEM_LIMIT = 56 * 1024 * 1024

OFF_DQ, OFF_DK, OFF_DV, OFF_GQ, OFF_GK, OFF_GV = 0, 512, 1024, 1536, 2048, 2176
OFF_MCQ, OFF_MCKV, OFF_MKR, W1_WIDTH = 2304, 2688, 2944, 3072


def _lambda_init(layer):
    return 0.8 - 0.6 * math.exp(-0.3 * layer)


def _rope_perm(dim):
    q = dim // 4
    return np.concatenate([np.arange(0, q), np.arange(2 * q, 3 * q), np.arange(q, 2 * q), np.arange(3 * q, 4 * q)])


_PERM64 = _rope_perm(HEAD_DIM)
_PERM32 = _rope_perm(MLA_ROPE)


def _main_columns():
    diff_qk = np.concatenate([h * 128 + m * 64 + _PERM64 for h in range(DIFF_HEADS) for m in range(2)])
    gq = np.concatenate([hkv * 256 + i * 64 + _PERM64 for i in range(GQA_GROUP) for hkv in range(GQA_KV_HEADS)])
    gk = np.concatenate([hkv * 64 + _PERM64 for hkv in range(GQA_KV_HEADS)])
    return np.concatenate([
        0 + diff_qk, 512 + diff_qk, 1024 + np.arange(512),
        1536 + gq, 2048 + gk, 2176 + np.arange(128),
        2304 + np.arange(MLA_Q_LORA), 2688 + np.arange(MLA_KV_LORA)])


_MAIN_COLS = _main_columns()
_GQA_OUT_ROWS = np.concatenate(
    [hkv * 256 + i * 64 + np.arange(64) for i in range(GQA_GROUP) for hkv in range(GQA_KV_HEADS)])
_IN_MKR = 2944
_IN_GATE = 2976


def _prep_layer_weights(w_in, w_mla_uq, w_mla_ukv, w_br_gqa):
    d = w_in.shape[0]
    mkr = jnp.pad(w_in[:, _IN_MKR + _PERM32], ((0, 0), (MLA_NOPE, LANES - MLA_QK)))
    w1 = jnp.concatenate([w_in[:, _MAIN_COLS], mkr], axis=1).astype(BF16)
    wg = w_in[:, _IN_GATE:].astype(BF16)
    uq = w_mla_uq.reshape(MLA_Q_LORA, MLA_HEADS, MLA_QK)
    uq = uq[:, :, np.concatenate([np.arange(MLA_NOPE), MLA_NOPE + _PERM32])]
    uq = jnp.pad(uq, ((0, 0), (0, 0), (0, LANES - MLA_QK))).reshape(MLA_Q_LORA, MLA_HEADS * LANES).astype(BF16)
    ukv = w_mla_ukv.reshape(MLA_KV_LORA, MLA_HEADS, MLA_NOPE + MLA_V)
    ukn = jnp.pad(ukv[:, :, :MLA_NOPE], ((0, 0), (0, 0), (0, LANES - MLA_NOPE))).reshape(MLA_KV_LORA, -1)
    uv = ukv[:, :, MLA_NOPE:].reshape(MLA_KV_LORA, -1)
    ukv2 = jnp.concatenate([ukn, uv], axis=1).astype(BF16)
    wbg = w_br_gqa[_GQA_OUT_ROWS].astype(BF16)
    del d
    return w1, wg, uq, ukv2, wbg


def _rope_tables(seq, n_ctx):
    rows = seq // GRID_W
    row = jnp.repeat(jnp.arange(rows, dtype=jnp.int32), GRID_W).astype(F32)
    col = jnp.tile(jnp.arange(GRID_W, dtype=jnp.int32), rows).astype(F32)

    def table(dim):
        a = dim // 2
        freqs = ROPE_THETA ** (-jnp.arange(0, a, 2, dtype=F32) / a)
        ang_r, ang_c = row[:, None] * freqs, col[:, None] * freqs
        ang = jnp.concatenate([ang_r, ang_c], axis=-1)
        cos = jnp.concatenate([jnp.cos(ang), jnp.cos(ang)], axis=-1)
        sin = jnp.concatenate([-jnp.sin(ang), jnp.sin(ang)], axis=-1)
        return cos, sin

    c64, s64 = table(HEAD_DIM)
    c64, s64 = jnp.tile(c64, (1, 2)), jnp.tile(s64, (1, 2))
    c32, s32 = table(MLA_ROPE)
    pad = ((0, 0), (MLA_NOPE, LANES - MLA_QK))
    c32 = jnp.pad(c32, pad, constant_values=1.0)
    s32 = jnp.pad(s32, pad)
    ctx_pad = ((n_ctx, 0), (0, 0))
    return (jnp.pad(c64, ctx_pad, constant_values=1.0), jnp.pad(s64, ctx_pad),
            jnp.pad(c32, ctx_pad, constant_values=1.0), jnp.pad(s32, ctx_pad))


def _rms(x):
    return x * lax.rsqrt(jnp.mean(x * x, axis=-1, keepdims=True) + EPS)


def _adaln(x, g, shift, scale):
    return (_rms(x) * g) * (1.0 + scale) + shift


def _rope(z, cos, sin, is_x1, shift):
    partner = jnp.where(is_x1, pltpu.roll(z, LANES - shift, 1), pltpu.roll(z, shift, 1))
    return z * cos + partner * sin


def _half_rms(z, lo, g):
    sq = z * z
    zero = jnp.zeros_like(sq)
    s_lo = jnp.sum(jnp.where(lo, sq, zero), axis=-1, keepdims=True)
    s_hi = jnp.sum(jnp.where(lo, zero, sq), axis=-1, keepdims=True)
    ms = jnp.where(lo, s_lo, s_hi) * (1.0 / HEAD_DIM)
    return z * lax.rsqrt(ms + EPS) * g


def _dot(a, b):
    return jnp.dot(a, b, preferred_element_type=F32)


def _mod_kernel(c_ref, w_ref, b_ref, o_ref):
    cc = c_ref[...]
    a = (cc * jax.nn.sigmoid(cc)).astype(BF16)
    o_ref[...] = _dot(a, w_ref[...].astype(BF16)) + b_ref[...]


def _mod_call(cc, w_mod, b_mod):
    depth, d, n = w_mod.shape
    tn = 1536
    rows = cc.shape[0]
    return pl.pallas_call(
        _mod_kernel,
        grid=(depth, n // tn),
        in_specs=[
            pl.BlockSpec((rows, d), lambda l, j: (0, 0)),
            pl.BlockSpec((None, d, tn), lambda l, j: (l, 0, j)),
            pl.BlockSpec((None, 1, tn), lambda l, j: (l, 0, j)),
        ],
        out_specs=pl.BlockSpec((None, rows, tn), lambda l, j: (l, 0, j)),
        out_shape=jax.ShapeDtypeStruct((depth, rows, n), F32),
        compiler_params=pltpu.CompilerParams(vmem_limit_bytes=VMEM_LIMIT),
        name="mod_vectors",
    )(cc, w_mod, b_mod.reshape(depth, 1, n))


def _proj_kernel(x_ref, mod_ref, g1_ref, w1_ref, wuq_ref, wukv_ref, gq_g_ref, gk_g_ref, gmq_ref, gmkv_ref,
                 c64_ref, s64_ref, c32_ref, s32_ref,
                 dq_o, dk_o, dv_o, gq_o, gk_o, gv_o, mq_o, mk_o, mv_o):
    x = x_ref[...]
    hb = _adaln(x, g1_ref[...], mod_ref[0:1, :], mod_ref[1:2, :]).astype(BF16)
    tm = x.shape[0]
    lane = lax.broadcasted_iota(jnp.int32, (tm, LANES), 1)
    lo = lane < HEAD_DIM
    x1_64 = (lane & (HEAD_DIM // 2)) == 0
    x1_32 = (lane >= MLA_NOPE) & (lane < MLA_NOPE + MLA_ROPE // 2)
    c64, s64 = c64_ref[...], s64_ref[...]
    c32, s32 = c32_ref[...], s32_ref[...]
    qk_scale = HEAD_DIM ** -0.5 * LOG2E

    def seg(off, width):
        return _dot(hb, w1_ref[:, off:off + width])

    for i in range(N_UNITS):
        sl = slice(i * LANES, (i + 1) * LANES)
        dq_o[:, sl] = (_rope(seg(OFF_DQ + i * LANES, LANES), c64, s64, x1_64, 32) * qk_scale).astype(BF16)
        dk_o[:, sl] = _rope(seg(OFF_DK + i * LANES, LANES), c64, s64, x1_64, 32).astype(BF16)
        gq = _half_rms(seg(OFF_GQ + i * LANES, LANES), lo, gq_g_ref[...])
        gq_o[:, sl] = (_rope(gq, c64, s64, x1_64, 32) * qk_scale).astype(BF16)
    dv_o[...] = seg(OFF_DV, BRANCH_W).astype(BF16)
    gk = _half_rms(seg(OFF_GK, LANES), lo, gk_g_ref[...])
    gk_o[...] = _rope(gk, c64, s64, x1_64, 32).astype(BF16)
    gv_o[...] = seg(OFF_GV, LANES).astype(BF16)

    mcq = (_rms(seg(OFF_MCQ, MLA_Q_LORA)) * gmq_ref[...]).astype(BF16)
    mckv = (_rms(seg(OFF_MCKV, MLA_KV_LORA)) * gmkv_ref[...]).astype(BF16)
    mkr = _rope(seg(OFF_MKR, LANES), c32, s32, x1_32, 16)
    mla_scale = MLA_QK ** -0.5 * LOG2E
    for h in range(MLA_HEADS):
        sl = slice(h * LANES, (h + 1) * LANES)
        mq = _dot(mcq, wuq_ref[:, sl])
        mq_o[:, sl] = (_rope(mq, c32, s32, x1_32, 16) * mla_scale).astype(BF16)
        mk_o[:, sl] = (_dot(mckv, wukv_ref[:, sl]) + mkr).astype(BF16)
    mv_o[...] = _dot(mckv, wukv_ref[:, MLA_HEADS * LANES:]).astype(BF16)


def _const_spec(shape):
    return pl.BlockSpec(shape, lambda *_: (0,) * len(shape))


def _proj_call(xall, modarr, g1, w1, wuq, wukv, gq_g, gk_g, gmq, gmkv, tables, n_ctx_tiles):
    b, t, d = xall.shape
    tm = TOKEN_TILE
    tok = lambda w: pl.BlockSpec((None, tm, w), lambda bi, ti: (bi, ti, 0))
    tab = pl.BlockSpec((tm, LANES), lambda bi, ti: (ti, 0))
    widths = (BRANCH_W, BRANCH_W, BRANCH_W, BRANCH_W, LANES, LANES, MLA_HEADS * LANES, MLA_HEADS * LANES, BRANCH_W)
    return pl.pallas_call(
        _proj_kernel,
        grid=(b, t // tm),
        in_specs=[
            tok(d),
            pl.BlockSpec((None, None, N_MOD, d), lambda bi, ti: (bi, (ti >= n_ctx_tiles).astype(jnp.int32), 0, 0)),
            _const_spec((1, d)), _const_spec(w1.shape), _const_spec(wuq.shape), _const_spec(wukv.shape),
            _const_spec((1, LANES)), _const_spec((1, LANES)),
            _const_spec((1, MLA_Q_LORA)), _const_spec((1, MLA_KV_LORA)),
            tab, tab, tab, tab,
        ],
        out_specs=[tok(w) for w in widths],
        out_shape=[jax.ShapeDtypeStruct((b, t, w), BF16) for w in widths],
        compiler_params=pltpu.CompilerParams(
            dimension_semantics=("parallel", "arbitrary"), vmem_limit_bytes=VMEM_LIMIT),
        name="input_projection",
    )(xall, modarr, g1, w1, wuq, wukv, gq_g, gk_g, gmq, gmkv, *tables)


def _attn_kernel(*refs, kind, n_ctx, lam_init, first_tile_is_ctx, upb):
    if kind == "diff":
        q_ref, k_ref, v_ref, lam_ref, gdo_ref, o_ref, vt_ref = refs
    else:
        q_ref, k_ref, v_ref, o_ref, vt_ref = refs
    tq = q_ref.shape[0]
    lane = lax.broadcasted_iota(jnp.int32, (tq, LANES), 1)
    lo = lane < HEAD_DIM
    n_keys = k_ref.shape[0]
    n_kv = vt_ref.shape[0]
    qw = q_ref.shape[1] // upb
    kw = k_ref.shape[1] // n_kv

    @pl.when(pl.program_id(2) == 0)
    def _():
        for i in range(n_kv):
            vt_ref[i, 0:LANES, :] = v_ref[:, i * LANES:(i + 1) * LANES].astype(F32).T.astype(BF16)
            vt_ref[i, LANES:, :] = jnp.ones((SUM_ROWS, n_keys), BF16)

    def scores(u, nk):
        q = q_ref[:, u * qw:(u + 1) * qw]
        kv = u % n_kv
        k = k_ref[0:nk, kv * kw:(kv + 1) * kw]
        if kind == "mla":
            qa, qb = q[:, :LANES], q[:, LANES:]
            ka, kb = k[:, :LANES], k[:, LANES:]
        else:
            zero = jnp.zeros_like(q)
            qa, qb = jnp.where(lo, q, zero), jnp.where(lo, zero, q)
            ka = kb = k
        nt = (((1,), (1,)), ((), ()))
        return (lax.dot_general(ka, qa, nt, preferred_element_type=F32),
                lax.dot_general(kb, qb, nt, preferred_element_type=F32))

    def softmax_pv(u, nk, st):
        p = jnp.exp2((st - jnp.max(st, axis=0, keepdims=True)).astype(BF16))
        ot = _dot(vt_ref[u % n_kv, :, 0:nk], p)
        return ot[0:LANES] / ot[LANES:LANES + 1]

    def finish(u, nk, sa, sb):
        oa, ob = softmax_pv(u, nk, sa), softmax_pv(u, nk, sb)
        if kind == "diff":
            o = (oa - lam_ref[0:1, 0:1] * ob).T
            o = _rms(o) * gdo_ref[...] * (1.0 - lam_init)
        else:
            row = lax.broadcasted_iota(jnp.int32, oa.shape, 0)
            o = jnp.where(row < HEAD_DIM, oa, ob).T
        o_ref[:, u * LANES:(u + 1) * LANES] = o.astype(BF16)

    def compute(nk):
        pending = scores(0, nk)
        for u in range(upb):
            nxt = scores(u + 1, nk) if u + 1 < upb else None
            finish(u, nk, *pending)
            pending = nxt

    if first_tile_is_ctx:
        j = pl.program_id(2)
        pl.when(j == 0)(lambda: compute(n_ctx))
        pl.when(j > 0)(lambda: compute(n_keys))
    else:
        compute(n_keys)


def _attn_call(kind, q, k, v, extra, *, n_ctx, lam_init, include_ctx):
    b, t, _ = k.shape
    tq = TOKEN_TILE
    upb = UNITS_PER_STEP
    n_ctx_tiles = n_ctx // tq
    q_off = 0 if include_ctx else n_ctx_tiles
    nq = t // tq - q_off
    qw = q.shape[-1] // N_UNITS
    shared_kv = kind == "gqa"
    n_kv = 1 if shared_kv else upb
    kw = LANES if shared_kv else k.shape[-1] // N_UNITS
    kv_map = (lambda bi, g, j: (bi, 0, 0)) if shared_kv else (lambda bi, g, j: (bi, 0, g))
    in_specs = [
        pl.BlockSpec((None, tq, upb * qw), lambda bi, g, j: (bi, j + q_off, g)),
        pl.BlockSpec((None, t, n_kv * kw), kv_map),
        pl.BlockSpec((None, t, n_kv * LANES), kv_map),
    ]
    if kind == "diff":
        in_specs += [_const_spec((1, LANES)), _const_spec((1, LANES))]
    return pl.pallas_call(
        functools.partial(_attn_kernel, kind=kind, n_ctx=n_ctx, lam_init=lam_init,
                          first_tile_is_ctx=include_ctx, upb=upb),
        grid=(b, N_UNITS // upb, nq),
        in_specs=in_specs,
        out_specs=pl.BlockSpec((None, tq, upb * LANES), lambda bi, g, j: (bi, j, g)),
        out_shape=jax.ShapeDtypeStruct((b, nq * tq, BRANCH_W), BF16),
        scratch_shapes=[pltpu.VMEM((n_kv, LANES + SUM_ROWS, t), BF16)],
        compiler_params=pltpu.CompilerParams(
            dimension_semantics=("parallel", "arbitrary", "arbitrary"), vmem_limit_bytes=VMEM_LIMIT),
        name="attention_" + kind,
    )(q, k, v, *extra)


def _merge_kernel(x_ref, mod_ref, g1_ref, wg_ref, bg_ref, od_ref, og_ref, om_ref,
                  wbd_ref, wbg_ref, wbm_ref, wout_ref, xo_ref):
    x = x_ref[...]
    d = x.shape[-1]
    hb = _adaln(x, g1_ref[...], mod_ref[0:1, :], mod_ref[1:2, :]).astype(BF16)
    y = None
    for i, (o_ref, wb_ref) in enumerate(((od_ref, wbd_ref), (og_ref, wbg_ref), (om_ref, wbm_ref))):
        sl = slice(i * d, (i + 1) * d)
        gate = jax.nn.sigmoid(_dot(hb, wg_ref[:, sl]) + bg_ref[:, sl])
        term = gate * _dot(o_ref[...], wb_ref[...])
        y = term if y is None else y + term
    out = _dot(y.astype(BF16), wout_ref[...])
    xo_ref[...] = x + mod_ref[2:3, :] * out


def _mod_spec(d, pos_off, n_ctx_tiles):
    return pl.BlockSpec((None, None, N_MOD, d),
                        lambda bi, ti: (bi, (ti + pos_off >= n_ctx_tiles).astype(jnp.int32), 0, 0))


def _merge_call(xall, modarr, g1, wg, bg, od, og, om, wbd, wbg, wbm, wout, *, x_off, n_ctx_tiles):
    b, _, d = xall.shape
    tm = TOKEN_TILE
    nt = od.shape[1] // tm
    tok = lambda w: pl.BlockSpec((None, tm, w), lambda bi, ti: (bi, ti, 0))
    return pl.pallas_call(
        _merge_kernel,
        grid=(b, nt),
        in_specs=[
            pl.BlockSpec((None, tm, d), lambda bi, ti: (bi, ti + x_off, 0)),
            _mod_spec(d, x_off, n_ctx_tiles),
            _const_spec((1, d)), _const_spec(wg.shape), _const_spec(bg.shape),
            tok(BRANCH_W), tok(BRANCH_W), tok(BRANCH_W),
            _const_spec(wbd.shape), _const_spec(wbg.shape), _const_spec(wbm.shape), _const_spec(wout.shape),
        ],
        out_specs=tok(d),
        out_shape=jax.ShapeDtypeStruct((b, nt * tm, d), F32),
        compiler_params=pltpu.CompilerParams(
            dimension_semantics=("parallel", "arbitrary"), vmem_limit_bytes=VMEM_LIMIT),
        name="branch_merge",
    )(xall, modarr, g1, wg, bg, od, og, om, wbd, wbg, wbm, wout)


def _ffn_kernel(*refs, final):
    if final:
        x_ref, mod_ref, g2_ref, win_ref, wout_ref, gf_ref, o_ref = refs
    else:
        x_ref, mod_ref, g2_ref, win_ref, wout_ref, o_ref = refs
    x = x_ref[...]
    hidden = wout_ref.shape[0]
    hb = _adaln(x, g2_ref[...], mod_ref[3:4, :], mod_ref[4:5, :]).astype(BF16)
    g = _dot(hb, win_ref[:, :hidden])
    u = _dot(hb, win_ref[:, hidden:])
    a = (g * jax.nn.sigmoid(g) * u).astype(BF16)
    xn = x + mod_ref[5:6, :] * _dot(a, wout_ref[...])
    if final:
        xn = _rms(xn) * gf_ref[...]
    o_ref[...] = xn


def _ffn_call(xin, modarr, g2, win, wout, gf, *, pos_off, n_ctx_tiles):
    b, t, d = xin.shape
    tm = TOKEN_TILE
    tok = pl.BlockSpec((None, tm, d), lambda bi, ti: (bi, ti, 0))
    final = gf is not None
    in_specs = [tok, _mod_spec(d, pos_off, n_ctx_tiles), _const_spec((1, d)),
                _const_spec(win.shape), _const_spec(wout.shape)]
    args = [xin, modarr, g2, win, wout]
    if final:
        in_specs.append(_const_spec((1, d)))
        args.append(gf)
    return pl.pallas_call(
        functools.partial(_ffn_kernel, final=final),
        grid=(b, t // tm),
        in_specs=in_specs,
        out_specs=tok,
        out_shape=jax.ShapeDtypeStruct((b, t, d), F32),
        compiler_params=pltpu.CompilerParams(
            dimension_semantics=("parallel", "arbitrary"), vmem_limit_bytes=VMEM_LIMIT),
        name="swiglu",
    )(*args)


def kernel(x, c, ctx, c_ctx, w_mod, b_mod, g_norm1, w_in, b_gate, lam_q1, lam_k1, lam_q2, lam_k2, g_diff_out, g_gqa_q, g_gqa_k, g_mla_q, w_mla_uq, g_mla_kv, w_mla_ukv, w_br_diff, w_br_gqa, w_br_mla, w_out, g_norm2, w_ffn_in, w_ffn_out, g_final):
    b, seq, d = x.shape
    n_ctx = ctx.shape[1]
    depth = w_in.shape[0]
    assert n_ctx % TOKEN_TILE == 0 and seq % TOKEN_TILE == 0 and seq % GRID_W == 0
    n_ctx_tiles = n_ctx // TOKEN_TILE

    mod_rows = 32
    cc = jnp.zeros((mod_rows, d), F32).at[:b].set(c).at[b].set(c_ctx)
    mods = _mod_call(cc, w_mod, b_mod)
    tables = _rope_tables(seq, n_ctx)
    xall = jnp.concatenate([ctx, x], axis=1)

    for l in range(depth):
        last = l == depth - 1
        lam_init = _lambda_init(l)
        lam = (jnp.exp(jnp.sum(lam_q1[l] * lam_k1[l])) - jnp.exp(jnp.sum(lam_q2[l] * lam_k2[l])) + lam_init)
        lam_vec = jnp.full((1, LANES), lam, F32)
        mod_lat = mods[l, :b].reshape(b, N_MOD, d)
        mod_ctx = jnp.broadcast_to(mods[l, b].reshape(1, N_MOD, d), (b, N_MOD, d))
        modarr = jnp.stack([mod_ctx, mod_lat], axis=1)

        w1, wg, wuq, wukv, wbg = _prep_layer_weights(w_in[l], w_mla_uq[l], w_mla_ukv[l], w_br_gqa[l])
        g1 = g_norm1[l].reshape(1, d)
        gq_g = jnp.tile(g_gqa_q[l][_PERM64], 2).reshape(1, LANES)
        gk_g = jnp.tile(g_gqa_k[l][_PERM64], 2).reshape(1, LANES)

        dq, dk, dv, gq, gk, gv, mq, mk, mv = _proj_call(
            xall, modarr, g1, w1, wuq, wukv, gq_g, gk_g,
            g_mla_q[l].reshape(1, -1), g_mla_kv[l].reshape(1, -1), tables, n_ctx_tiles)

        attn = functools.partial(_attn_call, n_ctx=n_ctx, lam_init=lam_init, include_ctx=not last)
        od = attn("diff", dq, dk, dv, (lam_vec, g_diff_out[l].reshape(1, LANES)))
        og = attn("gqa", gq, gk, gv, ())
        om = attn("mla", mq, mk, mv, ())

        x_off = n_ctx_tiles if last else 0
        xmid = _merge_call(
            xall, modarr, g1, wg, b_gate[l].reshape(1, -1), od, og, om,
            w_br_diff[l].astype(BF16), wbg, w_br_mla[l].astype(BF16), w_out[l].astype(BF16),
            x_off=x_off, n_ctx_tiles=n_ctx_tiles)
        xall = _ffn_call(
            xmid, modarr, g_norm2[l].reshape(1, d), w_ffn_in[l].astype(BF16), w_ffn_out[l].astype(BF16),
            g_final.reshape(1, d) if last else None, pos_off=x_off, n_ctx_tiles=n_ctx_tiles)
    return xall
```

```python
import functools
import math

import numpy as np
import jax
import jax.numpy as jnp
from jax import lax
from jax.experimental import pallas as pl
from jax.experimental.pallas import tpu as pltpu

F32 = jnp.float32
BF16 = jnp.bfloat16

GRID_W = 64
ROPE_THETA = 10000.0
EPS = 1e-6
N_MOD = 6

DIFF_HEADS = 4
HEAD_DIM = 64
GQA_KV_HEADS = 2
GQA_GROUP = 4
MLA_HEADS = 8
MLA_Q_LORA = 384
MLA_KV_LORA = 256
MLA_NOPE = 64
MLA_ROPE = 32
MLA_QK = MLA_NOPE + MLA_ROPE
MLA_V = 64
BRANCH_W = 512
N_UNITS = 4

LANES = 128
UNITS_PER_STEP = 4
SUM_ROWS = 16
LOG2E = math.log2(math.e)
TOKEN_TILE = 256
VMEM_LIMIT = 56 * 1024 * 1024

OFF_DQ, OFF_DK, OFF_DV, OFF_GQ, OFF_GK, OFF_GV = 0, 512, 1024, 1536, 2048, 2176
OFF_MCQ, OFF_MCKV, OFF_MKR, W1_WIDTH = 2304, 2688, 2944, 3072


def _lambda_init(layer):
    return 0.8 - 0.6 * math.exp(-0.3 * layer)


def _rope_perm(dim):
    q = dim // 4
    return np.concatenate([np.arange(0, q), np.arange(2 * q, 3 * q), np.arange(q, 2 * q), np.arange(3 * q, 4 * q)])


_PERM64 = _rope_perm(HEAD_DIM)
_PERM32 = _rope_perm(MLA_ROPE)


def _main_columns():
    diff_qk = np.concatenate([h * 128 + m * 64 + _PERM64 for h in range(DIFF_HEADS) for m in range(2)])
    gq = np.concatenate([hkv * 256 + i * 64 + _PERM64 for i in range(GQA_GROUP) for hkv in range(GQA_KV_HEADS)])
    gk = np.concatenate([hkv * 64 + _PERM64 for hkv in range(GQA_KV_HEADS)])
    return np.concatenate([
        0 + diff_qk, 512 + diff_qk, 1024 + np.arange(512),
        1536 + gq, 2048 + gk, 2176 + np.arange(128),
        2304 + np.arange(MLA_Q_LORA), 2688 + np.arange(MLA_KV_LORA)])


_MAIN_COLS = _main_columns()
_GQA_OUT_ROWS = np.concatenate(
    [hkv * 256 + i * 64 + np.arange(64) for i in range(GQA_GROUP) for hkv in range(GQA_KV_HEADS)])
_IN_MKR = 2944
_IN_GATE = 2976


def _prep_weights(w_in, w_mla_uq, w_mla_ukv, w_br_gqa):
    depth = w_in.shape[0]
    mkr = jnp.pad(w_in[:, :, _IN_MKR + _PERM32], ((0, 0), (0, 0), (MLA_NOPE, LANES - MLA_QK)))
    w1 = jnp.concatenate([w_in[:, :, _MAIN_COLS], mkr], axis=2).astype(BF16)
    wg = w_in[:, :, _IN_GATE:].astype(BF16)
    uq = w_mla_uq.reshape(depth, MLA_Q_LORA, MLA_HEADS, MLA_QK)
    uq = uq[..., np.concatenate([np.arange(MLA_NOPE), MLA_NOPE + _PERM32])]
    uq = jnp.pad(uq, ((0, 0), (0, 0), (0, 0), (0, LANES - MLA_QK)))
    uq = uq.reshape(depth, MLA_Q_LORA, MLA_HEADS * LANES).astype(BF16)
    ukv = w_mla_ukv.reshape(depth, MLA_KV_LORA, MLA_HEADS, MLA_NOPE + MLA_V)
    ukn = jnp.pad(ukv[..., :MLA_NOPE], ((0, 0), (0, 0), (0, 0), (0, LANES - MLA_NOPE)))
    ukn = ukn.reshape(depth, MLA_KV_LORA, -1)
    uv = ukv[..., MLA_NOPE:].reshape(depth, MLA_KV_LORA, -1)
    ukv2 = jnp.concatenate([ukn, uv], axis=2).astype(BF16)
    wbg = w_br_gqa[:, _GQA_OUT_ROWS].astype(BF16)
    return w1, wg, uq, ukv2, wbg


def _rope_tables(seq, n_ctx):
    rows = seq // GRID_W
    row = jnp.repeat(jnp.arange(rows, dtype=jnp.int32), GRID_W).astype(F32)
    col = jnp.tile(jnp.arange(GRID_W, dtype=jnp.int32), rows).astype(F32)

    def table(dim):
        a = dim // 2
        freqs = ROPE_THETA ** (-jnp.arange(0, a, 2, dtype=F32) / a)
        ang_r, ang_c = row[:, None] * freqs, col[:, None] * freqs
        ang = jnp.concatenate([ang_r, ang_c], axis=-1)
        cos = jnp.concatenate([jnp.cos(ang), jnp.cos(ang)], axis=-1)
        sin = jnp.concatenate([-jnp.sin(ang), jnp.sin(ang)], axis=-1)
        return cos, sin

    c64, s64 = table(HEAD_DIM)
    c64, s64 = jnp.tile(c64, (1, 2)), jnp.tile(s64, (1, 2))
    c32, s32 = table(MLA_ROPE)
    pad = ((0, 0), (MLA_NOPE, LANES - MLA_QK))
    c32 = jnp.pad(c32, pad, constant_values=1.0)
    s32 = jnp.pad(s32, pad)
    ctx_pad = ((n_ctx, 0), (0, 0))
    return (jnp.pad(c64, ctx_pad, constant_values=1.0), jnp.pad(s64, ctx_pad),
            jnp.pad(c32, ctx_pad, constant_values=1.0), jnp.pad(s32, ctx_pad))


def _rms(x):
    return x * lax.rsqrt(jnp.mean(x * x, axis=-1, keepdims=True) + EPS)


def _adaln(x, g, shift, scale):
    return (_rms(x) * g) * (1.0 + scale) + shift


def _rope(z, cos, sin, is_x1, shift):
    partner = jnp.where(is_x1, pltpu.roll(z, LANES - shift, 1), pltpu.roll(z, shift, 1))
    return z * cos + partner * sin


def _half_rms(z, lo, g):
    sq = z * z
    zero = jnp.zeros_like(sq)
    s_lo = jnp.sum(jnp.where(lo, sq, zero), axis=-1, keepdims=True)
    s_hi = jnp.sum(jnp.where(lo, zero, sq), axis=-1, keepdims=True)
    ms = jnp.where(lo, s_lo, s_hi) * (1.0 / HEAD_DIM)
    return z * lax.rsqrt(ms + EPS) * g


def _dot(a, b):
    return jnp.dot(a, b, preferred_element_type=F32)


def _const_spec(shape):
    return pl.BlockSpec(shape, lambda *_: (0,) * len(shape))


def _layer_spec(stacked, layer):
    shape = stacked.shape[1:]
    return pl.BlockSpec((None,) + shape, lambda *_: (layer,) + (0,) * len(shape))


def _mod_spec(modarr, layer, pos_off, n_ctx_tiles):
    return pl.BlockSpec((None, None, None) + modarr.shape[3:],
                        lambda bi, ti: (layer, bi, (ti + pos_off >= n_ctx_tiles).astype(jnp.int32), 0, 0))


def _token_specs(tokens, tile_off, n_ctx_tiles):
    tm, d = TOKEN_TILE, tokens[0].shape[-1]
    if len(tokens) == 1:
        return [pl.BlockSpec((None, tm, d), lambda bi, ti: (bi, ti + tile_off, 0))]
    assert tile_off == 0
    return [pl.BlockSpec((None, tm, d), lambda bi, ti: (bi, jnp.minimum(ti, n_ctx_tiles - 1), 0)),
            pl.BlockSpec((None, tm, d), lambda bi, ti: (bi, jnp.maximum(ti - n_ctx_tiles, 0), 0))]


def _read_tokens(tok_refs, n_ctx_tiles):
    if len(tok_refs) == 1:
        return tok_refs[0][...]
    ctx_ref, x_ref = tok_refs
    return jnp.where(pl.program_id(1) < n_ctx_tiles, ctx_ref[...], x_ref[...])


def _mod_kernel(c_ref, w_ref, b_ref, o_ref):
    cc = c_ref[...]
    a = (cc * jax.nn.sigmoid(cc)).astype(BF16)
    o_ref[...] = _dot(a, w_ref[...].astype(BF16)) + b_ref[...]


def _mod_call(cc, w_mod, b_mod):
    depth, d, n = w_mod.shape
    tn = 1536
    rows = cc.shape[0]
    return pl.pallas_call(
        _mod_kernel,
        grid=(depth, n // tn),
        in_specs=[
            pl.BlockSpec((rows, d), lambda l, j: (0, 0)),
            pl.BlockSpec((None, d, tn), lambda l, j: (l, 0, j)),
            pl.BlockSpec((None, 1, tn), lambda l, j: (l, 0, j)),
        ],
        out_specs=pl.BlockSpec((None, rows, tn), lambda l, j: (l, 0, j)),
        out_shape=jax.ShapeDtypeStruct((depth, rows, n), F32),
        compiler_params=pltpu.CompilerParams(vmem_limit_bytes=VMEM_LIMIT),
        name="mod_vectors",
    )(cc, w_mod, b_mod.reshape(depth, 1, n))


def _proj_kernel(*refs, n_tok, n_ctx_tiles):
    (mod_ref, g1_ref, w1_ref, wuq_ref, wukv_ref, gq_g_ref, gk_g_ref, gmq_ref, gmkv_ref,
     c64_ref, s64_ref, c32_ref, s32_ref,
     dq_o, dk_o, dv_o, gq_o, gk_o, gv_o, mq_o, mk_o, mv_o) = refs[n_tok:]
    x = _read_tokens(refs[:n_tok], n_ctx_tiles)
    hb = _adaln(x, g1_ref[...], mod_ref[0:1, :], mod_ref[1:2, :]).astype(BF16)
    tm = x.shape[0]
    lane = lax.broadcasted_iota(jnp.int32, (tm, LANES), 1)
    lo = lane < HEAD_DIM
    x1_64 = (lane & (HEAD_DIM // 2)) == 0
    x1_32 = (lane >= MLA_NOPE) & (lane < MLA_NOPE + MLA_ROPE // 2)
    c64, s64 = c64_ref[...], s64_ref[...]
    c32, s32 = c32_ref[...], s32_ref[...]
    qk_scale = HEAD_DIM ** -0.5 * LOG2E

    def seg(off, width):
        return _dot(hb, w1_ref[:, off:off + width])

    def blocks(z):
        return [(slice(i * LANES, (i + 1) * LANES), z[:, i * LANES:(i + 1) * LANES]) for i in range(z.shape[1] // LANES)]

    for sl, z in blocks(seg(OFF_DQ, BRANCH_W)):
        dq_o[:, sl] = (_rope(z, c64, s64, x1_64, 32) * qk_scale).astype(BF16)
    for sl, z in blocks(seg(OFF_DK, BRANCH_W)):
        dk_o[:, sl] = _rope(z, c64, s64, x1_64, 32).astype(BF16)
    dv_o[...] = seg(OFF_DV, BRANCH_W).astype(BF16)
    for sl, z in blocks(seg(OFF_GQ, BRANCH_W)):
        gq = _half_rms(z, lo, gq_g_ref[...])
        gq_o[:, sl] = (_rope(gq, c64, s64, x1_64, 32) * qk_scale).astype(BF16)
    gkv = seg(OFF_GK, 2 * LANES)
    gk = _half_rms(gkv[:, :LANES], lo, gk_g_ref[...])
    gk_o[...] = _rope(gk, c64, s64, x1_64, 32).astype(BF16)
    gv_o[...] = gkv[:, LANES:].astype(BF16)

    zm = seg(OFF_MCQ, W1_WIDTH - OFF_MCQ)
    mcq = (_rms(zm[:, :MLA_Q_LORA]) * gmq_ref[...]).astype(BF16)
    mckv = (_rms(zm[:, MLA_Q_LORA:MLA_Q_LORA + MLA_KV_LORA]) * gmkv_ref[...]).astype(BF16)
    mkr = _rope(zm[:, MLA_Q_LORA + MLA_KV_LORA:], c32, s32, x1_32, 16)
    mla_scale = MLA_QK ** -0.5 * LOG2E
    for sl, z in blocks(_dot(mcq, wuq_ref[...])):
        mq_o[:, sl] = (_rope(z, c32, s32, x1_32, 16) * mla_scale).astype(BF16)
    mkv = _dot(mckv, wukv_ref[...])
    for sl, z in blocks(mkv[:, :MLA_HEADS * LANES]):
        mk_o[:, sl] = (z + mkr).astype(BF16)
    mv_o[...] = mkv[:, MLA_HEADS * LANES:].astype(BF16)


def _proj_call(layer, tokens, modarr, g1, w1, wuq, wukv, gq_g, gk_g, gmq, gmkv, tables, n_ctx_tiles):
    b = tokens[0].shape[0]
    t = sum(a.shape[1] for a in tokens)
    tm = TOKEN_TILE
    tok = lambda w: pl.BlockSpec((None, tm, w), lambda bi, ti: (bi, ti, 0))
    tab = pl.BlockSpec((tm, LANES), lambda bi, ti: (ti, 0))
    widths = (BRANCH_W, BRANCH_W, BRANCH_W, BRANCH_W, LANES, LANES, MLA_HEADS * LANES, MLA_HEADS * LANES, BRANCH_W)
    return pl.pallas_call(
        functools.partial(_proj_kernel, n_tok=len(tokens), n_ctx_tiles=n_ctx_tiles),
        grid=(b, t // tm),
        in_specs=_token_specs(tokens, 0, n_ctx_tiles) + [_mod_spec(modarr, layer, 0, n_ctx_tiles)] + [
            _layer_spec(a, layer) for a in (g1, w1, wuq, wukv, gq_g, gk_g, gmq, gmkv)] + [tab, tab, tab, tab],
        out_specs=[tok(w) for w in widths],
        out_shape=[jax.ShapeDtypeStruct((b, t, w), BF16) for w in widths],
        compiler_params=pltpu.CompilerParams(
            dimension_semantics=("parallel", "arbitrary"), vmem_limit_bytes=VMEM_LIMIT),
        name="input_projection",
    )(*tokens, modarr, g1, w1, wuq, wukv, gq_g, gk_g, gmq, gmkv, *tables)


def _attn_kernel(*refs, kind, n_ctx, lam_init, first_tile_is_ctx, upb):
    if kind == "diff":
        q_ref, k_ref, v_ref, lam_ref, gdo_ref, o_ref, vt_ref = refs
    else:
        q_ref, k_ref, v_ref, o_ref, vt_ref = refs
    tq = q_ref.shape[0]
    lane = lax.broadcasted_iota(jnp.int32, (tq, LANES), 1)
    lo = lane < HEAD_DIM
    n_keys = k_ref.shape[0]
    n_kv = vt_ref.shape[0]
    qw = q_ref.shape[1] // upb
    kw = k_ref.shape[1] // n_kv

    @pl.when(pl.program_id(2) == 0)
    def _():
        for i in range(n_kv):
            vt_ref[i, 0:LANES, :] = v_ref[:, i * LANES:(i + 1) * LANES].astype(F32).T.astype(BF16)
            vt_ref[i, LANES:, :] = jnp.ones((SUM_ROWS, n_keys), BF16)

    def scores(u, nk):
        q = q_ref[:, u * qw:(u + 1) * qw]
        kv = u % n_kv
        k = k_ref[0:nk, kv * kw:(kv + 1) * kw]
        if kind == "mla":
            qa, qb = q[:, :LANES], q[:, LANES:]
            ka, kb = k[:, :LANES], k[:, LANES:]
        else:
            zero = jnp.zeros_like(q)
            qa, qb = jnp.where(lo, q, zero), jnp.where(lo, zero, q)
            ka = kb = k
        nt = (((1,), (1,)), ((), ()))
        return (lax.dot_general(ka, qa, nt, preferred_element_type=F32),
                lax.dot_general(kb, qb, nt, preferred_element_type=F32))

    def softmax_pv(u, nk, st):
        p = jnp.exp2((st - jnp.max(st, axis=0, keepdims=True)).astype(BF16))
        ot = _dot(vt_ref[u % n_kv, :, 0:nk], p)
        return ot[0:LANES] / ot[LANES:LANES + 1]

    def finish(u, nk, sa, sb):
        oa, ob = softmax_pv(u, nk, sa), softmax_pv(u, nk, sb)
        if kind == "diff":
            o = (oa - lam_ref[0:1, 0:1] * ob).T
            o = _rms(o) * gdo_ref[...] * (1.0 - lam_init)
        else:
            row = lax.broadcasted_iota(jnp.int32, oa.shape, 0)
            o = jnp.where(row < HEAD_DIM, oa, ob).T
        o_ref[:, u * LANES:(u + 1) * LANES] = o.astype(BF16)

    def compute(nk):
        pending = scores(0, nk)
        for u in range(upb):
            nxt = scores(u + 1, nk) if u + 1 < upb else None
            finish(u, nk, *pending)
            pending = nxt

    if first_tile_is_ctx:
        j = pl.program_id(2)
        pl.when(j == 0)(lambda: compute(n_ctx))
        pl.when(j > 0)(lambda: compute(n_keys))
    else:
        compute(n_keys)


def _attn_call(kind, layer, q, k, v, extra, *, n_ctx, lam_init, include_ctx):
    b, t, _ = k.shape
    tq = TOKEN_TILE
    upb = UNITS_PER_STEP
    n_ctx_tiles = n_ctx // tq
    q_off = 0 if include_ctx else n_ctx_tiles
    nq = t // tq - q_off
    qw = q.shape[-1] // N_UNITS
    shared_kv = kind == "gqa"
    n_kv = 1 if shared_kv else upb
    kw = LANES if shared_kv else k.shape[-1] // N_UNITS
    kv_map = (lambda bi, g, j: (bi, 0, 0)) if shared_kv else (lambda bi, g, j: (bi, 0, g))
    in_specs = [
        pl.BlockSpec((None, tq, upb * qw), lambda bi, g, j: (bi, j + q_off, g)),
        pl.BlockSpec((None, t, n_kv * kw), kv_map),
        pl.BlockSpec((None, t, n_kv * LANES), kv_map),
    ]
    in_specs += [_layer_spec(a, layer) for a in extra]
    return pl.pallas_call(
        functools.partial(_attn_kernel, kind=kind, n_ctx=n_ctx, lam_init=lam_init,
                          first_tile_is_ctx=include_ctx, upb=upb),
        grid=(b, N_UNITS // upb, nq),
        in_specs=in_specs,
        out_specs=pl.BlockSpec((None, tq, upb * LANES), lambda bi, g, j: (bi, j, g)),
        out_shape=jax.ShapeDtypeStruct((b, nq * tq, BRANCH_W), BF16),
        scratch_shapes=[pltpu.VMEM((n_kv, LANES + SUM_ROWS, t), BF16)],
        compiler_params=pltpu.CompilerParams(
            dimension_semantics=("parallel", "arbitrary", "arbitrary"), vmem_limit_bytes=VMEM_LIMIT),
        name="attention_" + kind,
    )(q, k, v, *extra)


def _merge_kernel(*refs, n_tok, n_ctx_tiles):
    (mod_ref, g1_ref, wg_ref, bg_ref, od_ref, og_ref, om_ref,
     wbd_ref, wbg_ref, wbm_ref, wout_ref, xo_ref) = refs[n_tok:]
    x = _read_tokens(refs[:n_tok], n_ctx_tiles)
    d = x.shape[-1]
    hb = _adaln(x, g1_ref[...], mod_ref[0:1, :], mod_ref[1:2, :]).astype(BF16)
    y = None
    for i, (o_ref, wb_ref) in enumerate(((od_ref, wbd_ref), (og_ref, wbg_ref), (om_ref, wbm_ref))):
        sl = slice(i * d, (i + 1) * d)
        gate = jax.nn.sigmoid(_dot(hb, wg_ref[:, sl]) + bg_ref[:, sl])
        term = gate * _dot(o_ref[...], wb_ref[...])
        y = term if y is None else y + term
    out = _dot(y.astype(BF16), wout_ref[...])
    xo_ref[...] = x + mod_ref[2:3, :] * out


def _merge_call(layer, tokens, modarr, g1, wg, bg, od, og, om, wbd, wbg, wbm, wout, *, x_off, n_ctx_tiles):
    b, d = tokens[0].shape[0], tokens[0].shape[-1]
    tm = TOKEN_TILE
    nt = od.shape[1] // tm
    tok = lambda w: pl.BlockSpec((None, tm, w), lambda bi, ti: (bi, ti, 0))
    return pl.pallas_call(
        functools.partial(_merge_kernel, n_tok=len(tokens), n_ctx_tiles=n_ctx_tiles),
        grid=(b, nt),
        in_specs=_token_specs(tokens, x_off, n_ctx_tiles) + [_mod_spec(modarr, layer, x_off, n_ctx_tiles)] + [
            _layer_spec(a, layer) for a in (g1, wg, bg)] + [tok(BRANCH_W)] * 3 + [
            _layer_spec(a, layer) for a in (wbd, wbg, wbm, wout)],
        out_specs=tok(d),
        out_shape=jax.ShapeDtypeStruct((b, nt * tm, d), F32),
        compiler_params=pltpu.CompilerParams(
            dimension_semantics=("parallel", "arbitrary"), vmem_limit_bytes=VMEM_LIMIT),
        name="branch_merge",
    )(*tokens, modarr, g1, wg, bg, od, og, om, wbd, wbg, wbm, wout)


def _ffn_kernel(*refs, final):
    if final:
        x_ref, mod_ref, g2_ref, win_ref, wout_ref, gf_ref, o_ref = refs
    else:
        x_ref, mod_ref, g2_ref, win_ref, wout_ref, o_ref = refs
    x = x_ref[...]
    hidden = wout_ref.shape[0]
    hb = _adaln(x, g2_ref[...], mod_ref[3:4, :], mod_ref[4:5, :]).astype(BF16)
    g = _dot(hb, win_ref[:, :hidden])
    u = _dot(hb, win_ref[:, hidden:])
    a = (g * jax.nn.sigmoid(g) * u).astype(BF16)
    xn = x + mod_ref[5:6, :] * _dot(a, wout_ref[...])
    if final:
        xn = _rms(xn) * gf_ref[...]
    o_ref[...] = xn


def _ffn_call(layer, xin, modarr, g2, win, wout, gf, *, pos_off, n_ctx_tiles):
    b, t, d = xin.shape
    tm = TOKEN_TILE
    tok = pl.BlockSpec((None, tm, d), lambda bi, ti: (bi, ti, 0))
    final = gf is not None
    in_specs = [tok, _mod_spec(modarr, layer, pos_off, n_ctx_tiles)] + [
        _layer_spec(a, layer) for a in (g2, win, wout)]
    args = [xin, modarr, g2, win, wout]
    if final:
        in_specs.append(_const_spec((1, d)))
        args.append(gf)
    return pl.pallas_call(
        functools.partial(_ffn_kernel, final=final),
        grid=(b, t // tm),
        in_specs=in_specs,
        out_specs=tok,
        out_shape=jax.ShapeDtypeStruct((b, t, d), F32),
        compiler_params=pltpu.CompilerParams(
            dimension_semantics=("parallel", "arbitrary"), vmem_limit_bytes=VMEM_LIMIT),
        name="swiglu",
    )(*args)


def kernel(x, c, ctx, c_ctx, w_mod, b_mod, g_norm1, w_in, b_gate, lam_q1, lam_k1, lam_q2, lam_k2, g_diff_out, g_gqa_q, g_gqa_k, g_mla_q, w_mla_uq, g_mla_kv, w_mla_ukv, w_br_diff, w_br_gqa, w_br_mla, w_out, g_norm2, w_ffn_in, w_ffn_out, g_final):
    b, seq, d = x.shape
    n_ctx = ctx.shape[1]
    depth = w_in.shape[0]
    assert n_ctx % TOKEN_TILE == 0 and seq % TOKEN_TILE == 0 and seq % GRID_W == 0
    n_ctx_tiles = n_ctx // TOKEN_TILE

    mod_rows = 32
    cc = jnp.zeros((mod_rows, d), F32).at[:b].set(c).at[b].set(c_ctx)
    mods = _mod_call(cc, w_mod, b_mod)
    mod_lat = mods[:, :b].reshape(depth, b, N_MOD, d)
    mod_ctx = jnp.broadcast_to(mods[:, b].reshape(depth, 1, N_MOD, d), (depth, b, N_MOD, d))
    modarr = jnp.stack([mod_ctx, mod_lat], axis=2)
    tables = _rope_tables(seq, n_ctx)

    lam_inits = [_lambda_init(l) for l in range(depth)]
    lam = (jnp.exp(jnp.sum(lam_q1 * lam_k1, axis=-1)) - jnp.exp(jnp.sum(lam_q2 * lam_k2, axis=-1))
           + jnp.asarray(lam_inits, F32))
    lam_vec = jnp.broadcast_to(lam.reshape(depth, 1, 1), (depth, 1, LANES))

    def row(a):
        return a.reshape(depth, 1, -1)

    w1, wg, wuq, wukv, wbg = _prep_weights(w_in, w_mla_uq, w_mla_ukv, w_br_gqa)
    wbd, wbm, wout = w_br_diff.astype(BF16), w_br_mla.astype(BF16), w_out.astype(BF16)
    wfi, wfo = w_ffn_in.astype(BF16), w_ffn_out.astype(BF16)
    g1, g2, bg = row(g_norm1), row(g_norm2), row(b_gate)
    gq_g = row(jnp.tile(g_gqa_q[:, _PERM64], (1, 2)))
    gk_g = row(jnp.tile(g_gqa_k[:, _PERM64], (1, 2)))
    gmq, gmkv, gdo = row(g_mla_q), row(g_mla_kv), row(g_diff_out)

    tokens = (ctx, x)
    for l in range(depth):
        last = l == depth - 1
        dq, dk, dv, gq, gk, gv, mq, mk, mv = _proj_call(
            l, tokens, modarr, g1, w1, wuq, wukv, gq_g, gk_g, gmq, gmkv, tables, n_ctx_tiles)

        attn = functools.partial(_attn_call, n_ctx=n_ctx, lam_init=lam_inits[l], include_ctx=not last)
        od = attn("diff", l, dq, dk, dv, (lam_vec, gdo))
        og = attn("gqa", l, gq, gk, gv, ())
        om = attn("mla", l, mq, mk, mv, ())

        x_off = n_ctx_tiles if last else 0
        xmid = _merge_call(l, tokens, modarr, g1, wg, bg, od, og, om, wbd, wbg, wbm, wout,
                           x_off=x_off, n_ctx_tiles=n_ctx_tiles)
        xout = _ffn_call(l, xmid, modarr, g2, wfi, wfo, g_final.reshape(1, d) if last else None,
                         pos_off=x_off, n_ctx_tiles=n_ctx_tiles)
        tokens = (xout,)
    return xout
```

```python
import functools
import math

import jax
import jax.numpy as jnp
from jax import lax
from jax.experimental import pallas as pl
from jax.experimental.pallas import tpu as pltpu

F32 = jnp.float32
BF16 = jnp.bfloat16

GRID_W = 64
ROPE_THETA = 10000.0
EPS = 1e-6
N_MOD = 6

DIFF_HEADS = 4
HEAD_DIM = 64
GQA_KV_HEADS = 2
GQA_GROUP = 4
MLA_HEADS = 8
MLA_Q_LORA = 384
MLA_KV_LORA = 256
MLA_NOPE = 64
MLA_ROPE = 32
MLA_QK = MLA_NOPE + MLA_ROPE
MLA_V = 64
ROT64 = HEAD_DIM // 4
ROT32 = MLA_ROPE // 4
BRANCH_W = 512
N_UNITS = 4

LANES = 128
UNITS_PER_STEP = 4
SUM_ROWS = 16
LOG2E = math.log2(math.e)
TOKEN_TILE = 256
VMEM_LIMIT = 56 * 1024 * 1024

OFF_DQ, OFF_DK, OFF_DV, OFF_GQ, OFF_GK, OFF_GV = 0, 512, 1024, 1536, 2048, 2176
OFF_MCQ, OFF_MCKV, OFF_MKR, W1_WIDTH = 2304, 2688, 2944, 3072


def _lambda_init(layer):
    return 0.8 - 0.6 * math.exp(-0.3 * layer)


_IN_GQ, _IN_GK, _IN_MKR, _IN_GATE = 1536, 2048, 2944, 2976


def _pair_heads(a, axis):
    shape = a.shape
    a = a.reshape(shape[:axis] + (GQA_KV_HEADS, GQA_GROUP, HEAD_DIM) + shape[axis + 1:])
    return jnp.swapaxes(a, axis, axis + 1).reshape(shape)


def _prep_weights(w_in, w_mla_uq, w_mla_ukv, w_br_gqa):
    depth = w_in.shape[0]
    mkr = jnp.pad(w_in[:, :, _IN_MKR:_IN_GATE], ((0, 0), (0, 0), (MLA_NOPE, LANES - MLA_QK)))
    w1 = jnp.concatenate([w_in[:, :, :_IN_GQ], _pair_heads(w_in[:, :, _IN_GQ:_IN_GK], 2),
                          w_in[:, :, _IN_GK:_IN_MKR], mkr], axis=2).astype(BF16)
    wg = w_in[:, :, _IN_GATE:].astype(BF16)
    uq = w_mla_uq.reshape(depth, MLA_Q_LORA, MLA_HEADS, MLA_QK)
    uq = jnp.pad(uq, ((0, 0), (0, 0), (0, 0), (0, LANES - MLA_QK)))
    uq = uq.reshape(depth, MLA_Q_LORA, MLA_HEADS * LANES).astype(BF16)
    ukv = w_mla_ukv.reshape(depth, MLA_KV_LORA, MLA_HEADS, MLA_NOPE + MLA_V)
    ukn = jnp.pad(ukv[..., :MLA_NOPE], ((0, 0), (0, 0), (0, 0), (0, LANES - MLA_NOPE)))
    ukn = ukn.reshape(depth, MLA_KV_LORA, -1)
    uv = ukv[..., MLA_NOPE:].reshape(depth, MLA_KV_LORA, -1)
    ukv2 = jnp.concatenate([ukn, uv], axis=2).astype(BF16)
    wbg = _pair_heads(w_br_gqa, 1).astype(BF16)
    return w1, wg, uq, ukv2, wbg


def _rope_tables(seq, n_ctx):
    rows = seq // GRID_W
    row = jnp.repeat(jnp.arange(rows, dtype=jnp.int32), GRID_W).astype(F32)
    col = jnp.tile(jnp.arange(GRID_W, dtype=jnp.int32), rows).astype(F32)

    def table(dim):
        a = dim // 2
        freqs = ROPE_THETA ** (-jnp.arange(0, a, 2, dtype=F32) / a)
        ang_r, ang_c = row[:, None] * freqs, col[:, None] * freqs
        cos = jnp.concatenate([jnp.cos(ang_r), jnp.cos(ang_r), jnp.cos(ang_c), jnp.cos(ang_c)], axis=-1)
        sin = jnp.concatenate([-jnp.sin(ang_r), jnp.sin(ang_r), -jnp.sin(ang_c), jnp.sin(ang_c)], axis=-1)
        return cos, sin

    c64, s64 = table(HEAD_DIM)
    c64, s64 = jnp.tile(c64, (1, 2)), jnp.tile(s64, (1, 2))
    c32, s32 = table(MLA_ROPE)
    pad = ((0, 0), (MLA_NOPE, LANES - MLA_QK))
    c32 = jnp.pad(c32, pad, constant_values=1.0)
    s32 = jnp.pad(s32, pad)
    ctx_pad = ((n_ctx, 0), (0, 0))
    return (jnp.pad(c64, ctx_pad, constant_values=1.0), jnp.pad(s64, ctx_pad),
            jnp.pad(c32, ctx_pad, constant_values=1.0), jnp.pad(s32, ctx_pad))


def _rms(x):
    return x * lax.rsqrt(jnp.mean(x * x, axis=-1, keepdims=True) + EPS)


def _adaln(x, g, shift, scale):
    return (_rms(x) * g) * (1.0 + scale) + shift


def _rope(z, cos, sin, is_x1, shift):
    partner = jnp.where(is_x1, pltpu.roll(z, LANES - shift, 1), pltpu.roll(z, shift, 1))
    return z * cos + partner * sin


def _half_rms(z, lo, g):
    sq = z * z
    zero = jnp.zeros_like(sq)
    s_lo = jnp.sum(jnp.where(lo, sq, zero), axis=-1, keepdims=True)
    s_hi = jnp.sum(jnp.where(lo, zero, sq), axis=-1, keepdims=True)
    ms = jnp.where(lo, s_lo, s_hi) * (1.0 / HEAD_DIM)
    return z * lax.rsqrt(ms + EPS) * g


def _dot(a, b):
    return jnp.dot(a, b, preferred_element_type=F32)


def _const_spec(shape):
    return pl.BlockSpec(shape, lambda *_: (0,) * len(shape))


def _layer_spec(stacked, layer):
    shape = stacked.shape[1:]
    return pl.BlockSpec((None,) + shape, lambda *_: (layer,) + (0,) * len(shape))


def _mod_spec(modarr, layer, pos_off, n_ctx_tiles):
    return pl.BlockSpec((None, None, None) + modarr.shape[3:],
                        lambda bi, ti: (layer, bi, (ti + pos_off >= n_ctx_tiles).astype(jnp.int32), 0, 0))


def _token_specs(tokens, tile_off, n_ctx_tiles):
    tm, d = TOKEN_TILE, tokens[0].shape[-1]
    if len(tokens) == 1:
        return [pl.BlockSpec((None, tm, d), lambda bi, ti: (bi, ti + tile_off, 0))]
    assert tile_off == 0
    return [pl.BlockSpec((None, tm, d), lambda bi, ti: (bi, jnp.minimum(ti, n_ctx_tiles - 1), 0)),
            pl.BlockSpec((None, tm, d), lambda bi, ti: (bi, jnp.maximum(ti - n_ctx_tiles, 0), 0))]


def _read_tokens(tok_refs, n_ctx_tiles):
    if len(tok_refs) == 1:
        return tok_refs[0][...]
    ctx_ref, x_ref = tok_refs
    return jnp.where(pl.program_id(1) < n_ctx_tiles, ctx_ref[...], x_ref[...])


def _mod_kernel(c_ref, w_ref, b_ref, o_ref):
    cc = c_ref[...]
    a = (cc * jax.nn.sigmoid(cc)).astype(BF16)
    o_ref[...] = _dot(a, w_ref[...].astype(BF16)) + b_ref[...]


def _mod_call(cc, w_mod, b_mod):
    depth, d, n = w_mod.shape
    tn = 1536
    rows = cc.shape[0]
    return pl.pallas_call(
        _mod_kernel,
        grid=(depth, n // tn),
        in_specs=[
            pl.BlockSpec((rows, d), lambda l, j: (0, 0)),
            pl.BlockSpec((None, d, tn), lambda l, j: (l, 0, j)),
            pl.BlockSpec((None, 1, tn), lambda l, j: (l, 0, j)),
        ],
        out_specs=pl.BlockSpec((None, rows, tn), lambda l, j: (l, 0, j)),
        out_shape=jax.ShapeDtypeStruct((depth, rows, n), F32),
        compiler_params=pltpu.CompilerParams(vmem_limit_bytes=VMEM_LIMIT),
        name="mod_vectors",
    )(cc, w_mod, b_mod.reshape(depth, 1, n))


def _proj_kernel(*refs, n_tok, n_ctx_tiles):
    (mod_ref, g1_ref, w1_ref, wuq_ref, wukv_ref, gq_g_ref, gk_g_ref, gmq_ref, gmkv_ref,
     c64_ref, s64_ref, c32_ref, s32_ref,
     dq_o, dk_o, dv_o, gq_o, gk_o, gv_o, mq_o, mk_o, mv_o) = refs[n_tok:]
    x = _read_tokens(refs[:n_tok], n_ctx_tiles)
    hb = _adaln(x, g1_ref[...], mod_ref[0:1, :], mod_ref[1:2, :]).astype(BF16)
    tm = x.shape[0]
    lane = lax.broadcasted_iota(jnp.int32, (tm, LANES), 1)
    lo = lane < HEAD_DIM
    x1_64 = (lane & ROT64) == 0
    x1_32 = (lane & ROT32) == 0
    c64, s64 = c64_ref[...], s64_ref[...]
    c32, s32 = c32_ref[...], s32_ref[...]
    qk_scale = HEAD_DIM ** -0.5 * LOG2E

    def seg(off, width):
        return _dot(hb, w1_ref[:, off:off + width])

    def blocks(z):
        return [(slice(i * LANES, (i + 1) * LANES), z[:, i * LANES:(i + 1) * LANES]) for i in range(z.shape[1] // LANES)]

    for sl, z in blocks(seg(OFF_DQ, BRANCH_W)):
        dq_o[:, sl] = (_rope(z, c64, s64, x1_64, ROT64) * qk_scale).astype(BF16)
    for sl, z in blocks(seg(OFF_DK, BRANCH_W)):
        dk_o[:, sl] = _rope(z, c64, s64, x1_64, ROT64).astype(BF16)
    dv_o[...] = seg(OFF_DV, BRANCH_W).astype(BF16)
    for sl, z in blocks(seg(OFF_GQ, BRANCH_W)):
        gq = _half_rms(z, lo, gq_g_ref[...])
        gq_o[:, sl] = (_rope(gq, c64, s64, x1_64, ROT64) * qk_scale).astype(BF16)
    gkv = seg(OFF_GK, 2 * LANES)
    gk = _half_rms(gkv[:, :LANES], lo, gk_g_ref[...])
    gk_o[...] = _rope(gk, c64, s64, x1_64, ROT64).astype(BF16)
    gv_o[...] = gkv[:, LANES:].astype(BF16)

    zm = seg(OFF_MCQ, W1_WIDTH - OFF_MCQ)
    mcq = (_rms(zm[:, :MLA_Q_LORA]) * gmq_ref[...]).astype(BF16)
    mckv = (_rms(zm[:, MLA_Q_LORA:MLA_Q_LORA + MLA_KV_LORA]) * gmkv_ref[...]).astype(BF16)
    mkr = _rope(zm[:, MLA_Q_LORA + MLA_KV_LORA:], c32, s32, x1_32, ROT32)
    mla_scale = MLA_QK ** -0.5 * LOG2E
    for sl, z in blocks(_dot(mcq, wuq_ref[...])):
        mq_o[:, sl] = (_rope(z, c32, s32, x1_32, ROT32) * mla_scale).astype(BF16)
    mkv = _dot(mckv, wukv_ref[...])
    for sl, z in blocks(mkv[:, :MLA_HEADS * LANES]):
        mk_o[:, sl] = (z + mkr).astype(BF16)
    mv_o[...] = mkv[:, MLA_HEADS * LANES:].astype(BF16)


def _proj_call(layer, tokens, modarr, g1, w1, wuq, wukv, gq_g, gk_g, gmq, gmkv, tables, n_ctx_tiles):
    b = tokens[0].shape[0]
    t = sum(a.shape[1] for a in tokens)
    tm = TOKEN_TILE
    tok = lambda w: pl.BlockSpec((None, tm, w), lambda bi, ti: (bi, ti, 0))
    tab = pl.BlockSpec((tm, LANES), lambda bi, ti: (ti, 0))
    widths = (BRANCH_W, BRANCH_W, BRANCH_W, BRANCH_W, LANES, LANES, MLA_HEADS * LANES, MLA_HEADS * LANES, BRANCH_W)
    return pl.pallas_call(
        functools.partial(_proj_kernel, n_tok=len(tokens), n_ctx_tiles=n_ctx_tiles),
        grid=(b, t // tm),
        in_specs=_token_specs(tokens, 0, n_ctx_tiles) + [_mod_spec(modarr, layer, 0, n_ctx_tiles)] + [
            _layer_spec(a, layer) for a in (g1, w1, wuq, wukv, gq_g, gk_g, gmq, gmkv)] + [tab, tab, tab, tab],
        out_specs=[tok(w) for w in widths],
        out_shape=[jax.ShapeDtypeStruct((b, t, w), BF16) for w in widths],
        compiler_params=pltpu.CompilerParams(
            dimension_semantics=("parallel", "arbitrary"), vmem_limit_bytes=VMEM_LIMIT),
        name="input_projection",
    )(*tokens, modarr, g1, w1, wuq, wukv, gq_g, gk_g, gmq, gmkv, *tables)


def _attn_kernel(*refs, kind, n_ctx, lam_init, first_tile_is_ctx, upb):
    if kind == "diff":
        q_ref, k_ref, v_ref, lam_ref, gdo_ref, o_ref, vt_ref = refs
    else:
        q_ref, k_ref, v_ref, o_ref, vt_ref = refs
    tq = q_ref.shape[0]
    lane = lax.broadcasted_iota(jnp.int32, (tq, LANES), 1)
    lo = lane < HEAD_DIM
    n_keys = k_ref.shape[0]
    n_kv = vt_ref.shape[0]
    qw = q_ref.shape[1] // upb
    kw = k_ref.shape[1] // n_kv

    @pl.when(pl.program_id(2) == 0)
    def _():
        for i in range(n_kv):
            vt_ref[i, 0:LANES, :] = v_ref[:, i * LANES:(i + 1) * LANES].astype(F32).T.astype(BF16)
            vt_ref[i, LANES:, :] = jnp.ones((SUM_ROWS, n_keys), BF16)

    def scores(u, nk):
        q = q_ref[:, u * qw:(u + 1) * qw]
        kv = u % n_kv
        k = k_ref[0:nk, kv * kw:(kv + 1) * kw]
        if kind == "mla":
            qa, qb = q[:, :LANES], q[:, LANES:]
            ka, kb = k[:, :LANES], k[:, LANES:]
        else:
            zero = jnp.zeros_like(q)
            qa, qb = jnp.where(lo, q, zero), jnp.where(lo, zero, q)
            ka = kb = k
        nt = (((1,), (1,)), ((), ()))
        return (lax.dot_general(ka, qa, nt, preferred_element_type=F32),
                lax.dot_general(kb, qb, nt, preferred_element_type=F32))

    def softmax_pv(u, nk, st):
        p = jnp.exp2((st - jnp.max(st, axis=0, keepdims=True)).astype(BF16))
        ot = _dot(vt_ref[u % n_kv, :, 0:nk], p)
        return ot[0:LANES] / ot[LANES:LANES + 1]

    def finish(u, nk, sa, sb):
        oa, ob = softmax_pv(u, nk, sa), softmax_pv(u, nk, sb)
        if kind == "diff":
            o = (oa - lam_ref[0:1, 0:1] * ob).T
            o = _rms(o) * gdo_ref[...] * (1.0 - lam_init)
        else:
            row = lax.broadcasted_iota(jnp.int32, oa.shape, 0)
            o = jnp.where(row < HEAD_DIM, oa, ob).T
        o_ref[:, u * LANES:(u + 1) * LANES] = o.astype(BF16)

    def compute(nk):
        pending = scores(0, nk)
        for u in range(upb):
            nxt = scores(u + 1, nk) if u + 1 < upb else None
            finish(u, nk, *pending)
            pending = nxt

    if first_tile_is_ctx:
        j = pl.program_id(2)
        pl.when(j == 0)(lambda: compute(n_ctx))
        pl.when(j > 0)(lambda: compute(n_keys))
    else:
        compute(n_keys)


def _attn_call(kind, layer, q, k, v, extra, *, n_ctx, lam_init, include_ctx):
    b, t, _ = k.shape
    tq = TOKEN_TILE
    upb = UNITS_PER_STEP
    n_ctx_tiles = n_ctx // tq
    q_off = 0 if include_ctx else n_ctx_tiles
    nq = t // tq - q_off
    qw = q.shape[-1] // N_UNITS
    shared_kv = kind == "gqa"
    n_kv = 1 if shared_kv else upb
    kw = LANES if shared_kv else k.shape[-1] // N_UNITS
    kv_map = (lambda bi, g, j: (bi, 0, 0)) if shared_kv else (lambda bi, g, j: (bi, 0, g))
    in_specs = [
        pl.BlockSpec((None, tq, upb * qw), lambda bi, g, j: (bi, j + q_off, g)),
        pl.BlockSpec((None, t, n_kv * kw), kv_map),
        pl.BlockSpec((None, t, n_kv * LANES), kv_map),
    ]
    in_specs += [_layer_spec(a, layer) for a in extra]
    return pl.pallas_call(
        functools.partial(_attn_kernel, kind=kind, n_ctx=n_ctx, lam_init=lam_init,
                          first_tile_is_ctx=include_ctx, upb=upb),
        grid=(b, N_UNITS // upb, nq),
        in_specs=in_specs,
        out_specs=pl.BlockSpec((None, tq, upb * LANES), lambda bi, g, j: (bi, j, g)),
        out_shape=jax.ShapeDtypeStruct((b, nq * tq, BRANCH_W), BF16),
        scratch_shapes=[pltpu.VMEM((n_kv, LANES + SUM_ROWS, t), BF16)],
        compiler_params=pltpu.CompilerParams(
            dimension_semantics=("parallel", "arbitrary", "arbitrary"), vmem_limit_bytes=VMEM_LIMIT),
        name="attention_" + kind,
    )(q, k, v, *extra)


def _merge_kernel(*refs, n_tok, n_ctx_tiles):
    (mod_ref, g1_ref, wg_ref, bg_ref, od_ref, og_ref, om_ref,
     wbd_ref, wbg_ref, wbm_ref, wout_ref, xo_ref) = refs[n_tok:]
    x = _read_tokens(refs[:n_tok], n_ctx_tiles)
    d = x.shape[-1]
    hb = _adaln(x, g1_ref[...], mod_ref[0:1, :], mod_ref[1:2, :]).astype(BF16)
    y = None
    for i, (o_ref, wb_ref) in enumerate(((od_ref, wbd_ref), (og_ref, wbg_ref), (om_ref, wbm_ref))):
        sl = slice(i * d, (i + 1) * d)
        gate = jax.nn.sigmoid(_dot(hb, wg_ref[:, sl]) + bg_ref[:, sl])
        term = gate * _dot(o_ref[...], wb_ref[...])
        y = term if y is None else y + term
    out = _dot(y.astype(BF16), wout_ref[...])
    xo_ref[...] = x + mod_ref[2:3, :] * out


def _merge_call(layer, tokens, modarr, g1, wg, bg, od, og, om, wbd, wbg, wbm, wout, *, x_off, n_ctx_tiles):
    b, d = tokens[0].shape[0], tokens[0].shape[-1]
    tm = TOKEN_TILE
    nt = od.shape[1] // tm
    tok = lambda w: pl.BlockSpec((None, tm, w), lambda bi, ti: (bi, ti, 0))
    return pl.pallas_call(
        functools.partial(_merge_kernel, n_tok=len(tokens), n_ctx_tiles=n_ctx_tiles),
        grid=(b, nt),
        in_specs=_token_specs(tokens, x_off, n_ctx_tiles) + [_mod_spec(modarr, layer, x_off, n_ctx_tiles)] + [
            _layer_spec(a, layer) for a in (g1, wg, bg)] + [tok(BRANCH_W)] * 3 + [
            _layer_spec(a, layer) for a in (wbd, wbg, wbm, wout)],
        out_specs=tok(d),
        out_shape=jax.ShapeDtypeStruct((b, nt * tm, d), F32),
        compiler_params=pltpu.CompilerParams(
            dimension_semantics=("parallel", "arbitrary"), vmem_limit_bytes=VMEM_LIMIT),
        name="branch_merge",
    )(*tokens, modarr, g1, wg, bg, od, og, om, wbd, wbg, wbm, wout)


def _ffn_kernel(*refs, final):
    if final:
        x_ref, mod_ref, g2_ref, win_ref, wout_ref, gf_ref, o_ref = refs
    else:
        x_ref, mod_ref, g2_ref, win_ref, wout_ref, o_ref = refs
    x = x_ref[...]
    hidden = wout_ref.shape[0]
    hb = _adaln(x, g2_ref[...], mod_ref[3:4, :], mod_ref[4:5, :]).astype(BF16)
    g = _dot(hb, win_ref[:, :hidden])
    u = _dot(hb, win_ref[:, hidden:])
    a = (g * jax.nn.sigmoid(g) * u).astype(BF16)
    xn = x + mod_ref[5:6, :] * _dot(a, wout_ref[...])
    if final:
        xn = _rms(xn) * gf_ref[...]
    o_ref[...] = xn


def _ffn_call(layer, xin, modarr, g2, win, wout, gf, *, pos_off, n_ctx_tiles):
    b, t, d = xin.shape
    tm = TOKEN_TILE
    tok = pl.BlockSpec((None, tm, d), lambda bi, ti: (bi, ti, 0))
    final = gf is not None
    in_specs = [tok, _mod_spec(modarr, layer, pos_off, n_ctx_tiles)] + [
        _layer_spec(a, layer) for a in (g2, win, wout)]
    args = [xin, modarr, g2, win, wout]
    if final:
        in_specs.append(_const_spec((1, d)))
        args.append(gf)
    return pl.pallas_call(
        functools.partial(_ffn_kernel, final=final),
        grid=(b, t // tm),
        in_specs=in_specs,
        out_specs=tok,
        out_shape=jax.ShapeDtypeStruct((b, t, d), F32),
        compiler_params=pltpu.CompilerParams(
            dimension_semantics=("parallel", "arbitrary"), vmem_limit_bytes=VMEM_LIMIT),
        name="swiglu",
    )(*args)


def kernel(x, c, ctx, c_ctx, w_mod, b_mod, g_norm1, w_in, b_gate, lam_q1, lam_k1, lam_q2, lam_k2, g_diff_out, g_gqa_q, g_gqa_k, g_mla_q, w_mla_uq, g_mla_kv, w_mla_ukv, w_br_diff, w_br_gqa, w_br_mla, w_out, g_norm2, w_ffn_in, w_ffn_out, g_final):
    b, seq, d = x.shape
    n_ctx = ctx.shape[1]
    depth = w_in.shape[0]
    assert n_ctx % TOKEN_TILE == 0 and seq % TOKEN_TILE == 0 and seq % GRID_W == 0
    n_ctx_tiles = n_ctx // TOKEN_TILE

    mod_rows = 32
    cc = jnp.zeros((mod_rows, d), F32).at[:b].set(c).at[b].set(c_ctx)
    mods = _mod_call(cc, w_mod, b_mod)
    mod_lat = mods[:, :b].reshape(depth, b, N_MOD, d)
    mod_ctx = jnp.broadcast_to(mods[:, b].reshape(depth, 1, N_MOD, d), (depth, b, N_MOD, d))
    modarr = jnp.stack([mod_ctx, mod_lat], axis=2)
    tables = _rope_tables(seq, n_ctx)

    lam_inits = [_lambda_init(l) for l in range(depth)]
    lam = (jnp.exp(jnp.sum(lam_q1 * lam_k1, axis=-1)) - jnp.exp(jnp.sum(lam_q2 * lam_k2, axis=-1))
           + jnp.asarray(lam_inits, F32))
    lam_vec = jnp.broadcast_to(lam.reshape(depth, 1, 1), (depth, 1, LANES))

    def row(a):
        return a.reshape(depth, 1, -1)

    w1, wg, wuq, wukv, wbg = _prep_weights(w_in, w_mla_uq, w_mla_ukv, w_br_gqa)
    wbd, wbm, wout = w_br_diff.astype(BF16), w_br_mla.astype(BF16), w_out.astype(BF16)
    wfi, wfo = w_ffn_in.astype(BF16), w_ffn_out.astype(BF16)
    g1, g2, bg = row(g_norm1), row(g_norm2), row(b_gate)
    gq_g, gk_g = row(jnp.tile(g_gqa_q, (1, 2))), row(jnp.tile(g_gqa_k, (1, 2)))
    gmq, gmkv, gdo = row(g_mla_q), row(g_mla_kv), row(g_diff_out)

    tokens = (ctx, x)
    for l in range(depth):
        last = l == depth - 1
        dq, dk, dv, gq, gk, gv, mq, mk, mv = _proj_call(
            l, tokens, modarr, g1, w1, wuq, wukv, gq_g, gk_g, gmq, gmkv, tables, n_ctx_tiles)

        attn = functools.partial(_attn_call, n_ctx=n_ctx, lam_init=lam_inits[l], include_ctx=not last)
        od = attn("diff", l, dq, dk, dv, (lam_vec, gdo))
        og = attn("gqa", l, gq, gk, gv, ())
        om = attn("mla", l, mq, mk, mv, ())

        x_off = n_ctx_tiles if last else 0
        xmid = _merge_call(l, tokens, modarr, g1, wg, bg, od, og, om, wbd, wbg, wbm, wout,
                           x_off=x_off, n_ctx_tiles=n_ctx_tiles)
        xout = _ffn_call(l, xmid, modarr, g2, wfi, wfo, g_final.reshape(1, d) if last else None,
                         pos_off=x_off, n_ctx_tiles=n_ctx_tiles)
        tokens = (xout,)
    return xout
```

```python
import functools
import math

import jax
import jax.numpy as jnp
from jax import lax
from jax.experimental import pallas as pl
from jax.experimental.pallas import tpu as pltpu

F32 = jnp.float32
BF16 = jnp.bfloat16

GRID_W = 64
ROPE_THETA = 10000.0
EPS = 1e-6
N_MOD = 6

DIFF_HEADS = 4
HEAD_DIM = 64
GQA_KV_HEADS = 2
GQA_GROUP = 4
MLA_HEADS = 8
MLA_Q_LORA = 384
MLA_KV_LORA = 256
MLA_NOPE = 64
MLA_ROPE = 32
MLA_QK = MLA_NOPE + MLA_ROPE
MLA_V = 64
ROT64 = HEAD_DIM // 4
ROT32 = MLA_ROPE // 4
BRANCH_W = 512
N_UNITS = 4

LANES = 128
UNITS_PER_STEP = 4
KEY_CHUNK = 256
SUM_ROWS = 16
LOG2E = math.log2(math.e)
TOKEN_TILE = 256
VMEM_LIMIT = 56 * 1024 * 1024

OFF_DQ, OFF_DK, OFF_DV, OFF_GQ, OFF_GK, OFF_GV = 0, 512, 1024, 1536, 2048, 2176
OFF_MCQ, OFF_MCKV, OFF_MKR, W1_WIDTH = 2304, 2688, 2944, 3072


def _lambda_init(layer):
    return 0.8 - 0.6 * math.exp(-0.3 * layer)


_IN_GQ, _IN_GK, _IN_MKR, _IN_GATE = 1536, 2048, 2944, 2976


def _pair_heads(a, axis):
    shape = a.shape
    a = a.reshape(shape[:axis] + (GQA_KV_HEADS, GQA_GROUP, HEAD_DIM) + shape[axis + 1:])
    return jnp.swapaxes(a, axis, axis + 1).reshape(shape)


def _prep_weights(w_in, w_mla_uq, w_mla_ukv, w_br_gqa):
    depth = w_in.shape[0]
    mkr = jnp.pad(w_in[:, :, _IN_MKR:_IN_GATE], ((0, 0), (0, 0), (MLA_NOPE, LANES - MLA_QK)))
    w1 = jnp.concatenate([w_in[:, :, :_IN_GQ], _pair_heads(w_in[:, :, _IN_GQ:_IN_GK], 2),
                          w_in[:, :, _IN_GK:_IN_MKR], mkr], axis=2).astype(BF16)
    wg = w_in[:, :, _IN_GATE:].astype(BF16)
    uq = w_mla_uq.reshape(depth, MLA_Q_LORA, MLA_HEADS, MLA_QK)
    uq = jnp.pad(uq, ((0, 0), (0, 0), (0, 0), (0, LANES - MLA_QK)))
    uq = uq.reshape(depth, MLA_Q_LORA, MLA_HEADS * LANES).astype(BF16)
    ukv = w_mla_ukv.reshape(depth, MLA_KV_LORA, MLA_HEADS, MLA_NOPE + MLA_V)
    ukn = jnp.pad(ukv[..., :MLA_NOPE], ((0, 0), (0, 0), (0, 0), (0, LANES - MLA_NOPE)))
    ukn = ukn.reshape(depth, MLA_KV_LORA, -1)
    uv = ukv[..., MLA_NOPE:].reshape(depth, MLA_KV_LORA, -1)
    ukv2 = jnp.concatenate([ukn, uv], axis=2).astype(BF16)
    wbg = _pair_heads(w_br_gqa, 1).astype(BF16)
    return w1, wg, uq, ukv2, wbg


def _rope_tables(seq, n_ctx):
    rows = seq // GRID_W
    row = jnp.repeat(jnp.arange(rows, dtype=jnp.int32), GRID_W).astype(F32)
    col = jnp.tile(jnp.arange(GRID_W, dtype=jnp.int32), rows).astype(F32)

    def table(dim):
        a = dim // 2
        freqs = ROPE_THETA ** (-jnp.arange(0, a, 2, dtype=F32) / a)
        ang_r, ang_c = row[:, None] * freqs, col[:, None] * freqs
        cos = jnp.concatenate([jnp.cos(ang_r), jnp.cos(ang_r), jnp.cos(ang_c), jnp.cos(ang_c)], axis=-1)
        sin = jnp.concatenate([-jnp.sin(ang_r), jnp.sin(ang_r), -jnp.sin(ang_c), jnp.sin(ang_c)], axis=-1)
        return cos, sin

    c64, s64 = table(HEAD_DIM)
    c64, s64 = jnp.tile(c64, (1, 2)), jnp.tile(s64, (1, 2))
    c32, s32 = table(MLA_ROPE)
    pad = ((0, 0), (MLA_NOPE, LANES - MLA_QK))
    c32 = jnp.pad(c32, pad, constant_values=1.0)
    s32 = jnp.pad(s32, pad)
    ctx_pad = ((n_ctx, 0), (0, 0))
    return (jnp.pad(c64, ctx_pad, constant_values=1.0), jnp.pad(s64, ctx_pad),
            jnp.pad(c32, ctx_pad, constant_values=1.0), jnp.pad(s32, ctx_pad))


def _rms(x):
    return x * lax.rsqrt(jnp.mean(x * x, axis=-1, keepdims=True) + EPS)


def _adaln(x, g, shift, scale):
    return (_rms(x) * g) * (1.0 + scale) + shift


def _rope(z, cos, sin, is_x1, shift):
    partner = jnp.where(is_x1, pltpu.roll(z, LANES - shift, 1), pltpu.roll(z, shift, 1))
    return z * cos + partner * sin


def _half_rms(z, lo, g):
    sq = z * z
    zero = jnp.zeros_like(sq)
    s_lo = jnp.sum(jnp.where(lo, sq, zero), axis=-1, keepdims=True)
    s_hi = jnp.sum(jnp.where(lo, zero, sq), axis=-1, keepdims=True)
    ms = jnp.where(lo, s_lo, s_hi) * (1.0 / HEAD_DIM)
    return z * lax.rsqrt(ms + EPS) * g


def _dot(a, b):
    return jnp.dot(a, b, preferred_element_type=F32)


def _const_spec(shape):
    return pl.BlockSpec(shape, lambda *_: (0,) * len(shape))


def _layer_spec(stacked, layer):
    shape = stacked.shape[1:]
    return pl.BlockSpec((None,) + shape, lambda *_: (layer,) + (0,) * len(shape))


def _mod_spec(modarr, layer, pos_off, n_ctx_tiles):
    return pl.BlockSpec((None, None, None) + modarr.shape[3:],
                        lambda bi, ti: (layer, bi, (ti + pos_off >= n_ctx_tiles).astype(jnp.int32), 0, 0))


def _token_specs(tokens, tile_off, n_ctx_tiles):
    tm, d = TOKEN_TILE, tokens[0].shape[-1]
    if len(tokens) == 1:
        return [pl.BlockSpec((None, tm, d), lambda bi, ti: (bi, ti + tile_off, 0))]
    assert tile_off == 0
    return [pl.BlockSpec((None, tm, d), lambda bi, ti: (bi, jnp.minimum(ti, n_ctx_tiles - 1), 0)),
            pl.BlockSpec((None, tm, d), lambda bi, ti: (bi, jnp.maximum(ti - n_ctx_tiles, 0), 0))]


def _read_tokens(tok_refs, n_ctx_tiles):
    if len(tok_refs) == 1:
        return tok_refs[0][...]
    ctx_ref, x_ref = tok_refs
    return jnp.where(pl.program_id(1) < n_ctx_tiles, ctx_ref[...], x_ref[...])


def _mod_kernel(c_ref, w_ref, b_ref, o_ref):
    cc = c_ref[...]
    a = (cc * jax.nn.sigmoid(cc)).astype(BF16)
    o_ref[...] = _dot(a, w_ref[...].astype(BF16)) + b_ref[...]


def _mod_call(cc, w_mod, b_mod):
    depth, d, n = w_mod.shape
    tn = 1536
    rows = cc.shape[0]
    return pl.pallas_call(
        _mod_kernel,
        grid=(depth, n // tn),
        in_specs=[
            pl.BlockSpec((rows, d), lambda l, j: (0, 0)),
            pl.BlockSpec((None, d, tn), lambda l, j: (l, 0, j)),
            pl.BlockSpec((None, 1, tn), lambda l, j: (l, 0, j)),
        ],
        out_specs=pl.BlockSpec((None, rows, tn), lambda l, j: (l, 0, j)),
        out_shape=jax.ShapeDtypeStruct((depth, rows, n), F32),
        compiler_params=pltpu.CompilerParams(vmem_limit_bytes=VMEM_LIMIT),
        name="mod_vectors",
    )(cc, w_mod, b_mod.reshape(depth, 1, n))


def _proj_kernel(*refs, n_tok, n_ctx_tiles):
    (mod_ref, g1_ref, w1_ref, wuq_ref, wukv_ref, gq_g_ref, gk_g_ref, gmq_ref, gmkv_ref,
     c64_ref, s64_ref, c32_ref, s32_ref,
     dq_o, dk_o, dv_o, gq_o, gk_o, gv_o, mq_o, mk_o, mv_o) = refs[n_tok:]
    x = _read_tokens(refs[:n_tok], n_ctx_tiles)
    hb = _adaln(x, g1_ref[...], mod_ref[0:1, :], mod_ref[1:2, :]).astype(BF16)
    tm = x.shape[0]
    lane = lax.broadcasted_iota(jnp.int32, (tm, LANES), 1)
    lo = lane < HEAD_DIM
    x1_64 = (lane & ROT64) == 0
    x1_32 = (lane & ROT32) == 0
    c64, s64 = c64_ref[...], s64_ref[...]
    c32, s32 = c32_ref[...], s32_ref[...]
    qk_scale = HEAD_DIM ** -0.5 * LOG2E

    def seg(off, width):
        return _dot(hb, w1_ref[:, off:off + width])

    def blocks(z):
        return [(slice(i * LANES, (i + 1) * LANES), z[:, i * LANES:(i + 1) * LANES]) for i in range(z.shape[1] // LANES)]

    for sl, z in blocks(seg(OFF_DQ, BRANCH_W)):
        dq_o[:, sl] = (_rope(z, c64, s64, x1_64, ROT64) * qk_scale).astype(BF16)
    for sl, z in blocks(seg(OFF_DK, BRANCH_W)):
        dk_o[:, sl] = _rope(z, c64, s64, x1_64, ROT64).astype(BF16)
    dv_o[...] = seg(OFF_DV, BRANCH_W).astype(BF16)
    for sl, z in blocks(seg(OFF_GQ, BRANCH_W)):
        gq = _half_rms(z, lo, gq_g_ref[...])
        gq_o[:, sl] = (_rope(gq, c64, s64, x1_64, ROT64) * qk_scale).astype(BF16)
    gkv = seg(OFF_GK, 2 * LANES)
    gk = _half_rms(gkv[:, :LANES], lo, gk_g_ref[...])
    gk_o[...] = _rope(gk, c64, s64, x1_64, ROT64).astype(BF16)
    gv_o[...] = gkv[:, LANES:].astype(BF16)

    zm = seg(OFF_MCQ, W1_WIDTH - OFF_MCQ)
    mcq = (_rms(zm[:, :MLA_Q_LORA]) * gmq_ref[...]).astype(BF16)
    mckv = (_rms(zm[:, MLA_Q_LORA:MLA_Q_LORA + MLA_KV_LORA]) * gmkv_ref[...]).astype(BF16)
    mkr = _rope(zm[:, MLA_Q_LORA + MLA_KV_LORA:], c32, s32, x1_32, ROT32)
    mla_scale = MLA_QK ** -0.5 * LOG2E
    for sl, z in blocks(_dot(mcq, wuq_ref[...])):
        mq_o[:, sl] = (_rope(z, c32, s32, x1_32, ROT32) * mla_scale).astype(BF16)
    mkv = _dot(mckv, wukv_ref[...])
    for sl, z in blocks(mkv[:, :MLA_HEADS * LANES]):
        mk_o[:, sl] = (z + mkr).astype(BF16)
    mv_o[...] = mkv[:, MLA_HEADS * LANES:].astype(BF16)


def _proj_call(layer, tokens, modarr, g1, w1, wuq, wukv, gq_g, gk_g, gmq, gmkv, tables, n_ctx_tiles):
    b = tokens[0].shape[0]
    t = sum(a.shape[1] for a in tokens)
    tm = TOKEN_TILE
    tok = lambda w: pl.BlockSpec((None, tm, w), lambda bi, ti: (bi, ti, 0))
    tab = pl.BlockSpec((tm, LANES), lambda bi, ti: (ti, 0))
    widths = (BRANCH_W, BRANCH_W, BRANCH_W, BRANCH_W, LANES, LANES, MLA_HEADS * LANES, MLA_HEADS * LANES, BRANCH_W)
    return pl.pallas_call(
        functools.partial(_proj_kernel, n_tok=len(tokens), n_ctx_tiles=n_ctx_tiles),
        grid=(b, t // tm),
        in_specs=_token_specs(tokens, 0, n_ctx_tiles) + [_mod_spec(modarr, layer, 0, n_ctx_tiles)] + [
            _layer_spec(a, layer) for a in (g1, w1, wuq, wukv, gq_g, gk_g, gmq, gmkv)] + [tab, tab, tab, tab],
        out_specs=[tok(w) for w in widths],
        out_shape=[jax.ShapeDtypeStruct((b, t, w), BF16) for w in widths],
        compiler_params=pltpu.CompilerParams(
            dimension_semantics=("parallel", "arbitrary"), vmem_limit_bytes=VMEM_LIMIT),
        name="input_projection",
    )(*tokens, modarr, g1, w1, wuq, wukv, gq_g, gk_g, gmq, gmkv, *tables)


def _attn_kernel(*refs, kind, n_ctx, lam_init, first_tile_is_ctx, upb):
    if kind == "diff":
        q_ref, k_ref, v_ref, lam_ref, gdo_ref, o_ref, vt_ref = refs
    else:
        q_ref, k_ref, v_ref, o_ref, vt_ref = refs
    tq = q_ref.shape[0]
    lane = lax.broadcasted_iota(jnp.int32, (tq, LANES), 1)
    lo = lane < HEAD_DIM
    n_keys = k_ref.shape[0]
    n_kv = vt_ref.shape[0]
    qw = q_ref.shape[1] // upb
    kw = k_ref.shape[1] // n_kv

    @pl.when(pl.program_id(2) == 0)
    def _():
        for i in range(n_kv):
            vt_ref[i, 0:LANES, :] = v_ref[:, i * LANES:(i + 1) * LANES].astype(F32).T.astype(BF16)
            vt_ref[i, LANES:, :] = jnp.ones((SUM_ROWS, n_keys), BF16)

    def key_chunks(nk):
        step = KEY_CHUNK if nk % KEY_CHUNK == 0 else nk
        return [(c, c + step) for c in range(0, nk, step)]

    def scores(u, nk):
        q = q_ref[:, u * qw:(u + 1) * qw]
        kv = u % n_kv
        if kind == "mla":
            qa, qb = q[:, :LANES], q[:, LANES:]
        else:
            zero = jnp.zeros_like(q)
            qa, qb = jnp.where(lo, q, zero), jnp.where(lo, zero, q)
        nt = (((1,), (1,)), ((), ()))
        out = ([], [])
        for c0, c1 in key_chunks(nk):
            k = k_ref[c0:c1, kv * kw:(kv + 1) * kw]
            ka, kb = (k[:, :LANES], k[:, LANES:]) if kind == "mla" else (k, k)
            out[0].append(lax.dot_general(ka, qa, nt, preferred_element_type=F32))
            out[1].append(lax.dot_general(kb, qb, nt, preferred_element_type=F32))
        return out

    def softmax_pv(u, nk, st_chunks):
        maxes, accs = [], []
        for (c0, c1), st in zip(key_chunks(nk), st_chunks):
            m = jnp.max(st, axis=0, keepdims=True)
            p = jnp.exp2((st - m).astype(BF16))
            accs.append(_dot(vt_ref[u % n_kv, :, c0:c1], p))
            maxes.append(m)
        if len(accs) == 1:
            ot = accs[0]
        else:
            m_all = functools.reduce(jnp.maximum, maxes)
            ot = sum(acc * jnp.exp2(m - m_all) for acc, m in zip(accs, maxes))
        return ot[0:LANES] / ot[LANES:LANES + 1]

    def finish(u, nk, sa, sb):
        oa, ob = softmax_pv(u, nk, sa), softmax_pv(u, nk, sb)
        if kind == "diff":
            o = (oa - lam_ref[0:1, 0:1] * ob).T
            o = _rms(o) * gdo_ref[...] * (1.0 - lam_init)
        else:
            row = lax.broadcasted_iota(jnp.int32, oa.shape, 0)
            o = jnp.where(row < HEAD_DIM, oa, ob).T
        o_ref[:, u * LANES:(u + 1) * LANES] = o.astype(BF16)

    def compute(nk):
        pending = scores(0, nk)
        for u in range(upb):
            nxt = scores(u + 1, nk) if u + 1 < upb else None
            finish(u, nk, *pending)
            pending = nxt

    if first_tile_is_ctx:
        j = pl.program_id(2)
        pl.when(j == 0)(lambda: compute(n_ctx))
        pl.when(j > 0)(lambda: compute(n_keys))
    else:
        compute(n_keys)


def _attn_call(kind, layer, q, k, v, extra, *, n_ctx, lam_init, include_ctx):
    b, t, _ = k.shape
    tq = TOKEN_TILE
    upb = UNITS_PER_STEP
    n_ctx_tiles = n_ctx // tq
    q_off = 0 if include_ctx else n_ctx_tiles
    nq = t // tq - q_off
    qw = q.shape[-1] // N_UNITS
    shared_kv = kind == "gqa"
    n_kv = 1 if shared_kv else upb
    kw = LANES if shared_kv else k.shape[-1] // N_UNITS
    kv_map = (lambda bi, g, j: (bi, 0, 0)) if shared_kv else (lambda bi, g, j: (bi, 0, g))
    in_specs = [
        pl.BlockSpec((None, tq, upb * qw), lambda bi, g, j: (bi, j + q_off, g)),
        pl.BlockSpec((None, t, n_kv * kw), kv_map),
        pl.BlockSpec((None, t, n_kv * LANES), kv_map),
    ]
    in_specs += [_layer_spec(a, layer) for a in extra]
    return pl.pallas_call(
        functools.partial(_attn_kernel, kind=kind, n_ctx=n_ctx, lam_init=lam_init,
                          first_tile_is_ctx=include_ctx, upb=upb),
        grid=(b, N_UNITS // upb, nq),
        in_specs=in_specs,
        out_specs=pl.BlockSpec((None, tq, upb * LANES), lambda bi, g, j: (bi, j, g)),
        out_shape=jax.ShapeDtypeStruct((b, nq * tq, BRANCH_W), BF16),
        scratch_shapes=[pltpu.VMEM((n_kv, LANES + SUM_ROWS, t), BF16)],
        compiler_params=pltpu.CompilerParams(
            dimension_semantics=("parallel", "arbitrary", "arbitrary"), vmem_limit_bytes=VMEM_LIMIT),
        name="attention_" + kind,
    )(q, k, v, *extra)


def _merge_kernel(*refs, n_tok, n_ctx_tiles):
    (mod_ref, g1_ref, wg_ref, bg_ref, od_ref, og_ref, om_ref,
     wbd_ref, wbg_ref, wbm_ref, wout_ref, xo_ref) = refs[n_tok:]
    x = _read_tokens(refs[:n_tok], n_ctx_tiles)
    d = x.shape[-1]
    hb = _adaln(x, g1_ref[...], mod_ref[0:1, :], mod_ref[1:2, :]).astype(BF16)
    y = None
    for i, (o_ref, wb_ref) in enumerate(((od_ref, wbd_ref), (og_ref, wbg_ref), (om_ref, wbm_ref))):
        sl = slice(i * d, (i + 1) * d)
        gate = jax.nn.sigmoid(_dot(hb, wg_ref[:, sl]) + bg_ref[:, sl])
        term = gate * _dot(o_ref[...], wb_ref[...])
        y = term if y is None else y + term
    out = _dot(y.astype(BF16), wout_ref[...])
    xo_ref[...] = x + mod_ref[2:3, :] * out


def _merge_call(layer, tokens, modarr, g1, wg, bg, od, og, om, wbd, wbg, wbm, wout, *, x_off, n_ctx_tiles):
    b, d = tokens[0].shape[0], tokens[0].shape[-1]
    tm = TOKEN_TILE
    nt = od.shape[1] // tm
    tok = lambda w: pl.BlockSpec((None, tm, w), lambda bi, ti: (bi, ti, 0))
    return pl.pallas_call(
        functools.partial(_merge_kernel, n_tok=len(tokens), n_ctx_tiles=n_ctx_tiles),
        grid=(b, nt),
        in_specs=_token_specs(tokens, x_off, n_ctx_tiles) + [_mod_spec(modarr, layer, x_off, n_ctx_tiles)] + [
            _layer_spec(a, layer) for a in (g1, wg, bg)] + [tok(BRANCH_W)] * 3 + [
            _layer_spec(a, layer) for a in (wbd, wbg, wbm, wout)],
        out_specs=tok(d),
        out_shape=jax.ShapeDtypeStruct((b, nt * tm, d), F32),
        compiler_params=pltpu.CompilerParams(
            dimension_semantics=("parallel", "arbitrary"), vmem_limit_bytes=VMEM_LIMIT),
        name="branch_merge",
    )(*tokens, modarr, g1, wg, bg, od, og, om, wbd, wbg, wbm, wout)


def _ffn_kernel(*refs, final):
    if final:
        x_ref, mod_ref, g2_ref, win_ref, wout_ref, gf_ref, o_ref = refs
    else:
        x_ref, mod_ref, g2_ref, win_ref, wout_ref, o_ref = refs
    x = x_ref[...]
    hidden = wout_ref.shape[0]
    hb = _adaln(x, g2_ref[...], mod_ref[3:4, :], mod_ref[4:5, :]).astype(BF16)
    g = _dot(hb, win_ref[:, :hidden])
    u = _dot(hb, win_ref[:, hidden:])
    a = (g * jax.nn.sigmoid(g) * u).astype(BF16)
    xn = x + mod_ref[5:6, :] * _dot(a, wout_ref[...])
    if final:
        xn = _rms(xn) * gf_ref[...]
    o_ref[...] = xn


def _ffn_call(layer, xin, modarr, g2, win, wout, gf, *, pos_off, n_ctx_tiles):
    b, t, d = xin.shape
    tm = TOKEN_TILE
    tok = pl.BlockSpec((None, tm, d), lambda bi, ti: (bi, ti, 0))
    final = gf is not None
    in_specs = [tok, _mod_spec(modarr, layer, pos_off, n_ctx_tiles)] + [
        _layer_spec(a, layer) for a in (g2, win, wout)]
    args = [xin, modarr, g2, win, wout]
    if final:
        in_specs.append(_const_spec((1, d)))
        args.append(gf)
    return pl.pallas_call(
        functools.partial(_ffn_kernel, final=final),
        grid=(b, t // tm),
        in_specs=in_specs,
        out_specs=tok,
        out_shape=jax.ShapeDtypeStruct((b, t, d), F32),
        compiler_params=pltpu.CompilerParams(
            dimension_semantics=("parallel", "arbitrary"), vmem_limit_bytes=VMEM_LIMIT),
        name="swiglu",
    )(*args)


def kernel(x, c, ctx, c_ctx, w_mod, b_mod, g_norm1, w_in, b_gate, lam_q1, lam_k1, lam_q2, lam_k2, g_diff_out, g_gqa_q, g_gqa_k, g_mla_q, w_mla_uq, g_mla_kv, w_mla_ukv, w_br_diff, w_br_gqa, w_br_mla, w_out, g_norm2, w_ffn_in, w_ffn_out, g_final):
    b, seq, d = x.shape
    n_ctx = ctx.shape[1]
    depth = w_in.shape[0]
    assert n_ctx % TOKEN_TILE == 0 and seq % TOKEN_TILE == 0 and seq % GRID_W == 0
    n_ctx_tiles = n_ctx // TOKEN_TILE

    mod_rows = 32
    cc = jnp.zeros((mod_rows, d), F32).at[:b].set(c).at[b].set(c_ctx)
    mods = _mod_call(cc, w_mod, b_mod)
    mod_lat = mods[:, :b].reshape(depth, b, N_MOD, d)
    mod_ctx = jnp.broadcast_to(mods[:, b].reshape(depth, 1, N_MOD, d), (depth, b, N_MOD, d))
    modarr = jnp.stack([mod_ctx, mod_lat], axis=2)
    tables = _rope_tables(seq, n_ctx)

    lam_inits = [_lambda_init(l) for l in range(depth)]
    lam = (jnp.exp(jnp.sum(lam_q1 * lam_k1, axis=-1)) - jnp.exp(jnp.sum(lam_q2 * lam_k2, axis=-1))
           + jnp.asarray(lam_inits, F32))
    lam_vec = jnp.broadcast_to(lam.reshape(depth, 1, 1), (depth, 1, LANES))

    def row(a):
        return a.reshape(depth, 1, -1)

    w1, wg, wuq, wukv, wbg = _prep_weights(w_in, w_mla_uq, w_mla_ukv, w_br_gqa)
    wbd, wbm, wout = w_br_diff.astype(BF16), w_br_mla.astype(BF16), w_out.astype(BF16)
    wfi, wfo = w_ffn_in.astype(BF16), w_ffn_out.astype(BF16)
    g1, g2, bg = row(g_norm1), row(g_norm2), row(b_gate)
    gq_g, gk_g = row(jnp.tile(g_gqa_q, (1, 2))), row(jnp.tile(g_gqa_k, (1, 2)))
    gmq, gmkv, gdo = row(g_mla_q), row(g_mla_kv), row(g_diff_out)

    tokens = (ctx, x)
    for l in range(depth):
        last = l == depth - 1
        dq, dk, dv, gq, gk, gv, mq, mk, mv = _proj_call(
            l, tokens, modarr, g1, w1, wuq, wukv, gq_g, gk_g, gmq, gmkv, tables, n_ctx_tiles)

        attn = functools.partial(_attn_call, n_ctx=n_ctx, lam_init=lam_inits[l], include_ctx=not last)
        od = attn("diff", l, dq, dk, dv, (lam_vec, gdo))
        og = attn("gqa", l, gq, gk, gv, ())
        om = attn("mla", l, mq, mk, mv, ())

        x_off = n_ctx_tiles if last else 0
        xmid = _merge_call(l, tokens, modarr, g1, wg, bg, od, og, om, wbd, wbg, wbm, wout,
                           x_off=x_off, n_ctx_tiles=n_ctx_tiles)
        xout = _ffn_call(l, xmid, modarr, g2, wfi, wfo, g_final.reshape(1, d) if last else None,
                         pos_off=x_off, n_ctx_tiles=n_ctx_tiles)
        tokens = (xout,)
    return xout
```

```python
import functools
import math

import jax
import jax.numpy as jnp
from jax import lax
from jax.experimental import pallas as pl
from jax.experimental.pallas import tpu as pltpu

F32 = jnp.float32
BF16 = jnp.bfloat16

GRID_W = 64
ROPE_THETA = 10000.0
EPS = 1e-6
N_MOD = 6

DIFF_HEADS = 4
HEAD_DIM = 64
GQA_KV_HEADS = 2
GQA_GROUP = 4
MLA_HEADS = 8
MLA_Q_LORA = 384
MLA_KV_LORA = 256
MLA_NOPE = 64
MLA_ROPE = 32
MLA_QK = MLA_NOPE + MLA_ROPE
MLA_V = 64
ROT64 = HEAD_DIM // 4
ROT32 = MLA_ROPE // 4
BRANCH_W = 512
N_UNITS = 4

LANES = 128
UNITS_PER_STEP = 4
KEY_CHUNK = 256
SUM_ROWS = 16
LOG2E = math.log2(math.e)
TOKEN_TILE = 256
VMEM_LIMIT = 56 * 1024 * 1024

OFF_DQ, OFF_DK, OFF_DV, OFF_GQ, OFF_GK, OFF_GV = 0, 512, 1024, 1536, 2048, 2176
OFF_MCQ, OFF_MCKV, OFF_MKR, W1_WIDTH = 2304, 2688, 2944, 3072


def _lambda_init(layer):
    return 0.8 - 0.6 * math.exp(-0.3 * layer)


_IN_GQ, _IN_GK, _IN_MKR, _IN_GATE = 1536, 2048, 2944, 2976


def _pair_heads(a, axis):
    shape = a.shape
    a = a.reshape(shape[:axis] + (GQA_KV_HEADS, GQA_GROUP, HEAD_DIM) + shape[axis + 1:])
    return jnp.swapaxes(a, axis, axis + 1).reshape(shape)


def _prep_weights(w_in, w_mla_uq, w_mla_ukv, w_br_gqa):
    depth = w_in.shape[0]
    mkr = jnp.pad(w_in[:, :, _IN_MKR:_IN_GATE], ((0, 0), (0, 0), (MLA_NOPE, LANES - MLA_QK)))
    w1 = jnp.concatenate([w_in[:, :, :_IN_GQ], _pair_heads(w_in[:, :, _IN_GQ:_IN_GK], 2),
                          w_in[:, :, _IN_GK:_IN_MKR], mkr], axis=2).astype(BF16)
    wg = w_in[:, :, _IN_GATE:].astype(BF16)
    uq = w_mla_uq.reshape(depth, MLA_Q_LORA, MLA_HEADS, MLA_QK)
    uq = jnp.pad(uq, ((0, 0), (0, 0), (0, 0), (0, LANES - MLA_QK)))
    uq = uq.reshape(depth, MLA_Q_LORA, MLA_HEADS * LANES).astype(BF16)
    ukv = w_mla_ukv.reshape(depth, MLA_KV_LORA, MLA_HEADS, MLA_NOPE + MLA_V)
    ukn = jnp.pad(ukv[..., :MLA_NOPE], ((0, 0), (0, 0), (0, 0), (0, LANES - MLA_NOPE)))
    ukn = ukn.reshape(depth, MLA_KV_LORA, -1)
    uv = ukv[..., MLA_NOPE:].reshape(depth, MLA_KV_LORA, -1)
    ukv2 = jnp.concatenate([ukn, uv], axis=2).astype(BF16)
    wbg = _pair_heads(w_br_gqa, 1).astype(BF16)
    return w1, wg, uq, ukv2, wbg


def _rope_tables(seq, n_ctx):
    rows = seq // GRID_W
    row = jnp.repeat(jnp.arange(rows, dtype=jnp.int32), GRID_W).astype(F32)
    col = jnp.tile(jnp.arange(GRID_W, dtype=jnp.int32), rows).astype(F32)

    def table(dim):
        a = dim // 2
        freqs = ROPE_THETA ** (-jnp.arange(0, a, 2, dtype=F32) / a)
        ang_r, ang_c = row[:, None] * freqs, col[:, None] * freqs
        cos = jnp.concatenate([jnp.cos(ang_r), jnp.cos(ang_r), jnp.cos(ang_c), jnp.cos(ang_c)], axis=-1)
        sin = jnp.concatenate([-jnp.sin(ang_r), jnp.sin(ang_r), -jnp.sin(ang_c), jnp.sin(ang_c)], axis=-1)
        return cos, sin

    c64, s64 = table(HEAD_DIM)
    c64, s64 = jnp.tile(c64, (1, 2)), jnp.tile(s64, (1, 2))
    c32, s32 = table(MLA_ROPE)
    pad = ((0, 0), (MLA_NOPE, LANES - MLA_QK))
    c32 = jnp.pad(c32, pad, constant_values=1.0)
    s32 = jnp.pad(s32, pad)
    ctx_pad = ((n_ctx, 0), (0, 0))
    return (jnp.pad(c64, ctx_pad, constant_values=1.0), jnp.pad(s64, ctx_pad),
            jnp.pad(c32, ctx_pad, constant_values=1.0), jnp.pad(s32, ctx_pad))


def _rms(x):
    return x * lax.rsqrt(jnp.mean(x * x, axis=-1, keepdims=True) + EPS)


def _adaln(x, g, shift, scale):
    return (_rms(x) * g) * (1.0 + scale) + shift


def _rope(z, cos, sin, is_x1, shift):
    zb = z.astype(BF16)
    partner = jnp.where(is_x1, pltpu.roll(zb, LANES - shift, 1), pltpu.roll(zb, shift, 1))
    return z * cos + partner.astype(F32) * sin


def _half_rms(z, lo, g):
    sq = z * z
    zero = jnp.zeros_like(sq)
    s_lo = jnp.sum(jnp.where(lo, sq, zero), axis=-1, keepdims=True)
    s_hi = jnp.sum(jnp.where(lo, zero, sq), axis=-1, keepdims=True)
    ms = jnp.where(lo, s_lo, s_hi) * (1.0 / HEAD_DIM)
    return z * lax.rsqrt(ms + EPS) * g


def _dot(a, b):
    return jnp.dot(a, b, preferred_element_type=F32)


def _const_spec(shape):
    return pl.BlockSpec(shape, lambda *_: (0,) * len(shape))


def _layer_spec(stacked, layer):
    shape = stacked.shape[1:]
    return pl.BlockSpec((None,) + shape, lambda *_: (layer,) + (0,) * len(shape))


def _mod_spec(modarr, layer, pos_off, n_ctx_tiles):
    return pl.BlockSpec((None, None, None) + modarr.shape[3:],
                        lambda bi, ti: (layer, bi, (ti + pos_off >= n_ctx_tiles).astype(jnp.int32), 0, 0))


def _token_specs(tokens, tile_off, n_ctx_tiles):
    tm, d = TOKEN_TILE, tokens[0].shape[-1]
    if len(tokens) == 1:
        return [pl.BlockSpec((None, tm, d), lambda bi, ti: (bi, ti + tile_off, 0))]
    assert tile_off == 0
    return [pl.BlockSpec((None, tm, d), lambda bi, ti: (bi, jnp.minimum(ti, n_ctx_tiles - 1), 0)),
            pl.BlockSpec((None, tm, d), lambda bi, ti: (bi, jnp.maximum(ti - n_ctx_tiles, 0), 0))]


def _read_tokens(tok_refs, n_ctx_tiles):
    if len(tok_refs) == 1:
        return tok_refs[0][...]
    ctx_ref, x_ref = tok_refs
    return jnp.where(pl.program_id(1) < n_ctx_tiles, ctx_ref[...], x_ref[...])


def _mod_kernel(c_ref, w_ref, b_ref, o_ref):
    cc = c_ref[...]
    a = (cc * jax.nn.sigmoid(cc)).astype(BF16)
    o_ref[...] = _dot(a, w_ref[...].astype(BF16)) + b_ref[...]


def _mod_call(cc, w_mod, b_mod):
    depth, d, n = w_mod.shape
    tn = 1536
    rows = cc.shape[0]
    return pl.pallas_call(
        _mod_kernel,
        grid=(depth, n // tn),
        in_specs=[
            pl.BlockSpec((rows, d), lambda l, j: (0, 0)),
            pl.BlockSpec((None, d, tn), lambda l, j: (l, 0, j)),
            pl.BlockSpec((None, 1, tn), lambda l, j: (l, 0, j)),
        ],
        out_specs=pl.BlockSpec((None, rows, tn), lambda l, j: (l, 0, j)),
        out_shape=jax.ShapeDtypeStruct((depth, rows, n), F32),
        compiler_params=pltpu.CompilerParams(vmem_limit_bytes=VMEM_LIMIT),
        name="mod_vectors",
    )(cc, w_mod, b_mod.reshape(depth, 1, n))


def _proj_kernel(*refs, n_tok, n_ctx_tiles):
    (mod_ref, g1_ref, w1_ref, wuq_ref, wukv_ref, gq_g_ref, gk_g_ref, gmq_ref, gmkv_ref,
     c64_ref, s64_ref, c32_ref, s32_ref,
     dq_o, dk_o, dv_o, gq_o, gk_o, gv_o, mq_o, mk_o, mv_o) = refs[n_tok:]
    x = _read_tokens(refs[:n_tok], n_ctx_tiles)
    hb = _adaln(x, g1_ref[...], mod_ref[0:1, :], mod_ref[1:2, :]).astype(BF16)
    tm = x.shape[0]
    lane = lax.broadcasted_iota(jnp.int32, (tm, LANES), 1)
    lo = lane < HEAD_DIM
    x1_64 = (lane & ROT64) == 0
    x1_32 = (lane & ROT32) == 0
    c64, s64 = c64_ref[...], s64_ref[...]
    c32, s32 = c32_ref[...], s32_ref[...]
    qk_scale = HEAD_DIM ** -0.5 * LOG2E

    def seg(off, width):
        return _dot(hb, w1_ref[:, off:off + width])

    def blocks(z):
        return [(slice(i * LANES, (i + 1) * LANES), z[:, i * LANES:(i + 1) * LANES]) for i in range(z.shape[1] // LANES)]

    for sl, z in blocks(seg(OFF_DQ, BRANCH_W)):
        dq_o[:, sl] = (_rope(z, c64, s64, x1_64, ROT64) * qk_scale).astype(BF16)
    for sl, z in blocks(seg(OFF_DK, BRANCH_W)):
        dk_o[:, sl] = _rope(z, c64, s64, x1_64, ROT64).astype(BF16)
    dv_o[...] = seg(OFF_DV, BRANCH_W).astype(BF16)
    for sl, z in blocks(seg(OFF_GQ, BRANCH_W)):
        gq = _half_rms(z, lo, gq_g_ref[...])
        gq_o[:, sl] = (_rope(gq, c64, s64, x1_64, ROT64) * qk_scale).astype(BF16)
    gkv = seg(OFF_GK, 2 * LANES)
    gk = _half_rms(gkv[:, :LANES], lo, gk_g_ref[...])
    gk_o[...] = _rope(gk, c64, s64, x1_64, ROT64).astype(BF16)
    gv_o[...] = gkv[:, LANES:].astype(BF16)

    zm = seg(OFF_MCQ, W1_WIDTH - OFF_MCQ)
    mcq = (_rms(zm[:, :MLA_Q_LORA]) * gmq_ref[...]).astype(BF16)
    mckv = (_rms(zm[:, MLA_Q_LORA:MLA_Q_LORA + MLA_KV_LORA]) * gmkv_ref[...]).astype(BF16)
    mkr = _rope(zm[:, MLA_Q_LORA + MLA_KV_LORA:], c32, s32, x1_32, ROT32)
    mla_scale = MLA_QK ** -0.5 * LOG2E
    for sl, z in blocks(_dot(mcq, wuq_ref[...])):
        mq_o[:, sl] = (_rope(z, c32, s32, x1_32, ROT32) * mla_scale).astype(BF16)
    mkv = _dot(mckv, wukv_ref[...])
    for sl, z in blocks(mkv[:, :MLA_HEADS * LANES]):
        mk_o[:, sl] = (z + mkr).astype(BF16)
    mv_o[...] = mkv[:, MLA_HEADS * LANES:].astype(BF16)


def _proj_call(layer, tokens, modarr, g1, w1, wuq, wukv, gq_g, gk_g, gmq, gmkv, tables, n_ctx_tiles):
    b = tokens[0].shape[0]
    t = sum(a.shape[1] for a in tokens)
    tm = TOKEN_TILE
    tok = lambda w: pl.BlockSpec((None, tm, w), lambda bi, ti: (bi, ti, 0))
    tab = pl.BlockSpec((tm, LANES), lambda bi, ti: (ti, 0))
    widths = (BRANCH_W, BRANCH_W, BRANCH_W, BRANCH_W, LANES, LANES, MLA_HEADS * LANES, MLA_HEADS * LANES, BRANCH_W)
    return pl.pallas_call(
        functools.partial(_proj_kernel, n_tok=len(tokens), n_ctx_tiles=n_ctx_tiles),
        grid=(b, t // tm),
        in_specs=_token_specs(tokens, 0, n_ctx_tiles) + [_mod_spec(modarr, layer, 0, n_ctx_tiles)] + [
            _layer_spec(a, layer) for a in (g1, w1, wuq, wukv, gq_g, gk_g, gmq, gmkv)] + [tab, tab, tab, tab],
        out_specs=[tok(w) for w in widths],
        out_shape=[jax.ShapeDtypeStruct((b, t, w), BF16) for w in widths],
        compiler_params=pltpu.CompilerParams(
            dimension_semantics=("parallel", "arbitrary"), vmem_limit_bytes=VMEM_LIMIT),
        name="input_projection",
    )(*tokens, modarr, g1, w1, wuq, wukv, gq_g, gk_g, gmq, gmkv, *tables)


def _attn_kernel(*refs, kind, n_ctx, lam_init, first_tile_is_ctx, upb):
    if kind == "diff":
        q_ref, k_ref, v_ref, lam_ref, gdo_ref, o_ref, vt_ref = refs
    else:
        q_ref, k_ref, v_ref, o_ref, vt_ref = refs
    tq = q_ref.shape[0]
    lane = lax.broadcasted_iota(jnp.int32, (tq, LANES), 1)
    lo = lane < HEAD_DIM
    n_keys = k_ref.shape[0]
    n_kv = vt_ref.shape[0]
    qw = q_ref.shape[1] // upb
    kw = k_ref.shape[1] // n_kv

    @pl.when(pl.program_id(2) == 0)
    def _():
        for i in range(n_kv):
            vt_ref[i, 0:LANES, :] = v_ref[:, i * LANES:(i + 1) * LANES].astype(F32).T.astype(BF16)
            vt_ref[i, LANES:, :] = jnp.ones((SUM_ROWS, n_keys), BF16)

    def key_chunks(nk):
        step = KEY_CHUNK if nk % KEY_CHUNK == 0 else nk
        return [(c, c + step) for c in range(0, nk, step)]

    def scores(u, nk):
        q = q_ref[:, u * qw:(u + 1) * qw]
        kv = u % n_kv
        if kind == "mla":
            qa, qb = q[:, :LANES], q[:, LANES:]
        else:
            zero = jnp.zeros_like(q)
            qa, qb = jnp.where(lo, q, zero), jnp.where(lo, zero, q)
        nt = (((1,), (1,)), ((), ()))
        out = ([], [])
        for c0, c1 in key_chunks(nk):
            k = k_ref[c0:c1, kv * kw:(kv + 1) * kw]
            ka, kb = (k[:, :LANES], k[:, LANES:]) if kind == "mla" else (k, k)
            out[0].append(lax.dot_general(ka, qa, nt, preferred_element_type=F32))
            out[1].append(lax.dot_general(kb, qb, nt, preferred_element_type=F32))
        return out

    def softmax_pv(u, nk, st_chunks):
        maxes, accs = [], []
        for (c0, c1), st in zip(key_chunks(nk), st_chunks):
            m = jnp.max(st, axis=0, keepdims=True)
            p = jnp.exp2((st - m).astype(BF16))
            accs.append(_dot(vt_ref[u % n_kv, :, c0:c1], p))
            maxes.append(m)
        if len(accs) == 1:
            ot = accs[0]
        else:
            m_all = functools.reduce(jnp.maximum, maxes)
            ot = sum(acc * jnp.exp2(m - m_all) for acc, m in zip(accs, maxes))
        return ot[0:LANES] / ot[LANES:LANES + 1]

    def finish(u, nk, sa, sb):
        oa, ob = softmax_pv(u, nk, sa), softmax_pv(u, nk, sb)
        if kind == "diff":
            o = (oa - lam_ref[0:1, 0:1] * ob).T
            o = _rms(o) * gdo_ref[...] * (1.0 - lam_init)
        else:
            row = lax.broadcasted_iota(jnp.int32, oa.shape, 0)
            o = jnp.where(row < HEAD_DIM, oa, ob).T
        o_ref[:, u * LANES:(u + 1) * LANES] = o.astype(BF16)

    def compute(nk):
        pending = scores(0, nk)
        for u in range(upb):
            nxt = scores(u + 1, nk) if u + 1 < upb else None
            finish(u, nk, *pending)
            pending = nxt

    if first_tile_is_ctx:
        j = pl.program_id(2)
        pl.when(j == 0)(lambda: compute(n_ctx))
        pl.when(j > 0)(lambda: compute(n_keys))
    else:
        compute(n_keys)


def _attn_call(kind, layer, q, k, v, extra, *, n_ctx, lam_init, include_ctx):
    b, t, _ = k.shape
    tq = TOKEN_TILE
    upb = UNITS_PER_STEP
    n_ctx_tiles = n_ctx // tq
    q_off = 0 if include_ctx else n_ctx_tiles
    nq = t // tq - q_off
    qw = q.shape[-1] // N_UNITS
    shared_kv = kind == "gqa"
    n_kv = 1 if shared_kv else upb
    kw = LANES if shared_kv else k.shape[-1] // N_UNITS
    kv_map = (lambda bi, g, j: (bi, 0, 0)) if shared_kv else (lambda bi, g, j: (bi, 0, g))
    in_specs = [
        pl.BlockSpec((None, tq, upb * qw), lambda bi, g, j: (bi, j + q_off, g)),
        pl.BlockSpec((None, t, n_kv * kw), kv_map),
        pl.BlockSpec((None, t, n_kv * LANES), kv_map),
    ]
    in_specs += [_layer_spec(a, layer) for a in extra]
    return pl.pallas_call(
        functools.partial(_attn_kernel, kind=kind, n_ctx=n_ctx, lam_init=lam_init,
                          first_tile_is_ctx=include_ctx, upb=upb),
        grid=(b, N_UNITS // upb, nq),
        in_specs=in_specs,
        out_specs=pl.BlockSpec((None, tq, upb * LANES), lambda bi, g, j: (bi, j, g)),
        out_shape=jax.ShapeDtypeStruct((b, nq * tq, BRANCH_W), BF16),
        scratch_shapes=[pltpu.VMEM((n_kv, LANES + SUM_ROWS, t), BF16)],
        compiler_params=pltpu.CompilerParams(
            dimension_semantics=("parallel", "arbitrary", "arbitrary"), vmem_limit_bytes=VMEM_LIMIT),
        name="attention_" + kind,
    )(q, k, v, *extra)


def _merge_kernel(*refs, n_tok, n_ctx_tiles):
    (mod_ref, g1_ref, wg_ref, bg_ref, od_ref, og_ref, om_ref,
     wbd_ref, wbg_ref, wbm_ref, wout_ref, xo_ref) = refs[n_tok:]
    x = _read_tokens(refs[:n_tok], n_ctx_tiles)
    d = x.shape[-1]
    hb = _adaln(x, g1_ref[...], mod_ref[0:1, :], mod_ref[1:2, :]).astype(BF16)
    y = None
    for i, (o_ref, wb_ref) in enumerate(((od_ref, wbd_ref), (og_ref, wbg_ref), (om_ref, wbm_ref))):
        sl = slice(i * d, (i + 1) * d)
        gate = jax.nn.sigmoid(_dot(hb, wg_ref[:, sl]) + bg_ref[:, sl])
        term = gate * _dot(o_ref[...], wb_ref[...])
        y = term if y is None else y + term
    out = _dot(y.astype(BF16), wout_ref[...])
    xo_ref[...] = x + mod_ref[2:3, :] * out


def _merge_call(layer, tokens, modarr, g1, wg, bg, od, og, om, wbd, wbg, wbm, wout, *, x_off, n_ctx_tiles):
    b, d = tokens[0].shape[0], tokens[0].shape[-1]
    tm = TOKEN_TILE
    nt = od.shape[1] // tm
    tok = lambda w: pl.BlockSpec((None, tm, w), lambda bi, ti: (bi, ti, 0))
    return pl.pallas_call(
        functools.partial(_merge_kernel, n_tok=len(tokens), n_ctx_tiles=n_ctx_tiles),
        grid=(b, nt),
        in_specs=_token_specs(tokens, x_off, n_ctx_tiles) + [_mod_spec(modarr, layer, x_off, n_ctx_tiles)] + [
            _layer_spec(a, layer) for a in (g1, wg, bg)] + [tok(BRANCH_W)] * 3 + [
            _layer_spec(a, layer) for a in (wbd, wbg, wbm, wout)],
        out_specs=tok(d),
        out_shape=jax.ShapeDtypeStruct((b, nt * tm, d), F32),
        compiler_params=pltpu.CompilerParams(
            dimension_semantics=("parallel", "arbitrary"), vmem_limit_bytes=VMEM_LIMIT),
        name="branch_merge",
    )(*tokens, modarr, g1, wg, bg, od, og, om, wbd, wbg, wbm, wout)


def _ffn_kernel(*refs, final):
    if final:
        x_ref, mod_ref, g2_ref, win_ref, wout_ref, gf_ref, o_ref = refs
    else:
        x_ref, mod_ref, g2_ref, win_ref, wout_ref, o_ref = refs
    x = x_ref[...]
    hidden = wout_ref.shape[0]
    hb = _adaln(x, g2_ref[...], mod_ref[3:4, :], mod_ref[4:5, :]).astype(BF16)
    g = _dot(hb, win_ref[:, :hidden])
    u = _dot(hb, win_ref[:, hidden:])
    a = (g * jax.nn.sigmoid(g) * u).astype(BF16)
    xn = x + mod_ref[5:6, :] * _dot(a, wout_ref[...])
    if final:
        xn = _rms(xn) * gf_ref[...]
    o_ref[...] = xn


def _ffn_call(layer, xin, modarr, g2, win, wout, gf, *, pos_off, n_ctx_tiles):
    b, t, d = xin.shape
    tm = TOKEN_TILE
    tok = pl.BlockSpec((None, tm, d), lambda bi, ti: (bi, ti, 0))
    final = gf is not None
    in_specs = [tok, _mod_spec(modarr, layer, pos_off, n_ctx_tiles)] + [
        _layer_spec(a, layer) for a in (g2, win, wout)]
    args = [xin, modarr, g2, win, wout]
    if final:
        in_specs.append(_const_spec((1, d)))
        args.append(gf)
    return pl.pallas_call(
        functools.partial(_ffn_kernel, final=final),
        grid=(b, t // tm),
        in_specs=in_specs,
        out_specs=tok,
        out_shape=jax.ShapeDtypeStruct((b, t, d), F32),
        compiler_params=pltpu.CompilerParams(
            dimension_semantics=("parallel", "arbitrary"), vmem_limit_bytes=VMEM_LIMIT),
        name="swiglu",
    )(*args)


def kernel(x, c, ctx, c_ctx, w_mod, b_mod, g_norm1, w_in, b_gate, lam_q1, lam_k1, lam_q2, lam_k2, g_diff_out, g_gqa_q, g_gqa_k, g_mla_q, w_mla_uq, g_mla_kv, w_mla_ukv, w_br_diff, w_br_gqa, w_br_mla, w_out, g_norm2, w_ffn_in, w_ffn_out, g_final):
    b, seq, d = x.shape
    n_ctx = ctx.shape[1]
    depth = w_in.shape[0]
    assert n_ctx % TOKEN_TILE == 0 and seq % TOKEN_TILE == 0 and seq % GRID_W == 0
    n_ctx_tiles = n_ctx // TOKEN_TILE

    mod_rows = 32
    cc = jnp.zeros((mod_rows, d), F32).at[:b].set(c).at[b].set(c_ctx)
    mods = _mod_call(cc, w_mod, b_mod)
    mod_lat = mods[:, :b].reshape(depth, b, N_MOD, d)
    mod_ctx = jnp.broadcast_to(mods[:, b].reshape(depth, 1, N_MOD, d), (depth, b, N_MOD, d))
    modarr = jnp.stack([mod_ctx, mod_lat], axis=2)
    tables = _rope_tables(seq, n_ctx)

    lam_inits = [_lambda_init(l) for l in range(depth)]
    lam = (jnp.exp(jnp.sum(lam_q1 * lam_k1, axis=-1)) - jnp.exp(jnp.sum(lam_q2 * lam_k2, axis=-1))
           + jnp.asarray(lam_inits, F32))
    lam_vec = jnp.broadcast_to(lam.reshape(depth, 1, 1), (depth, 1, LANES))

    def row(a):
        return a.reshape(depth, 1, -1)

    w1, wg, wuq, wukv, wbg = _prep_weights(w_in, w_mla_uq, w_mla_ukv, w_br_gqa)
    wbd, wbm, wout = w_br_diff.astype(BF16), w_br_mla.astype(BF16), w_out.astype(BF16)
    wfi, wfo = w_ffn_in.astype(BF16), w_ffn_out.astype(BF16)
    g1, g2, bg = row(g_norm1), row(g_norm2), row(b_gate)
    gq_g, gk_g = row(jnp.tile(g_gqa_q, (1, 2))), row(jnp.tile(g_gqa_k, (1, 2)))
    gmq, gmkv, gdo = row(g_mla_q), row(g_mla_kv), row(g_diff_out)

    tokens = (ctx, x)
    for l in range(depth):
        last = l == depth - 1
        dq, dk, dv, gq, gk, gv, mq, mk, mv = _proj_call(
            l, tokens, modarr, g1, w1, wuq, wukv, gq_g, gk_g, gmq, gmkv, tables, n_ctx_tiles)

        attn = functools.partial(_attn_call, n_ctx=n_ctx, lam_init=lam_inits[l], include_ctx=not last)
        od = attn("diff", l, dq, dk, dv, (lam_vec, gdo))
        og = attn("gqa", l, gq, gk, gv, ())
        om = attn("mla", l, mq, mk, mv, ())

        x_off = n_ctx_tiles if last else 0
        xmid = _merge_call(l, tokens, modarr, g1, wg, bg, od, og, om, wbd, wbg, wbm, wout,
                           x_off=x_off, n_ctx_tiles=n_ctx_tiles)
        xout = _ffn_call(l, xmid, modarr, g2, wfi, wfo, g_final.reshape(1, d) if last else None,
                         pos_off=x_off, n_ctx_tiles=n_ctx_tiles)
        tokens = (xout,)
    return xout
```

```python
import functools
import math

import jax
import jax.numpy as jnp
from jax import lax
from jax.experimental import pallas as pl
from jax.experimental.pallas import tpu as pltpu

F32 = jnp.float32
BF16 = jnp.bfloat16

GRID_W = 64
ROPE_THETA = 10000.0
EPS = 1e-6
N_MOD = 6

DIFF_HEADS = 4
HEAD_DIM = 64
GQA_KV_HEADS = 2
GQA_GROUP = 4
MLA_HEADS = 8
MLA_Q_LORA = 384
MLA_KV_LORA = 256
MLA_NOPE = 64
MLA_ROPE = 32
MLA_QK = MLA_NOPE + MLA_ROPE
MLA_V = 64
ROT64 = HEAD_DIM // 4
ROT32 = MLA_ROPE // 4
BRANCH_W = 512
N_UNITS = 4

LANES = 128
UNITS_PER_STEP = 4
KEY_CHUNK = 256
SUM_ROWS = 16
LOG2E = math.log2(math.e)
TOKEN_TILE = 256
VMEM_LIMIT = 56 * 1024 * 1024

OFF_DQ, OFF_DK, OFF_DV, OFF_GQ, OFF_GK, OFF_GV = 0, 512, 1024, 1536, 2048, 2176
OFF_MCQ, OFF_MCKV, OFF_MKR, W1_WIDTH = 2304, 2688, 2944, 3072


def _lambda_init(layer):
    return 0.8 - 0.6 * math.exp(-0.3 * layer)


_IN_GQ, _IN_GK, _IN_MKR, _IN_GATE = 1536, 2048, 2944, 2976


def _pair_heads(a, axis):
    shape = a.shape
    a = a.reshape(shape[:axis] + (GQA_KV_HEADS, GQA_GROUP, HEAD_DIM) + shape[axis + 1:])
    return jnp.swapaxes(a, axis, axis + 1).reshape(shape)


def _prep_weights(w_in, w_mla_uq, w_mla_ukv, w_br_gqa):
    depth = w_in.shape[0]
    mkr = jnp.pad(w_in[:, :, _IN_MKR:_IN_GATE], ((0, 0), (0, 0), (MLA_NOPE, LANES - MLA_QK)))
    w1 = jnp.concatenate([w_in[:, :, :_IN_GQ], _pair_heads(w_in[:, :, _IN_GQ:_IN_GK], 2),
                          w_in[:, :, _IN_GK:_IN_MKR], mkr], axis=2).astype(BF16)
    wg = w_in[:, :, _IN_GATE:].astype(BF16)
    uq = w_mla_uq.reshape(depth, MLA_Q_LORA, MLA_HEADS, MLA_QK)
    uq = jnp.pad(uq, ((0, 0), (0, 0), (0, 0), (0, LANES - MLA_QK)))
    uq = uq.reshape(depth, MLA_Q_LORA, MLA_HEADS * LANES).astype(BF16)
    ukv = w_mla_ukv.reshape(depth, MLA_KV_LORA, MLA_HEADS, MLA_NOPE + MLA_V)
    ukn = jnp.pad(ukv[..., :MLA_NOPE], ((0, 0), (0, 0), (0, 0), (0, LANES - MLA_NOPE)))
    ukn = ukn.reshape(depth, MLA_KV_LORA, -1)
    uv = ukv[..., MLA_NOPE:].reshape(depth, MLA_KV_LORA, -1)
    ukv2 = jnp.concatenate([ukn, uv], axis=2).astype(BF16)
    wbg = _pair_heads(w_br_gqa, 1).astype(BF16)
    return w1, wg, uq, ukv2, wbg


def _rope_tables(seq, n_ctx):
    rows = seq // GRID_W
    row = jnp.repeat(jnp.arange(rows, dtype=jnp.int32), GRID_W).astype(F32)
    col = jnp.tile(jnp.arange(GRID_W, dtype=jnp.int32), rows).astype(F32)

    def table(dim):
        a = dim // 2
        freqs = ROPE_THETA ** (-jnp.arange(0, a, 2, dtype=F32) / a)
        ang_r, ang_c = row[:, None] * freqs, col[:, None] * freqs
        cos = jnp.concatenate([jnp.cos(ang_r), jnp.cos(ang_r), jnp.cos(ang_c), jnp.cos(ang_c)], axis=-1)
        sin = jnp.concatenate([-jnp.sin(ang_r), jnp.sin(ang_r), -jnp.sin(ang_c), jnp.sin(ang_c)], axis=-1)
        return cos, sin

    c64, s64 = table(HEAD_DIM)
    c64, s64 = jnp.tile(c64, (1, 2)), jnp.tile(s64, (1, 2))
    c32, s32 = table(MLA_ROPE)
    pad = ((0, 0), (MLA_NOPE, LANES - MLA_QK))
    c32 = jnp.pad(c32, pad, constant_values=1.0)
    s32 = jnp.pad(s32, pad)
    ctx_pad = ((n_ctx, 0), (0, 0))
    return (jnp.pad(c64, ctx_pad, constant_values=1.0), jnp.pad(s64, ctx_pad),
            jnp.pad(c32, ctx_pad, constant_values=1.0), jnp.pad(s32, ctx_pad))


def _rms(x):
    return x * lax.rsqrt(jnp.mean(x * x, axis=-1, keepdims=True) + EPS)


def _adaln(x, g, shift, scale):
    return (_rms(x) * g) * (1.0 + scale) + shift


def _rope(z, cos, sin, is_x1, shift):
    zb = z.astype(BF16)
    partner = jnp.where(is_x1, pltpu.roll(zb, LANES - shift, 1), pltpu.roll(zb, shift, 1))
    return z * cos + partner.astype(F32) * sin


def _half_rms(z, lo, g):
    sq = z * z
    zero = jnp.zeros_like(sq)
    s_lo = jnp.sum(jnp.where(lo, sq, zero), axis=-1, keepdims=True)
    s_hi = jnp.sum(jnp.where(lo, zero, sq), axis=-1, keepdims=True)
    ms = jnp.where(lo, s_lo, s_hi) * (1.0 / HEAD_DIM)
    return z * lax.rsqrt(ms + EPS) * g


def _dot(a, b):
    return jnp.dot(a, b, preferred_element_type=F32)


def _const_spec(shape):
    return pl.BlockSpec(shape, lambda *_: (0,) * len(shape))


def _layer_spec(stacked, layer):
    shape = stacked.shape[1:]
    return pl.BlockSpec((None,) + shape, lambda *_: (layer,) + (0,) * len(shape))


def _mod_spec(modarr, layer, pos_off, n_ctx_tiles):
    return pl.BlockSpec((None, None, None) + modarr.shape[3:],
                        lambda bi, ti: (layer, bi, (ti + pos_off >= n_ctx_tiles).astype(jnp.int32), 0, 0))


def _token_specs(tokens, tile_off, n_ctx_tiles):
    tm, d = TOKEN_TILE, tokens[0].shape[-1]
    if len(tokens) == 1:
        return [pl.BlockSpec((None, tm, d), lambda bi, ti: (bi, ti + tile_off, 0))]
    assert tile_off == 0
    return [pl.BlockSpec((None, tm, d), lambda bi, ti: (bi, jnp.minimum(ti, n_ctx_tiles - 1), 0)),
            pl.BlockSpec((None, tm, d), lambda bi, ti: (bi, jnp.maximum(ti - n_ctx_tiles, 0), 0))]


def _read_tokens(tok_refs, n_ctx_tiles):
    if len(tok_refs) == 1:
        return tok_refs[0][...]
    ctx_ref, x_ref = tok_refs
    return jnp.where(pl.program_id(1) < n_ctx_tiles, ctx_ref[...], x_ref[...])


def _mod_kernel(c_ref, w_ref, b_ref, o_ref):
    cc = c_ref[...]
    a = (cc * jax.nn.sigmoid(cc)).astype(BF16)
    o_ref[...] = _dot(a, w_ref[...].astype(BF16)) + b_ref[...]


def _mod_call(cc, w_mod, b_mod):
    depth, d, n = w_mod.shape
    tn = 1536
    rows = cc.shape[0]
    return pl.pallas_call(
        _mod_kernel,
        grid=(depth, n // tn),
        in_specs=[
            pl.BlockSpec((rows, d), lambda l, j: (0, 0)),
            pl.BlockSpec((None, d, tn), lambda l, j: (l, 0, j)),
            pl.BlockSpec((None, 1, tn), lambda l, j: (l, 0, j)),
        ],
        out_specs=pl.BlockSpec((None, rows, tn), lambda l, j: (l, 0, j)),
        out_shape=jax.ShapeDtypeStruct((depth, rows, n), F32),
        compiler_params=pltpu.CompilerParams(vmem_limit_bytes=VMEM_LIMIT),
        name="mod_vectors",
    )(cc, w_mod, b_mod.reshape(depth, 1, n))


def _proj_kernel(*refs, n_tok, n_ctx_tiles):
    (mod_ref, g1_ref, w1_ref, wuq_ref, wukv_ref, gq_g_ref, gk_g_ref, gmq_ref, gmkv_ref,
     c64_ref, s64_ref, c32_ref, s32_ref,
     dq_o, dk_o, dv_o, gq_o, gk_o, gv_o, mq_o, mk_o, mv_o) = refs[n_tok:]
    x = _read_tokens(refs[:n_tok], n_ctx_tiles)
    hb = _adaln(x, g1_ref[...], mod_ref[0:1, :], mod_ref[1:2, :]).astype(BF16)
    tm = x.shape[0]
    lane = lax.broadcasted_iota(jnp.int32, (tm, LANES), 1)
    lo = lane < HEAD_DIM
    x1_64 = (lane & ROT64) == 0
    x1_32 = (lane & ROT32) == 0
    c64, s64 = c64_ref[...], s64_ref[...]
    c32, s32 = c32_ref[...], s32_ref[...]
    qk_scale = HEAD_DIM ** -0.5 * LOG2E

    def seg(off, width):
        return _dot(hb, w1_ref[:, off:off + width])

    def blocks(z):
        return [(slice(i * LANES, (i + 1) * LANES), z[:, i * LANES:(i + 1) * LANES]) for i in range(z.shape[1] // LANES)]

    for sl, z in blocks(seg(OFF_DQ, BRANCH_W)):
        dq_o[:, sl] = (_rope(z, c64, s64, x1_64, ROT64) * qk_scale).astype(BF16)
    for sl, z in blocks(seg(OFF_DK, BRANCH_W)):
        dk_o[:, sl] = _rope(z, c64, s64, x1_64, ROT64).astype(BF16)
    dv_o[...] = seg(OFF_DV, BRANCH_W).astype(BF16)
    for sl, z in blocks(seg(OFF_GQ, BRANCH_W)):
        gq = _half_rms(z, lo, gq_g_ref[...])
        gq_o[:, sl] = (_rope(gq, c64, s64, x1_64, ROT64) * qk_scale).astype(BF16)
    gkv = seg(OFF_GK, 2 * LANES)
    gk = _half_rms(gkv[:, :LANES], lo, gk_g_ref[...])
    gk_o[...] = _rope(gk, c64, s64, x1_64, ROT64).astype(BF16)
    gv_o[...] = gkv[:, LANES:].astype(BF16)

    zm = seg(OFF_MCQ, W1_WIDTH - OFF_MCQ)
    mcq = (_rms(zm[:, :MLA_Q_LORA]) * gmq_ref[...]).astype(BF16)
    mckv = (_rms(zm[:, MLA_Q_LORA:MLA_Q_LORA + MLA_KV_LORA]) * gmkv_ref[...]).astype(BF16)
    mkr = _rope(zm[:, MLA_Q_LORA + MLA_KV_LORA:], c32, s32, x1_32, ROT32)
    mla_scale = MLA_QK ** -0.5 * LOG2E
    for sl, z in blocks(_dot(mcq, wuq_ref[...])):
        mq_o[:, sl] = (_rope(z, c32, s32, x1_32, ROT32) * mla_scale).astype(BF16)
    mkv = _dot(mckv, wukv_ref[...])
    for sl, z in blocks(mkv[:, :MLA_HEADS * LANES]):
        mk_o[:, sl] = (z + mkr).astype(BF16)
    mv_o[...] = mkv[:, MLA_HEADS * LANES:].astype(BF16)


def _proj_call(layer, tokens, modarr, g1, w1, wuq, wukv, gq_g, gk_g, gmq, gmkv, tables, n_ctx_tiles):
    b = tokens[0].shape[0]
    t = sum(a.shape[1] for a in tokens)
    tm = TOKEN_TILE
    tok = lambda w: pl.BlockSpec((None, tm, w), lambda bi, ti: (bi, ti, 0))
    tab = pl.BlockSpec((tm, LANES), lambda bi, ti: (ti, 0))
    widths = (BRANCH_W, BRANCH_W, BRANCH_W, BRANCH_W, LANES, LANES, MLA_HEADS * LANES, MLA_HEADS * LANES, BRANCH_W)
    return pl.pallas_call(
        functools.partial(_proj_kernel, n_tok=len(tokens), n_ctx_tiles=n_ctx_tiles),
        grid=(b, t // tm),
        in_specs=_token_specs(tokens, 0, n_ctx_tiles) + [_mod_spec(modarr, layer, 0, n_ctx_tiles)] + [
            _layer_spec(a, layer) for a in (g1, w1, wuq, wukv, gq_g, gk_g, gmq, gmkv)] + [tab, tab, tab, tab],
        out_specs=[tok(w) for w in widths],
        out_shape=[jax.ShapeDtypeStruct((b, t, w), BF16) for w in widths],
        compiler_params=pltpu.CompilerParams(
            dimension_semantics=("parallel", "arbitrary"), vmem_limit_bytes=VMEM_LIMIT),
        name="input_projection",
    )(*tokens, modarr, g1, w1, wuq, wukv, gq_g, gk_g, gmq, gmkv, *tables)


def _attn_kernel(*refs, kind, n_ctx, lam_init, first_tile_is_ctx, upb):
    if kind == "diff":
        q_ref, k_ref, v_ref, lam_ref, gdo_ref, o_ref, vt_ref = refs
    else:
        q_ref, k_ref, v_ref, o_ref, vt_ref = refs
    tq = q_ref.shape[0]
    lane = lax.broadcasted_iota(jnp.int32, (tq, LANES), 1)
    lo = lane < HEAD_DIM
    n_keys = k_ref.shape[0]
    n_kv = vt_ref.shape[0]
    qw = q_ref.shape[1] // upb
    kw = k_ref.shape[1] // n_kv

    @pl.when(pl.program_id(2) == 0)
    def _():
        for i in range(n_kv):
            vt_ref[i, 0:LANES, :] = v_ref[:, i * LANES:(i + 1) * LANES].astype(F32).T.astype(BF16)
            vt_ref[i, LANES:, :] = jnp.ones((SUM_ROWS, n_keys), BF16)

    def key_chunks(nk):
        step = KEY_CHUNK if nk % KEY_CHUNK == 0 else nk
        return [(c, c + step) for c in range(0, nk, step)]

    def unit_queries(u):
        q = q_ref[:, u * qw:(u + 1) * qw]
        if kind == "mla":
            return q[:, :LANES], q[:, LANES:]
        zero = jnp.zeros_like(q)
        return jnp.where(lo, q, zero), jnp.where(lo, zero, q)

    def chunk_scores(u, qa, qb, c0, c1):
        kv = u % n_kv
        k = k_ref[c0:c1, kv * kw:(kv + 1) * kw]
        ka, kb = (k[:, :LANES], k[:, LANES:]) if kind == "mla" else (k, k)
        nt = (((1,), (1,)), ((), ()))
        return (lax.dot_general(ka, qa, nt, preferred_element_type=F32),
                lax.dot_general(kb, qb, nt, preferred_element_type=F32))

    def chunk_softmax_pv(u, c0, c1, st):
        m = jnp.max(st, axis=0, keepdims=True)
        p = jnp.exp2((st - m).astype(BF16))
        return m, _dot(vt_ref[u % n_kv, :, c0:c1], p)

    def merge_chunks(parts):
        if len(parts) == 1:
            ot = parts[0][1]
        else:
            m_all = functools.reduce(jnp.maximum, [m for m, _ in parts])
            ot = sum(acc * jnp.exp2(m - m_all) for m, acc in parts)
        return ot[0:LANES] / ot[LANES:LANES + 1]

    def write_unit(u, oa, ob):
        if kind == "diff":
            o = (oa - lam_ref[0:1, 0:1] * ob).T
            o = _rms(o) * gdo_ref[...] * (1.0 - lam_init)
        else:
            row = lax.broadcasted_iota(jnp.int32, oa.shape, 0)
            o = jnp.where(row < HEAD_DIM, oa, ob).T
        o_ref[:, u * LANES:(u + 1) * LANES] = o.astype(BF16)

    def compute(nk):
        chunks = key_chunks(nk)
        for u in range(upb):
            sa, sb = chunk_scores(u, *unit_queries(u), 0, nk)
            oa = merge_chunks([chunk_softmax_pv(u, c0, c1, sa[c0:c1]) for c0, c1 in chunks])
            ob = merge_chunks([chunk_softmax_pv(u, c0, c1, sb[c0:c1]) for c0, c1 in chunks])
            write_unit(u, oa, ob)

    if first_tile_is_ctx:
        j = pl.program_id(2)
        pl.when(j == 0)(lambda: compute(n_ctx))
        pl.when(j > 0)(lambda: compute(n_keys))
    else:
        compute(n_keys)


def _attn_call(kind, layer, q, k, v, extra, *, n_ctx, lam_init, include_ctx):
    b, t, _ = k.shape
    tq = TOKEN_TILE
    upb = UNITS_PER_STEP
    n_ctx_tiles = n_ctx // tq
    q_off = 0 if include_ctx else n_ctx_tiles
    nq = t // tq - q_off
    qw = q.shape[-1] // N_UNITS
    shared_kv = kind == "gqa"
    n_kv = 1 if shared_kv else upb
    kw = LANES if shared_kv else k.shape[-1] // N_UNITS
    kv_map = (lambda bi, g, j: (bi, 0, 0)) if shared_kv else (lambda bi, g, j: (bi, 0, g))
    in_specs = [
        pl.BlockSpec((None, tq, upb * qw), lambda bi, g, j: (bi, j + q_off, g)),
        pl.BlockSpec((None, t, n_kv * kw), kv_map),
        pl.BlockSpec((None, t, n_kv * LANES), kv_map),
    ]
    in_specs += [_layer_spec(a, layer) for a in extra]
    return pl.pallas_call(
        functools.partial(_attn_kernel, kind=kind, n_ctx=n_ctx, lam_init=lam_init,
                          first_tile_is_ctx=include_ctx, upb=upb),
        grid=(b, N_UNITS // upb, nq),
        in_specs=in_specs,
        out_specs=pl.BlockSpec((None, tq, upb * LANES), lambda bi, g, j: (bi, j, g)),
        out_shape=jax.ShapeDtypeStruct((b, nq * tq, BRANCH_W), BF16),
        scratch_shapes=[pltpu.VMEM((n_kv, LANES + SUM_ROWS, t), BF16)],
        compiler_params=pltpu.CompilerParams(
            dimension_semantics=("parallel", "arbitrary", "arbitrary"), vmem_limit_bytes=VMEM_LIMIT),
        name="attention_" + kind,
    )(q, k, v, *extra)


def _merge_kernel(*refs, n_tok, n_ctx_tiles):
    (mod_ref, g1_ref, wg_ref, bg_ref, od_ref, og_ref, om_ref,
     wbd_ref, wbg_ref, wbm_ref, wout_ref, xo_ref) = refs[n_tok:]
    x = _read_tokens(refs[:n_tok], n_ctx_tiles)
    d = x.shape[-1]
    hb = _adaln(x, g1_ref[...], mod_ref[0:1, :], mod_ref[1:2, :]).astype(BF16)
    y = None
    for i, (o_ref, wb_ref) in enumerate(((od_ref, wbd_ref), (og_ref, wbg_ref), (om_ref, wbm_ref))):
        sl = slice(i * d, (i + 1) * d)
        gate = jax.nn.sigmoid(_dot(hb, wg_ref[:, sl]) + bg_ref[:, sl])
        term = gate * _dot(o_ref[...], wb_ref[...])
        y = term if y is None else y + term
    out = _dot(y.astype(BF16), wout_ref[...])
    xo_ref[...] = x + mod_ref[2:3, :] * out


def _merge_call(layer, tokens, modarr, g1, wg, bg, od, og, om, wbd, wbg, wbm, wout, *, x_off, n_ctx_tiles):
    b, d = tokens[0].shape[0], tokens[0].shape[-1]
    tm = TOKEN_TILE
    nt = od.shape[1] // tm
    tok = lambda w: pl.BlockSpec((None, tm, w), lambda bi, ti: (bi, ti, 0))
    return pl.pallas_call(
        functools.partial(_merge_kernel, n_tok=len(tokens), n_ctx_tiles=n_ctx_tiles),
        grid=(b, nt),
        in_specs=_token_specs(tokens, x_off, n_ctx_tiles) + [_mod_spec(modarr, layer, x_off, n_ctx_tiles)] + [
            _layer_spec(a, layer) for a in (g1, wg, bg)] + [tok(BRANCH_W)] * 3 + [
            _layer_spec(a, layer) for a in (wbd, wbg, wbm, wout)],
        out_specs=tok(d),
        out_shape=jax.ShapeDtypeStruct((b, nt * tm, d), F32),
        compiler_params=pltpu.CompilerParams(
            dimension_semantics=("parallel", "arbitrary"), vmem_limit_bytes=VMEM_LIMIT),
        name="branch_merge",
    )(*tokens, modarr, g1, wg, bg, od, og, om, wbd, wbg, wbm, wout)


def _ffn_kernel(*refs, final):
    if final:
        x_ref, mod_ref, g2_ref, win_ref, wout_ref, gf_ref, o_ref = refs
    else:
        x_ref, mod_ref, g2_ref, win_ref, wout_ref, o_ref = refs
    x = x_ref[...]
    hidden = wout_ref.shape[0]
    hb = _adaln(x, g2_ref[...], mod_ref[3:4, :], mod_ref[4:5, :]).astype(BF16)
    g = _dot(hb, win_ref[:, :hidden])
    u = _dot(hb, win_ref[:, hidden:])
    a = (g * jax.nn.sigmoid(g) * u).astype(BF16)
    xn = x + mod_ref[5:6, :] * _dot(a, wout_ref[...])
    if final:
        xn = _rms(xn) * gf_ref[...]
    o_ref[...] = xn


def _ffn_call(layer, xin, modarr, g2, win, wout, gf, *, pos_off, n_ctx_tiles):
    b, t, d = xin.shape
    tm = TOKEN_TILE
    tok = pl.BlockSpec((None, tm, d), lambda bi, ti: (bi, ti, 0))
    final = gf is not None
    in_specs = [tok, _mod_spec(modarr, layer, pos_off, n_ctx_tiles)] + [
        _layer_spec(a, layer) for a in (g2, win, wout)]
    args = [xin, modarr, g2, win, wout]
    if final:
        in_specs.append(_const_spec((1, d)))
        args.append(gf)
    return pl.pallas_call(
        functools.partial(_ffn_kernel, final=final),
        grid=(b, t // tm),
        in_specs=in_specs,
        out_specs=tok,
        out_shape=jax.ShapeDtypeStruct((b, t, d), F32),
        compiler_params=pltpu.CompilerParams(
            dimension_semantics=("parallel", "arbitrary"), vmem_limit_bytes=VMEM_LIMIT),
        name="swiglu",
    )(*args)


def kernel(x, c, ctx, c_ctx, w_mod, b_mod, g_norm1, w_in, b_gate, lam_q1, lam_k1, lam_q2, lam_k2, g_diff_out, g_gqa_q, g_gqa_k, g_mla_q, w_mla_uq, g_mla_kv, w_mla_ukv, w_br_diff, w_br_gqa, w_br_mla, w_out, g_norm2, w_ffn_in, w_ffn_out, g_final):
    b, seq, d = x.shape
    n_ctx = ctx.shape[1]
    depth = w_in.shape[0]
    assert n_ctx % TOKEN_TILE == 0 and seq % TOKEN_TILE == 0 and seq % GRID_W == 0
    n_ctx_tiles = n_ctx // TOKEN_TILE

    mod_rows = 32
    cc = jnp.zeros((mod_rows, d), F32).at[:b].set(c).at[b].set(c_ctx)
    mods = _mod_call(cc, w_mod, b_mod)
    mod_lat = mods[:, :b].reshape(depth, b, N_MOD, d)
    mod_ctx = jnp.broadcast_to(mods[:, b].reshape(depth, 1, N_MOD, d), (depth, b, N_MOD, d))
    modarr = jnp.stack([mod_ctx, mod_lat], axis=2)
    tables = _rope_tables(seq, n_ctx)

    lam_inits = [_lambda_init(l) for l in range(depth)]
    lam = (jnp.exp(jnp.sum(lam_q1 * lam_k1, axis=-1)) - jnp.exp(jnp.sum(lam_q2 * lam_k2, axis=-1))
           + jnp.asarray(lam_inits, F32))
    lam_vec = jnp.broadcast_to(lam.reshape(depth, 1, 1), (depth, 1, LANES))

    def row(a):
        return a.reshape(depth, 1, -1)

    w1, wg, wuq, wukv, wbg = _prep_weights(w_in, w_mla_uq, w_mla_ukv, w_br_gqa)
    wbd, wbm, wout = w_br_diff.astype(BF16), w_br_mla.astype(BF16), w_out.astype(BF16)
    wfi, wfo = w_ffn_in.astype(BF16), w_ffn_out.astype(BF16)
    g1, g2, bg = row(g_norm1), row(g_norm2), row(b_gate)
    gq_g, gk_g = row(jnp.tile(g_gqa_q, (1, 2))), row(jnp.tile(g_gqa_k, (1, 2)))
    gmq, gmkv, gdo = row(g_mla_q), row(g_mla_kv), row(g_diff_out)

    tokens = (ctx, x)
    for l in range(depth):
        last = l == depth - 1
        dq, dk, dv, gq, gk, gv, mq, mk, mv = _proj_call(
            l, tokens, modarr, g1, w1, wuq, wukv, gq_g, gk_g, gmq, gmkv, tables, n_ctx_tiles)

        attn = functools.partial(_attn_call, n_ctx=n_ctx, lam_init=lam_inits[l], include_ctx=not last)
        od = attn("diff", l, dq, dk, dv, (lam_vec, gdo))
        og = attn("gqa", l, gq, gk, gv, ())
        om = attn("mla", l, mq, mk, mv, ())

        x_off = n_ctx_tiles if last else 0
        xmid = _merge_call(l, tokens, modarr, g1, wg, bg, od, og, om, wbd, wbg, wbm, wout,
                           x_off=x_off, n_ctx_tiles=n_ctx_tiles)
        xout = _ffn_call(l, xmid, modarr, g2, wfi, wfo, g_final.reshape(1, d) if last else None,
                         pos_off=x_off, n_ctx_tiles=n_ctx_tiles)
        tokens = (xout,)
    return xout
```

```python
import functools
import math

import jax
import jax.numpy as jnp
from jax import lax
from jax.experimental import pallas as pl
from jax.experimental.pallas import tpu as pltpu

F32 = jnp.float32
BF16 = jnp.bfloat16

GRID_W = 64
ROPE_THETA = 10000.0
EPS = 1e-6
N_MOD = 6

DIFF_HEADS = 4
HEAD_DIM = 64
GQA_KV_HEADS = 2
GQA_GROUP = 4
MLA_HEADS = 8
MLA_Q_LORA = 384
MLA_KV_LORA = 256
MLA_NOPE = 64
MLA_ROPE = 32
MLA_QK = MLA_NOPE + MLA_ROPE
MLA_V = 64
ROT64 = HEAD_DIM // 4
ROT32 = MLA_ROPE // 4
BRANCH_W = 512
N_UNITS = 4

LANES = 128
LATENT_QUERY_TILE = 512
UNITS_PER_STEP = 4
KEY_CHUNK = 256
SUM_ROWS = 16
LOG2E = math.log2(math.e)
TOKEN_TILE = 256
VMEM_LIMIT = 56 * 1024 * 1024

OFF_DQ, OFF_DK, OFF_DV, OFF_GQ, OFF_GK, OFF_GV = 0, 512, 1024, 1536, 2048, 2176
OFF_MCQ, OFF_MCKV, OFF_MKR, W1_WIDTH = 2304, 2688, 2944, 3072


def _lambda_init(layer):
    return 0.8 - 0.6 * math.exp(-0.3 * layer)


_IN_GQ, _IN_GK, _IN_MKR, _IN_GATE = 1536, 2048, 2944, 2976


def _pair_heads(a, axis):
    shape = a.shape
    a = a.reshape(shape[:axis] + (GQA_KV_HEADS, GQA_GROUP, HEAD_DIM) + shape[axis + 1:])
    return jnp.swapaxes(a, axis, axis + 1).reshape(shape)


def _prep_weights(w_in, w_mla_uq, w_mla_ukv, w_br_gqa):
    depth = w_in.shape[0]
    mkr = jnp.pad(w_in[:, :, _IN_MKR:_IN_GATE], ((0, 0), (0, 0), (MLA_NOPE, LANES - MLA_QK)))
    w1 = jnp.concatenate([w_in[:, :, :_IN_GQ], _pair_heads(w_in[:, :, _IN_GQ:_IN_GK], 2),
                          w_in[:, :, _IN_GK:_IN_MKR], mkr], axis=2).astype(BF16)
    wg = w_in[:, :, _IN_GATE:].astype(BF16)
    uq = w_mla_uq.reshape(depth, MLA_Q_LORA, MLA_HEADS, MLA_QK)
    uq = jnp.pad(uq, ((0, 0), (0, 0), (0, 0), (0, LANES - MLA_QK)))
    uq = uq.reshape(depth, MLA_Q_LORA, MLA_HEADS * LANES).astype(BF16)
    ukv = w_mla_ukv.reshape(depth, MLA_KV_LORA, MLA_HEADS, MLA_NOPE + MLA_V)
    ukn = jnp.pad(ukv[..., :MLA_NOPE], ((0, 0), (0, 0), (0, 0), (0, LANES - MLA_NOPE)))
    ukn = ukn.reshape(depth, MLA_KV_LORA, -1)
    uv = ukv[..., MLA_NOPE:].reshape(depth, MLA_KV_LORA, -1)
    ukv2 = jnp.concatenate([ukn, uv], axis=2).astype(BF16)
    wbg = _pair_heads(w_br_gqa, 1).astype(BF16)
    return w1, wg, uq, ukv2, wbg


def _rope_tables(seq, n_ctx):
    rows = seq // GRID_W
    row = jnp.repeat(jnp.arange(rows, dtype=jnp.int32), GRID_W).astype(F32)
    col = jnp.tile(jnp.arange(GRID_W, dtype=jnp.int32), rows).astype(F32)

    def table(dim):
        a = dim // 2
        freqs = ROPE_THETA ** (-jnp.arange(0, a, 2, dtype=F32) / a)
        ang_r, ang_c = row[:, None] * freqs, col[:, None] * freqs
        cos = jnp.concatenate([jnp.cos(ang_r), jnp.cos(ang_r), jnp.cos(ang_c), jnp.cos(ang_c)], axis=-1)
        sin = jnp.concatenate([-jnp.sin(ang_r), jnp.sin(ang_r), -jnp.sin(ang_c), jnp.sin(ang_c)], axis=-1)
        return cos, sin

    c64, s64 = table(HEAD_DIM)
    c64, s64 = jnp.tile(c64, (1, 2)), jnp.tile(s64, (1, 2))
    c32, s32 = table(MLA_ROPE)
    pad = ((0, 0), (MLA_NOPE, LANES - MLA_QK))
    c32 = jnp.pad(c32, pad, constant_values=1.0)
    s32 = jnp.pad(s32, pad)
    ctx_pad = ((n_ctx, 0), (0, 0))
    return (jnp.pad(c64, ctx_pad, constant_values=1.0), jnp.pad(s64, ctx_pad),
            jnp.pad(c32, ctx_pad, constant_values=1.0), jnp.pad(s32, ctx_pad))


def _rms(x):
    return x * lax.rsqrt(jnp.mean(x * x, axis=-1, keepdims=True) + EPS)


def _adaln(x, g, shift, scale):
    return (_rms(x) * g) * (1.0 + scale) + shift


def _rope(z, cos, sin, is_x1, shift):
    zb = z.astype(BF16)
    partner = jnp.where(is_x1, pltpu.roll(zb, LANES - shift, 1), pltpu.roll(zb, shift, 1))
    return z * cos + partner.astype(F32) * sin


def _half_rms(z, lo, g):
    sq = z * z
    zero = jnp.zeros_like(sq)
    s_lo = jnp.sum(jnp.where(lo, sq, zero), axis=-1, keepdims=True)
    s_hi = jnp.sum(jnp.where(lo, zero, sq), axis=-1, keepdims=True)
    ms = jnp.where(lo, s_lo, s_hi) * (1.0 / HEAD_DIM)
    return z * lax.rsqrt(ms + EPS) * g


def _dot(a, b):
    return jnp.dot(a, b, preferred_element_type=F32)


def _const_spec(shape):
    return pl.BlockSpec(shape, lambda *_: (0,) * len(shape))


def _layer_spec(stacked, layer):
    shape = stacked.shape[1:]
    return pl.BlockSpec((None,) + shape, lambda *_: (layer,) + (0,) * len(shape))


def _mod_spec(modarr, layer, pos_off, n_ctx_tiles):
    return pl.BlockSpec((None, None, None) + modarr.shape[3:],
                        lambda bi, ti: (layer, bi, (ti + pos_off >= n_ctx_tiles).astype(jnp.int32), 0, 0))


def _token_specs(tokens, tile_off, n_ctx_tiles):
    tm, d = TOKEN_TILE, tokens[0].shape[-1]
    if len(tokens) == 1:
        return [pl.BlockSpec((None, tm, d), lambda bi, ti: (bi, ti + tile_off, 0))]
    assert tile_off == 0
    return [pl.BlockSpec((None, tm, d), lambda bi, ti: (bi, jnp.minimum(ti, n_ctx_tiles - 1), 0)),
            pl.BlockSpec((None, tm, d), lambda bi, ti: (bi, jnp.maximum(ti - n_ctx_tiles, 0), 0))]


def _read_tokens(tok_refs, n_ctx_tiles):
    if len(tok_refs) == 1:
        return tok_refs[0][...]
    ctx_ref, x_ref = tok_refs
    return jnp.where(pl.program_id(1) < n_ctx_tiles, ctx_ref[...], x_ref[...])


def _mod_kernel(c_ref, w_ref, b_ref, o_ref):
    cc = c_ref[...]
    a = (cc * jax.nn.sigmoid(cc)).astype(BF16)
    o_ref[...] = _dot(a, w_ref[...].astype(BF16)) + b_ref[...]


def _mod_call(cc, w_mod, b_mod):
    depth, d, n = w_mod.shape
    tn = 1536
    rows = cc.shape[0]
    return pl.pallas_call(
        _mod_kernel,
        grid=(depth, n // tn),
        in_specs=[
            pl.BlockSpec((rows, d), lambda l, j: (0, 0)),
            pl.BlockSpec((None, d, tn), lambda l, j: (l, 0, j)),
            pl.BlockSpec((None, 1, tn), lambda l, j: (l, 0, j)),
        ],
        out_specs=pl.BlockSpec((None, rows, tn), lambda l, j: (l, 0, j)),
        out_shape=jax.ShapeDtypeStruct((depth, rows, n), F32),
        compiler_params=pltpu.CompilerParams(vmem_limit_bytes=VMEM_LIMIT),
        name="mod_vectors",
    )(cc, w_mod, b_mod.reshape(depth, 1, n))


def _proj_kernel(*refs, n_tok, n_ctx_tiles):
    (mod_ref, g1_ref, w1_ref, wuq_ref, wukv_ref, gq_g_ref, gk_g_ref, gmq_ref, gmkv_ref,
     c64_ref, s64_ref, c32_ref, s32_ref,
     dq_o, dk_o, dv_o, gq_o, gk_o, gv_o, mq_o, mk_o, mv_o, hb_o) = refs[n_tok:]
    x = _read_tokens(refs[:n_tok], n_ctx_tiles)
    hb = _adaln(x, g1_ref[...], mod_ref[0:1, :], mod_ref[1:2, :]).astype(BF16)
    hb_o[...] = hb
    tm = x.shape[0]
    lane = lax.broadcasted_iota(jnp.int32, (tm, LANES), 1)
    lo = lane < HEAD_DIM
    x1_64 = (lane & ROT64) == 0
    x1_32 = (lane & ROT32) == 0
    c64, s64 = c64_ref[...], s64_ref[...]
    c32, s32 = c32_ref[...], s32_ref[...]
    qk_scale = HEAD_DIM ** -0.5 * LOG2E

    def seg(off, width):
        return _dot(hb, w1_ref[:, off:off + width])

    def blocks(z):
        return [(slice(i * LANES, (i + 1) * LANES), z[:, i * LANES:(i + 1) * LANES]) for i in range(z.shape[1] // LANES)]

    for sl, z in blocks(seg(OFF_DQ, BRANCH_W)):
        dq_o[:, sl] = (_rope(z, c64, s64, x1_64, ROT64) * qk_scale).astype(BF16)
    for sl, z in blocks(seg(OFF_DK, BRANCH_W)):
        dk_o[:, sl] = _rope(z, c64, s64, x1_64, ROT64).astype(BF16)
    dv_o[...] = seg(OFF_DV, BRANCH_W).astype(BF16)
    for sl, z in blocks(seg(OFF_GQ, BRANCH_W)):
        gq = _half_rms(z, lo, gq_g_ref[...])
        gq_o[:, sl] = (_rope(gq, c64, s64, x1_64, ROT64) * qk_scale).astype(BF16)
    gkv = seg(OFF_GK, 2 * LANES)
    gk = _half_rms(gkv[:, :LANES], lo, gk_g_ref[...])
    gk_o[...] = _rope(gk, c64, s64, x1_64, ROT64).astype(BF16)
    gv_o[...] = gkv[:, LANES:].astype(BF16)

    zm = seg(OFF_MCQ, W1_WIDTH - OFF_MCQ)
    mcq = (_rms(zm[:, :MLA_Q_LORA]) * gmq_ref[...]).astype(BF16)
    mckv = (_rms(zm[:, MLA_Q_LORA:MLA_Q_LORA + MLA_KV_LORA]) * gmkv_ref[...]).astype(BF16)
    mkr = _rope(zm[:, MLA_Q_LORA + MLA_KV_LORA:], c32, s32, x1_32, ROT32)
    mla_scale = MLA_QK ** -0.5 * LOG2E
    for sl, z in blocks(_dot(mcq, wuq_ref[...])):
        mq_o[:, sl] = (_rope(z, c32, s32, x1_32, ROT32) * mla_scale).astype(BF16)
    mkv = _dot(mckv, wukv_ref[...])
    for sl, z in blocks(mkv[:, :MLA_HEADS * LANES]):
        mk_o[:, sl] = (z + mkr).astype(BF16)
    mv_o[...] = mkv[:, MLA_HEADS * LANES:].astype(BF16)


def _proj_call(layer, tokens, modarr, g1, w1, wuq, wukv, gq_g, gk_g, gmq, gmkv, tables, n_ctx_tiles, ctx_queries):
    b = tokens[0].shape[0]
    t = sum(a.shape[1] for a in tokens)
    tm = TOKEN_TILE
    tok = lambda w: pl.BlockSpec((None, tm, w), lambda bi, ti: (bi, ti, 0))
    q_rows = t if ctx_queries else t - n_ctx_tiles * tm
    qtok = tok if ctx_queries else (
        lambda w: pl.BlockSpec((None, tm, w), lambda bi, ti: (bi, jnp.maximum(ti - n_ctx_tiles, 0), 0)))
    is_q = (True, False, False, True, False, False, True, False, False, False)
    tab = pl.BlockSpec((tm, LANES), lambda bi, ti: (ti, 0))
    widths = (BRANCH_W, BRANCH_W, BRANCH_W, BRANCH_W, LANES, LANES, MLA_HEADS * LANES, MLA_HEADS * LANES, BRANCH_W,
              tokens[0].shape[-1])
    return pl.pallas_call(
        functools.partial(_proj_kernel, n_tok=len(tokens), n_ctx_tiles=n_ctx_tiles),
        grid=(b, t // tm),
        in_specs=_token_specs(tokens, 0, n_ctx_tiles) + [_mod_spec(modarr, layer, 0, n_ctx_tiles)] + [
            _layer_spec(a, layer) for a in (g1, w1, wuq, wukv, gq_g, gk_g, gmq, gmkv)] + [tab, tab, tab, tab],
        out_specs=[(qtok if q else tok)(w) for w, q in zip(widths, is_q)],
        out_shape=[jax.ShapeDtypeStruct((b, q_rows if q else t, w), BF16) for w, q in zip(widths, is_q)],
        compiler_params=pltpu.CompilerParams(
            dimension_semantics=("parallel", "arbitrary"), vmem_limit_bytes=VMEM_LIMIT),
        name="input_projection",
    )(*tokens, modarr, g1, w1, wuq, wukv, gq_g, gk_g, gmq, gmkv, *tables)


def _attn_kernel(*refs, kind, n_ctx, lam_init, first_tile_is_ctx, upb):
    if kind == "diff":
        q_ref, k_ref, v_ref, lam_ref, gdo_ref, o_ref, vt_ref = refs
    else:
        q_ref, k_ref, v_ref, o_ref, vt_ref = refs
    tq = q_ref.shape[0]
    lane = lax.broadcasted_iota(jnp.int32, (tq, LANES), 1)
    lo = lane < HEAD_DIM
    n_keys = k_ref.shape[0]
    n_kv = vt_ref.shape[0]
    qw = q_ref.shape[1] // upb
    kw = k_ref.shape[1] // n_kv

    @pl.when(pl.program_id(2) == 0)
    def _():
        for i in range(n_kv):
            vt_ref[i, 0:LANES, :] = v_ref[:, i * LANES:(i + 1) * LANES].astype(F32).T.astype(BF16)
            vt_ref[i, LANES:, :] = jnp.ones((SUM_ROWS, n_keys), BF16)

    def key_chunks(nk):
        step = KEY_CHUNK if nk % KEY_CHUNK == 0 else nk
        return [(c, c + step) for c in range(0, nk, step)]

    def unit_queries(u):
        q = q_ref[:, u * qw:(u + 1) * qw]
        if kind == "mla":
            return q[:, :LANES], q[:, LANES:]
        zero = jnp.zeros_like(q)
        return jnp.where(lo, q, zero), jnp.where(lo, zero, q)

    def chunk_scores(u, qa, qb, c0, c1):
        kv = u % n_kv
        k = k_ref[c0:c1, kv * kw:(kv + 1) * kw]
        ka, kb = (k[:, :LANES], k[:, LANES:]) if kind == "mla" else (k, k)
        nt = (((1,), (1,)), ((), ()))
        return (lax.dot_general(ka, qa, nt, preferred_element_type=F32),
                lax.dot_general(kb, qb, nt, preferred_element_type=F32))

    def chunk_softmax_pv(u, c0, c1, st):
        m = jnp.max(st, axis=0, keepdims=True)
        p = jnp.exp2((st - m).astype(BF16))
        return m, _dot(vt_ref[u % n_kv, :, c0:c1], p)

    def merge_chunks(parts):
        if len(parts) == 1:
            ot = parts[0][1]
        else:
            m_all = functools.reduce(jnp.maximum, [m for m, _ in parts])
            ot = sum(acc * jnp.exp2(m - m_all) for m, acc in parts)
        return ot[0:LANES] / ot[LANES:LANES + 1]

    def write_unit(u, oa, ob):
        if kind == "diff":
            o = (oa - lam_ref[0:1, 0:1] * ob).T
            o = _rms(o) * gdo_ref[...] * (1.0 - lam_init)
        else:
            row = lax.broadcasted_iota(jnp.int32, oa.shape, 0)
            o = jnp.where(row < HEAD_DIM, oa, ob).T
        o_ref[:, u * LANES:(u + 1) * LANES] = o.astype(BF16)

    def compute(nk):
        chunks = key_chunks(nk)
        for u in range(upb):
            sa, sb = chunk_scores(u, *unit_queries(u), 0, nk)
            oa = merge_chunks([chunk_softmax_pv(u, c0, c1, sa[c0:c1]) for c0, c1 in chunks])
            ob = merge_chunks([chunk_softmax_pv(u, c0, c1, sb[c0:c1]) for c0, c1 in chunks])
            write_unit(u, oa, ob)

    if first_tile_is_ctx:
        j = pl.program_id(2)
        pl.when(j == 0)(lambda: compute(n_ctx))
        pl.when(j > 0)(lambda: compute(n_keys))
    else:
        compute(n_keys)


def _attn_call(kind, layer, q, k, v, extra, *, n_ctx, lam_init, include_ctx):
    b, t, _ = k.shape
    tq = TOKEN_TILE if include_ctx else LATENT_QUERY_TILE
    upb = UNITS_PER_STEP
    nq = q.shape[1] // tq
    qw = q.shape[-1] // N_UNITS
    shared_kv = kind == "gqa"
    n_kv = 1 if shared_kv else upb
    kw = LANES if shared_kv else k.shape[-1] // N_UNITS
    kv_map = (lambda bi, g, j: (bi, 0, 0)) if shared_kv else (lambda bi, g, j: (bi, 0, g))
    in_specs = [
        pl.BlockSpec((None, tq, upb * qw), lambda bi, g, j: (bi, j, g)),
        pl.BlockSpec((None, t, n_kv * kw), kv_map),
        pl.BlockSpec((None, t, n_kv * LANES), kv_map),
    ]
    in_specs += [_layer_spec(a, layer) for a in extra]
    return pl.pallas_call(
        functools.partial(_attn_kernel, kind=kind, n_ctx=n_ctx, lam_init=lam_init,
                          first_tile_is_ctx=include_ctx, upb=upb),
        grid=(b, N_UNITS // upb, nq),
        in_specs=in_specs,
        out_specs=pl.BlockSpec((None, tq, upb * LANES), lambda bi, g, j: (bi, j, g)),
        out_shape=jax.ShapeDtypeStruct((b, nq * tq, BRANCH_W), BF16),
        scratch_shapes=[pltpu.VMEM((n_kv, LANES + SUM_ROWS, t), BF16)],
        compiler_params=pltpu.CompilerParams(
            dimension_semantics=("parallel", "arbitrary", "arbitrary"), vmem_limit_bytes=VMEM_LIMIT),
        name="attention_" + kind,
    )(q, k, v, *extra)


def _merge_kernel(*refs, n_tok, n_ctx_tiles):
    (mod_ref, hb_ref, wg_ref, bg_ref, od_ref, og_ref, om_ref,
     wbd_ref, wbg_ref, wbm_ref, wout_ref, xo_ref) = refs[n_tok:]
    x = _read_tokens(refs[:n_tok], n_ctx_tiles)
    d = x.shape[-1]
    hb = hb_ref[...]
    y = None
    for i, (o_ref, wb_ref) in enumerate(((od_ref, wbd_ref), (og_ref, wbg_ref), (om_ref, wbm_ref))):
        sl = slice(i * d, (i + 1) * d)
        gate = jax.nn.sigmoid(_dot(hb, wg_ref[:, sl]) + bg_ref[:, sl])
        term = gate * _dot(o_ref[...], wb_ref[...])
        y = term if y is None else y + term
    out = _dot(y.astype(BF16), wout_ref[...])
    xo_ref[...] = x + mod_ref[2:3, :] * out


def _merge_call(layer, tokens, modarr, hb, wg, bg, od, og, om, wbd, wbg, wbm, wout, *, x_off, n_ctx_tiles):
    b, d = tokens[0].shape[0], tokens[0].shape[-1]
    tm = TOKEN_TILE
    nt = od.shape[1] // tm
    tok = lambda w: pl.BlockSpec((None, tm, w), lambda bi, ti: (bi, ti, 0))
    return pl.pallas_call(
        functools.partial(_merge_kernel, n_tok=len(tokens), n_ctx_tiles=n_ctx_tiles),
        grid=(b, nt),
        in_specs=_token_specs(tokens, x_off, n_ctx_tiles) + [_mod_spec(modarr, layer, x_off, n_ctx_tiles)] + [
            pl.BlockSpec((None, tm, d), lambda bi, ti: (bi, ti + x_off, 0))] + [
            _layer_spec(a, layer) for a in (wg, bg)] + [tok(BRANCH_W)] * 3 + [
            _layer_spec(a, layer) for a in (wbd, wbg, wbm, wout)],
        out_specs=tok(d),
        out_shape=jax.ShapeDtypeStruct((b, nt * tm, d), F32),
        compiler_params=pltpu.CompilerParams(
            dimension_semantics=("parallel", "arbitrary"), vmem_limit_bytes=VMEM_LIMIT),
        name="branch_merge",
    )(*tokens, modarr, hb, wg, bg, od, og, om, wbd, wbg, wbm, wout)


def _ffn_kernel(*refs, final):
    if final:
        x_ref, mod_ref, g2_ref, win_ref, wout_ref, gf_ref, o_ref = refs
    else:
        x_ref, mod_ref, g2_ref, win_ref, wout_ref, o_ref = refs
    x = x_ref[...]
    hidden = wout_ref.shape[0]
    hb = _adaln(x, g2_ref[...], mod_ref[3:4, :], mod_ref[4:5, :]).astype(BF16)
    g = _dot(hb, win_ref[:, :hidden])
    u = _dot(hb, win_ref[:, hidden:])
    a = (g * jax.nn.sigmoid(g) * u).astype(BF16)
    xn = x + mod_ref[5:6, :] * _dot(a, wout_ref[...])
    if final:
        xn = _rms(xn) * gf_ref[...]
    o_ref[...] = xn


def _ffn_call(layer, xin, modarr, g2, win, wout, gf, *, pos_off, n_ctx_tiles):
    b, t, d = xin.shape
    tm = TOKEN_TILE
    tok = pl.BlockSpec((None, tm, d), lambda bi, ti: (bi, ti, 0))
    final = gf is not None
    in_specs = [tok, _mod_spec(modarr, layer, pos_off, n_ctx_tiles)] + [
        _layer_spec(a, layer) for a in (g2, win, wout)]
    args = [xin, modarr, g2, win, wout]
    if final:
        in_specs.append(_const_spec((1, d)))
        args.append(gf)
    return pl.pallas_call(
        functools.partial(_ffn_kernel, final=final),
        grid=(b, t // tm),
        in_specs=in_specs,
        out_specs=tok,
        out_shape=jax.ShapeDtypeStruct((b, t, d), F32),
        compiler_params=pltpu.CompilerParams(
            dimension_semantics=("parallel", "arbitrary"), vmem_limit_bytes=VMEM_LIMIT),
        name="swiglu",
    )(*args)


def kernel(x, c, ctx, c_ctx, w_mod, b_mod, g_norm1, w_in, b_gate, lam_q1, lam_k1, lam_q2, lam_k2, g_diff_out, g_gqa_q, g_gqa_k, g_mla_q, w_mla_uq, g_mla_kv, w_mla_ukv, w_br_diff, w_br_gqa, w_br_mla, w_out, g_norm2, w_ffn_in, w_ffn_out, g_final):
    b, seq, d = x.shape
    n_ctx = ctx.shape[1]
    depth = w_in.shape[0]
    assert n_ctx % TOKEN_TILE == 0 and seq % TOKEN_TILE == 0 and seq % GRID_W == 0
    n_ctx_tiles = n_ctx // TOKEN_TILE

    mod_rows = 32
    cc = jnp.zeros((mod_rows, d), F32).at[:b].set(c).at[b].set(c_ctx)
    mods = _mod_call(cc, w_mod, b_mod)
    mod_lat = mods[:, :b].reshape(depth, b, N_MOD, d)
    mod_ctx = jnp.broadcast_to(mods[:, b].reshape(depth, 1, N_MOD, d), (depth, b, N_MOD, d))
    modarr = jnp.stack([mod_ctx, mod_lat], axis=2)
    tables = _rope_tables(seq, n_ctx)

    lam_inits = [_lambda_init(l) for l in range(depth)]
    lam = (jnp.exp(jnp.sum(lam_q1 * lam_k1, axis=-1)) - jnp.exp(jnp.sum(lam_q2 * lam_k2, axis=-1))
           + jnp.asarray(lam_inits, F32))
    lam_vec = jnp.broadcast_to(lam.reshape(depth, 1, 1), (depth, 1, LANES))

    def row(a):
        return a.reshape(depth, 1, -1)

    w1, wg, wuq, wukv, wbg = _prep_weights(w_in, w_mla_uq, w_mla_ukv, w_br_gqa)
    wbd, wbm, wout = w_br_diff.astype(BF16), w_br_mla.astype(BF16), w_out.astype(BF16)
    wfi, wfo = w_ffn_in.astype(BF16), w_ffn_out.astype(BF16)
    g1, g2, bg = row(g_norm1), row(g_norm2), row(b_gate)
    gq_g, gk_g = row(jnp.tile(g_gqa_q, (1, 2))), row(jnp.tile(g_gqa_k, (1, 2)))
    gmq, gmkv, gdo = row(g_mla_q), row(g_mla_kv), row(g_diff_out)

    tokens = (ctx, x)
    for l in range(depth):
        last = l == depth - 1
        dq, dk, dv, gq, gk, gv, mq, mk, mv, hb = _proj_call(
            l, tokens, modarr, g1, w1, wuq, wukv, gq_g, gk_g, gmq, gmkv, tables, n_ctx_tiles, not last)

        attn = functools.partial(_attn_call, n_ctx=n_ctx, lam_init=lam_inits[l], include_ctx=not last)
        od = attn("diff", l, dq, dk, dv, (lam_vec, gdo))
        og = attn("gqa", l, gq, gk, gv, ())
        om = attn("mla", l, mq, mk, mv, ())

        x_off = n_ctx_tiles if last else 0
        xmid = _merge_call(l, tokens, modarr, hb, wg, bg, od, og, om, wbd, wbg, wbm, wout,
                           x_off=x_off, n_ctx_tiles=n_ctx_tiles)
        xout = _ffn_call(l, xmid, modarr, g2, wfi, wfo, g_final.reshape(1, d) if last else None,
                         pos_off=x_off, n_ctx_tiles=n_ctx_tiles)
        tokens = (xout,)
    return xout
```

```python
import functools
import math

import jax
import jax.numpy as jnp
from jax import lax
from jax.experimental import pallas as pl
from jax.experimental.pallas import tpu as pltpu

F32 = jnp.float32
BF16 = jnp.bfloat16

GRID_W = 64
ROPE_THETA = 10000.0
EPS = 1e-6
N_MOD = 6

DIFF_HEADS = 4
HEAD_DIM = 64
GQA_KV_HEADS = 2
GQA_GROUP = 4
MLA_HEADS = 8
MLA_Q_LORA = 384
MLA_KV_LORA = 256
MLA_NOPE = 64
MLA_ROPE = 32
MLA_QK = MLA_NOPE + MLA_ROPE
MLA_V = 64
ROT64 = HEAD_DIM // 4
ROT32 = MLA_ROPE // 4
BRANCH_W = 512
N_UNITS = 4

LANES = 128
UNITS_PER_STEP = 4
KEY_CHUNK = 256
SUM_ROWS = 16
LOG2E = math.log2(math.e)
TOKEN_TILE = 256
VMEM_LIMIT = 56 * 1024 * 1024

OFF_DQ, OFF_DK, OFF_DV, OFF_GQ, OFF_GK, OFF_GV = 0, 512, 1024, 1536, 2048, 2176
OFF_MCQ, OFF_MCKV, OFF_MKR, W1_WIDTH = 2304, 2688, 2944, 3072


def _lambda_init(layer):
    return 0.8 - 0.6 * math.exp(-0.3 * layer)


_IN_GQ, _IN_GK, _IN_MKR, _IN_GATE = 1536, 2048, 2944, 2976


def _pair_heads(a, axis):
    shape = a.shape
    a = a.reshape(shape[:axis] + (GQA_KV_HEADS, GQA_GROUP, HEAD_DIM) + shape[axis + 1:])
    return jnp.swapaxes(a, axis, axis + 1).reshape(shape)


def _prep_weights(w_in, w_mla_uq, w_mla_ukv, w_br_gqa):
    depth = w_in.shape[0]
    mkr = jnp.pad(w_in[:, :, _IN_MKR:_IN_GATE], ((0, 0), (0, 0), (MLA_NOPE, LANES - MLA_QK)))
    w1 = jnp.concatenate([w_in[:, :, :_IN_GQ], _pair_heads(w_in[:, :, _IN_GQ:_IN_GK], 2),
                          w_in[:, :, _IN_GK:_IN_MKR], mkr], axis=2).astype(BF16)
    wg = w_in[:, :, _IN_GATE:].astype(BF16)
    uq = w_mla_uq.reshape(depth, MLA_Q_LORA, MLA_HEADS, MLA_QK)
    uq = jnp.pad(uq, ((0, 0), (0, 0), (0, 0), (0, LANES - MLA_QK)))
    uq = uq.reshape(depth, MLA_Q_LORA, MLA_HEADS * LANES).astype(BF16)
    ukv = w_mla_ukv.reshape(depth, MLA_KV_LORA, MLA_HEADS, MLA_NOPE + MLA_V)
    ukn = jnp.pad(ukv[..., :MLA_NOPE], ((0, 0), (0, 0), (0, 0), (0, LANES - MLA_NOPE)))
    ukn = ukn.reshape(depth, MLA_KV_LORA, -1)
    uv = ukv[..., MLA_NOPE:].reshape(depth, MLA_KV_LORA, -1)
    ukv2 = jnp.concatenate([ukn, uv], axis=2).astype(BF16)
    wbg = _pair_heads(w_br_gqa, 1).astype(BF16)
    return w1, wg, uq, ukv2, wbg


def _rope_tables(seq, n_ctx):
    rows = seq // GRID_W
    row = jnp.repeat(jnp.arange(rows, dtype=jnp.int32), GRID_W).astype(F32)
    col = jnp.tile(jnp.arange(GRID_W, dtype=jnp.int32), rows).astype(F32)

    def table(dim):
        a = dim // 2
        freqs = ROPE_THETA ** (-jnp.arange(0, a, 2, dtype=F32) / a)
        ang_r, ang_c = row[:, None] * freqs, col[:, None] * freqs
        cos = jnp.concatenate([jnp.cos(ang_r), jnp.cos(ang_r), jnp.cos(ang_c), jnp.cos(ang_c)], axis=-1)
        sin = jnp.concatenate([-jnp.sin(ang_r), jnp.sin(ang_r), -jnp.sin(ang_c), jnp.sin(ang_c)], axis=-1)
        return cos, sin

    c64, s64 = table(HEAD_DIM)
    c64, s64 = jnp.tile(c64, (1, 2)), jnp.tile(s64, (1, 2))
    c32, s32 = table(MLA_ROPE)
    pad = ((0, 0), (MLA_NOPE, LANES - MLA_QK))
    c32 = jnp.pad(c32, pad, constant_values=1.0)
    s32 = jnp.pad(s32, pad)
    ctx_pad = ((n_ctx, 0), (0, 0))
    return (jnp.pad(c64, ctx_pad, constant_values=1.0), jnp.pad(s64, ctx_pad),
            jnp.pad(c32, ctx_pad, constant_values=1.0), jnp.pad(s32, ctx_pad))


def _rms(x):
    return x * lax.rsqrt(jnp.mean(x * x, axis=-1, keepdims=True) + EPS)


def _adaln(x, g, shift, scale):
    return (_rms(x) * g) * (1.0 + scale) + shift


def _rope(z, cos, sin, is_x1, shift):
    zb = z.astype(BF16)
    partner = jnp.where(is_x1, pltpu.roll(zb, LANES - shift, 1), pltpu.roll(zb, shift, 1))
    return z * cos + partner.astype(F32) * sin


def _half_rms(z, lo, g):
    sq = z * z
    zero = jnp.zeros_like(sq)
    s_lo = jnp.sum(jnp.where(lo, sq, zero), axis=-1, keepdims=True)
    s_hi = jnp.sum(jnp.where(lo, zero, sq), axis=-1, keepdims=True)
    ms = jnp.where(lo, s_lo, s_hi) * (1.0 / HEAD_DIM)
    return z * lax.rsqrt(ms + EPS) * g


def _dot(a, b):
    return jnp.dot(a, b, preferred_element_type=F32)


def _const_spec(shape):
    return pl.BlockSpec(shape, lambda *_: (0,) * len(shape))


def _layer_spec(stacked, layer):
    shape = stacked.shape[1:]
    return pl.BlockSpec((None,) + shape, lambda *_: (layer,) + (0,) * len(shape))


def _mod_spec(modarr, layer, pos_off, n_ctx_tiles):
    return pl.BlockSpec((None, None, None) + modarr.shape[3:],
                        lambda bi, ti: (layer, bi, (ti + pos_off >= n_ctx_tiles).astype(jnp.int32), 0, 0))


def _token_specs(tokens, tile_off, n_ctx_tiles):
    tm, d = TOKEN_TILE, tokens[0].shape[-1]
    if len(tokens) == 1:
        return [pl.BlockSpec((None, tm, d), lambda bi, ti: (bi, ti + tile_off, 0))]
    assert tile_off == 0
    return [pl.BlockSpec((None, tm, d), lambda bi, ti: (bi, jnp.minimum(ti, n_ctx_tiles - 1), 0)),
            pl.BlockSpec((None, tm, d), lambda bi, ti: (bi, jnp.maximum(ti - n_ctx_tiles, 0), 0))]


def _read_tokens(tok_refs, n_ctx_tiles):
    if len(tok_refs) == 1:
        return tok_refs[0][...]
    ctx_ref, x_ref = tok_refs
    return jnp.where(pl.program_id(1) < n_ctx_tiles, ctx_ref[...], x_ref[...])


def _mod_kernel(c_ref, w_ref, b_ref, o_ref):
    cc = c_ref[...]
    a = (cc * jax.nn.sigmoid(cc)).astype(BF16)
    o_ref[...] = _dot(a, w_ref[...].astype(BF16)) + b_ref[...]


def _mod_call(cc, w_mod, b_mod):
    depth, d, n = w_mod.shape
    tn = 1536
    rows = cc.shape[0]
    return pl.pallas_call(
        _mod_kernel,
        grid=(depth, n // tn),
        in_specs=[
            pl.BlockSpec((rows, d), lambda l, j: (0, 0)),
            pl.BlockSpec((None, d, tn), lambda l, j: (l, 0, j)),
            pl.BlockSpec((None, 1, tn), lambda l, j: (l, 0, j)),
        ],
        out_specs=pl.BlockSpec((None, rows, tn), lambda l, j: (l, 0, j)),
        out_shape=jax.ShapeDtypeStruct((depth, rows, n), F32),
        compiler_params=pltpu.CompilerParams(vmem_limit_bytes=VMEM_LIMIT),
        name="mod_vectors",
    )(cc, w_mod, b_mod.reshape(depth, 1, n))


def _proj_kernel(*refs, n_tok, n_ctx_tiles):
    (mod_ref, g1_ref, w1_ref, wuq_ref, wukv_ref, gq_g_ref, gk_g_ref, gmq_ref, gmkv_ref,
     c64_ref, s64_ref, c32_ref, s32_ref,
     dq_o, dk_o, dv_o, gq_o, gk_o, gv_o, mq_o, mk_o, mv_o) = refs[n_tok:]
    x = _read_tokens(refs[:n_tok], n_ctx_tiles)
    hb = _adaln(x, g1_ref[...], mod_ref[0:1, :], mod_ref[1:2, :]).astype(BF16)
    tm = x.shape[0]
    lane = lax.broadcasted_iota(jnp.int32, (tm, LANES), 1)
    lo = lane < HEAD_DIM
    x1_64 = (lane & ROT64) == 0
    x1_32 = (lane & ROT32) == 0
    c64, s64 = c64_ref[...], s64_ref[...]
    c32, s32 = c32_ref[...], s32_ref[...]
    qk_scale = HEAD_DIM ** -0.5 * LOG2E

    def seg(off, width):
        return _dot(hb, w1_ref[:, off:off + width])

    def blocks(z):
        return [(slice(i * LANES, (i + 1) * LANES), z[:, i * LANES:(i + 1) * LANES]) for i in range(z.shape[1] // LANES)]

    zm = seg(OFF_MCQ, W1_WIDTH - OFF_MCQ)
    mcq = (_rms(zm[:, :MLA_Q_LORA]) * gmq_ref[...]).astype(BF16)
    mckv = (_rms(zm[:, MLA_Q_LORA:MLA_Q_LORA + MLA_KV_LORA]) * gmkv_ref[...]).astype(BF16)
    mkr = _rope(zm[:, MLA_Q_LORA + MLA_KV_LORA:], c32, s32, x1_32, ROT32)

    for sl, z in blocks(seg(OFF_DQ, BRANCH_W)):
        dq_o[:, sl] = (_rope(z, c64, s64, x1_64, ROT64) * qk_scale).astype(BF16)
    for sl, z in blocks(seg(OFF_DK, BRANCH_W)):
        dk_o[:, sl] = _rope(z, c64, s64, x1_64, ROT64).astype(BF16)
    dv_o[...] = seg(OFF_DV, BRANCH_W).astype(BF16)
    for sl, z in blocks(seg(OFF_GQ, BRANCH_W)):
        gq = _half_rms(z, lo, gq_g_ref[...])
        gq_o[:, sl] = (_rope(gq, c64, s64, x1_64, ROT64) * qk_scale).astype(BF16)
    gkv = seg(OFF_GK, 2 * LANES)
    gk = _half_rms(gkv[:, :LANES], lo, gk_g_ref[...])
    gk_o[...] = _rope(gk, c64, s64, x1_64, ROT64).astype(BF16)
    gv_o[...] = gkv[:, LANES:].astype(BF16)

    mla_scale = MLA_QK ** -0.5 * LOG2E
    for sl, z in blocks(_dot(mcq, wuq_ref[...])):
        mq_o[:, sl] = (_rope(z, c32, s32, x1_32, ROT32) * mla_scale).astype(BF16)
    mkv = _dot(mckv, wukv_ref[...])
    for sl, z in blocks(mkv[:, :MLA_HEADS * LANES]):
        mk_o[:, sl] = (z + mkr).astype(BF16)
    mv_o[...] = mkv[:, MLA_HEADS * LANES:].astype(BF16)


def _proj_call(layer, tokens, modarr, g1, w1, wuq, wukv, gq_g, gk_g, gmq, gmkv, tables, n_ctx_tiles):
    b = tokens[0].shape[0]
    t = sum(a.shape[1] for a in tokens)
    tm = TOKEN_TILE
    tok = lambda w: pl.BlockSpec((None, tm, w), lambda bi, ti: (bi, ti, 0))
    tab = pl.BlockSpec((tm, LANES), lambda bi, ti: (ti, 0))
    widths = (BRANCH_W, BRANCH_W, BRANCH_W, BRANCH_W, LANES, LANES, MLA_HEADS * LANES, MLA_HEADS * LANES, BRANCH_W)
    return pl.pallas_call(
        functools.partial(_proj_kernel, n_tok=len(tokens), n_ctx_tiles=n_ctx_tiles),
        grid=(b, t // tm),
        in_specs=_token_specs(tokens, 0, n_ctx_tiles) + [_mod_spec(modarr, layer, 0, n_ctx_tiles)] + [
            _layer_spec(a, layer) for a in (g1, w1, wuq, wukv, gq_g, gk_g, gmq, gmkv)] + [tab, tab, tab, tab],
        out_specs=[tok(w) for w in widths],
        out_shape=[jax.ShapeDtypeStruct((b, t, w), BF16) for w in widths],
        compiler_params=pltpu.CompilerParams(
            dimension_semantics=("parallel", "arbitrary"), vmem_limit_bytes=VMEM_LIMIT),
        name="input_projection",
    )(*tokens, modarr, g1, w1, wuq, wukv, gq_g, gk_g, gmq, gmkv, *tables)


def _attn_kernel(*refs, kind, n_ctx, lam_init, first_tile_is_ctx, upb):
    if kind == "diff":
        q_ref, k_ref, v_ref, lam_ref, gdo_ref, o_ref, vt_ref = refs
    else:
        q_ref, k_ref, v_ref, o_ref, vt_ref = refs
    tq = q_ref.shape[0]
    lane = lax.broadcasted_iota(jnp.int32, (tq, LANES), 1)
    lo = lane < HEAD_DIM
    n_keys = k_ref.shape[0]
    n_kv = vt_ref.shape[0]
    qw = q_ref.shape[1] // upb
    kw = k_ref.shape[1] // n_kv

    @pl.when(pl.program_id(2) == 0)
    def _():
        for i in range(n_kv):
            vt_ref[i, 0:LANES, :] = v_ref[:, i * LANES:(i + 1) * LANES].astype(F32).T.astype(BF16)
            vt_ref[i, LANES:, :] = jnp.ones((SUM_ROWS, n_keys), BF16)

    def key_chunks(nk):
        step = KEY_CHUNK if nk % KEY_CHUNK == 0 else nk
        return [(c, c + step) for c in range(0, nk, step)]

    def unit_queries(u):
        q = q_ref[:, u * qw:(u + 1) * qw]
        if kind == "mla":
            return q[:, :LANES], q[:, LANES:]
        zero = jnp.zeros_like(q)
        return jnp.where(lo, q, zero), jnp.where(lo, zero, q)

    def chunk_scores(u, qa, qb, c0, c1):
        kv = u % n_kv
        k = k_ref[c0:c1, kv * kw:(kv + 1) * kw]
        ka, kb = (k[:, :LANES], k[:, LANES:]) if kind == "mla" else (k, k)
        nt = (((1,), (1,)), ((), ()))
        return (lax.dot_general(ka, qa, nt, preferred_element_type=F32),
                lax.dot_general(kb, qb, nt, preferred_element_type=F32))

    def chunk_softmax_pv(u, c0, c1, st):
        m = jnp.max(st, axis=0, keepdims=True)
        p = jnp.exp2((st - m).astype(BF16))
        return m, _dot(vt_ref[u % n_kv, :, c0:c1], p)

    def merge_chunks(parts):
        if len(parts) == 1:
            ot = parts[0][1]
        else:
            m_all = functools.reduce(jnp.maximum, [m for m, _ in parts])
            ot = sum(acc * jnp.exp2(m - m_all) for m, acc in parts)
        return ot[0:LANES] / ot[LANES:LANES + 1]

    def write_unit(u, oa, ob):
        if kind == "diff":
            o = (oa - lam_ref[0:1, 0:1] * ob).T
            o = _rms(o) * gdo_ref[...] * (1.0 - lam_init)
        else:
            row = lax.broadcasted_iota(jnp.int32, oa.shape, 0)
            o = jnp.where(row < HEAD_DIM, oa, ob).T
        o_ref[:, u * LANES:(u + 1) * LANES] = o.astype(BF16)

    def compute(nk):
        chunks = key_chunks(nk)
        for u in range(upb):
            sa, sb = chunk_scores(u, *unit_queries(u), 0, nk)
            oa = merge_chunks([chunk_softmax_pv(u, c0, c1, sa[c0:c1]) for c0, c1 in chunks])
            ob = merge_chunks([chunk_softmax_pv(u, c0, c1, sb[c0:c1]) for c0, c1 in chunks])
            write_unit(u, oa, ob)

    if first_tile_is_ctx:
        j = pl.program_id(2)
        pl.when(j == 0)(lambda: compute(n_ctx))
        pl.when(j > 0)(lambda: compute(n_keys))
    else:
        compute(n_keys)


def _attn_call(kind, layer, q, k, v, extra, *, n_ctx, lam_init, include_ctx):
    b, t, _ = k.shape
    tq = TOKEN_TILE
    upb = UNITS_PER_STEP
    n_ctx_tiles = n_ctx // tq
    q_off = 0 if include_ctx else n_ctx_tiles
    nq = t // tq - q_off
    qw = q.shape[-1] // N_UNITS
    shared_kv = kind == "gqa"
    n_kv = 1 if shared_kv else upb
    kw = LANES if shared_kv else k.shape[-1] // N_UNITS
    kv_map = (lambda bi, g, j: (bi, 0, 0)) if shared_kv else (lambda bi, g, j: (bi, 0, g))
    in_specs = [
        pl.BlockSpec((None, tq, upb * qw), lambda bi, g, j: (bi, j + q_off, g)),
        pl.BlockSpec((None, t, n_kv * kw), kv_map),
        pl.BlockSpec((None, t, n_kv * LANES), kv_map),
    ]
    in_specs += [_layer_spec(a, layer) for a in extra]
    return pl.pallas_call(
        functools.partial(_attn_kernel, kind=kind, n_ctx=n_ctx, lam_init=lam_init,
                          first_tile_is_ctx=include_ctx, upb=upb),
        grid=(b, N_UNITS // upb, nq),
        in_specs=in_specs,
        out_specs=pl.BlockSpec((None, tq, upb * LANES), lambda bi, g, j: (bi, j, g)),
        out_shape=jax.ShapeDtypeStruct((b, nq * tq, BRANCH_W), BF16),
        scratch_shapes=[pltpu.VMEM((n_kv, LANES + SUM_ROWS, t), BF16)],
        compiler_params=pltpu.CompilerParams(
            dimension_semantics=("parallel", "arbitrary", "arbitrary"), vmem_limit_bytes=VMEM_LIMIT),
        name="attention_" + kind,
    )(q, k, v, *extra)


def _merge_kernel(*refs, n_tok, n_ctx_tiles):
    (mod_ref, g1_ref, wg_ref, bg_ref, od_ref, og_ref, om_ref,
     wbd_ref, wbg_ref, wbm_ref, wout_ref, xo_ref) = refs[n_tok:]
    x = _read_tokens(refs[:n_tok], n_ctx_tiles)
    d = x.shape[-1]
    hb = _adaln(x, g1_ref[...], mod_ref[0:1, :], mod_ref[1:2, :]).astype(BF16)
    y = None
    for i, (o_ref, wb_ref) in enumerate(((od_ref, wbd_ref), (og_ref, wbg_ref), (om_ref, wbm_ref))):
        sl = slice(i * d, (i + 1) * d)
        gate = jax.nn.sigmoid(_dot(hb, wg_ref[:, sl]) + bg_ref[:, sl])
        term = gate * _dot(o_ref[...], wb_ref[...])
        y = term if y is None else y + term
    out = _dot(y.astype(BF16), wout_ref[...])
    xo_ref[...] = x + mod_ref[2:3, :] * out


def _merge_call(layer, tokens, modarr, g1, wg, bg, od, og, om, wbd, wbg, wbm, wout, *, x_off, n_ctx_tiles):
    b, d = tokens[0].shape[0], tokens[0].shape[-1]
    tm = TOKEN_TILE
    nt = od.shape[1] // tm
    tok = lambda w: pl.BlockSpec((None, tm, w), lambda bi, ti: (bi, ti, 0))
    return pl.pallas_call(
        functools.partial(_merge_kernel, n_tok=len(tokens), n_ctx_tiles=n_ctx_tiles),
        grid=(b, nt),
        in_specs=_token_specs(tokens, x_off, n_ctx_tiles) + [_mod_spec(modarr, layer, x_off, n_ctx_tiles)] + [
            _layer_spec(a, layer) for a in (g1, wg, bg)] + [tok(BRANCH_W)] * 3 + [
            _layer_spec(a, layer) for a in (wbd, wbg, wbm, wout)],
        out_specs=tok(d),
        out_shape=jax.ShapeDtypeStruct((b, nt * tm, d), F32),
        compiler_params=pltpu.CompilerParams(
            dimension_semantics=("parallel", "arbitrary"), vmem_limit_bytes=VMEM_LIMIT),
        name="branch_merge",
    )(*tokens, modarr, g1, wg, bg, od, og, om, wbd, wbg, wbm, wout)


def _ffn_kernel(*refs, final):
    if final:
        x_ref, mod_ref, g2_ref, win_ref, wout_ref, gf_ref, o_ref = refs
    else:
        x_ref, mod_ref, g2_ref, win_ref, wout_ref, o_ref = refs
    x = x_ref[...]
    hidden = wout_ref.shape[0]
    hb = _adaln(x, g2_ref[...], mod_ref[3:4, :], mod_ref[4:5, :]).astype(BF16)
    g = _dot(hb, win_ref[:, :hidden])
    u = _dot(hb, win_ref[:, hidden:])
    a = (g * jax.nn.sigmoid(g) * u).astype(BF16)
    xn = x + mod_ref[5:6, :] * _dot(a, wout_ref[...])
    if final:
        xn = _rms(xn) * gf_ref[...]
    o_ref[...] = xn


def _ffn_call(layer, xin, modarr, g2, win, wout, gf, *, pos_off, n_ctx_tiles):
    b, t, d = xin.shape
    tm = TOKEN_TILE
    tok = pl.BlockSpec((None, tm, d), lambda bi, ti: (bi, ti, 0))
    final = gf is not None
    in_specs = [tok, _mod_spec(modarr, layer, pos_off, n_ctx_tiles)] + [
        _layer_spec(a, layer) for a in (g2, win, wout)]
    args = [xin, modarr, g2, win, wout]
    if final:
        in_specs.append(_const_spec((1, d)))
        args.append(gf)
    return pl.pallas_call(
        functools.partial(_ffn_kernel, final=final),
        grid=(b, t // tm),
        in_specs=in_specs,
        out_specs=tok,
        out_shape=jax.ShapeDtypeStruct((b, t, d), F32),
        compiler_params=pltpu.CompilerParams(
            dimension_semantics=("parallel", "arbitrary"), vmem_limit_bytes=VMEM_LIMIT),
        name="swiglu",
    )(*args)


def kernel(x, c, ctx, c_ctx, w_mod, b_mod, g_norm1, w_in, b_gate, lam_q1, lam_k1, lam_q2, lam_k2, g_diff_out, g_gqa_q, g_gqa_k, g_mla_q, w_mla_uq, g_mla_kv, w_mla_ukv, w_br_diff, w_br_gqa, w_br_mla, w_out, g_norm2, w_ffn_in, w_ffn_out, g_final):
    b, seq, d = x.shape
    n_ctx = ctx.shape[1]
    depth = w_in.shape[0]
    assert n_ctx % TOKEN_TILE == 0 and seq % TOKEN_TILE == 0 and seq % GRID_W == 0
    n_ctx_tiles = n_ctx // TOKEN_TILE

    mod_rows = 32
    cc = jnp.zeros((mod_rows, d), F32).at[:b].set(c).at[b].set(c_ctx)
    mods = _mod_call(cc, w_mod, b_mod)
    mod_lat = mods[:, :b].reshape(depth, b, N_MOD, d)
    mod_ctx = jnp.broadcast_to(mods[:, b].reshape(depth, 1, N_MOD, d), (depth, b, N_MOD, d))
    modarr = jnp.stack([mod_ctx, mod_lat], axis=2)
    tables = _rope_tables(seq, n_ctx)

    lam_inits = [_lambda_init(l) for l in range(depth)]
    lam = (jnp.exp(jnp.sum(lam_q1 * lam_k1, axis=-1)) - jnp.exp(jnp.sum(lam_q2 * lam_k2, axis=-1))
           + jnp.asarray(lam_inits, F32))
    lam_vec = jnp.broadcast_to(lam.reshape(depth, 1, 1), (depth, 1, LANES))

    def row(a):
        return a.reshape(depth, 1, -1)

    w1, wg, wuq, wukv, wbg = _prep_weights(w_in, w_mla_uq, w_mla_ukv, w_br_gqa)
    wbd, wbm, wout = w_br_diff.astype(BF16), w_br_mla.astype(BF16), w_out.astype(BF16)
    wfi, wfo = w_ffn_in.astype(BF16), w_ffn_out.astype(BF16)
    g1, g2, bg = row(g_norm1), row(g_norm2), row(b_gate)
    gq_g, gk_g = row(jnp.tile(g_gqa_q, (1, 2))), row(jnp.tile(g_gqa_k, (1, 2)))
    gmq, gmkv, gdo = row(g_mla_q), row(g_mla_kv), row(g_diff_out)

    tokens = (ctx, x)
    for l in range(depth):
        last = l == depth - 1
        dq, dk, dv, gq, gk, gv, mq, mk, mv = _proj_call(
            l, tokens, modarr, g1, w1, wuq, wukv, gq_g, gk_g, gmq, gmkv, tables, n_ctx_tiles)

        attn = functools.partial(_attn_call, n_ctx=n_ctx, lam_init=lam_inits[l], include_ctx=not last)
        od = attn("diff", l, dq, dk, dv, (lam_vec, gdo))
        og = attn("gqa", l, gq, gk, gv, ())
        om = attn("mla", l, mq, mk, mv, ())

        x_off = n_ctx_tiles if last else 0
        xmid = _merge_call(l, tokens, modarr, g1, wg, bg, od, og, om, wbd, wbg, wbm, wout,
                           x_off=x_off, n_ctx_tiles=n_ctx_tiles)
        xout = _ffn_call(l, xmid, modarr, g2, wfi, wfo, g_final.reshape(1, d) if last else None,
                         pos_off=x_off, n_ctx_tiles=n_ctx_tiles)
        tokens = (xout,)
    return xout
```

```python
import functools
import math

import jax
import jax.numpy as jnp
from jax import lax
from jax.experimental import pallas as pl
from jax.experimental.pallas import tpu as pltpu

F32 = jnp.float32
BF16 = jnp.bfloat16

GRID_W = 64
ROPE_THETA = 10000.0
EPS = 1e-6
N_MOD = 6

DIFF_HEADS = 4
HEAD_DIM = 64
GQA_KV_HEADS = 2
GQA_GROUP = 4
MLA_HEADS = 8
MLA_Q_LORA = 384
MLA_KV_LORA = 256
MLA_NOPE = 64
MLA_ROPE = 32
MLA_QK = MLA_NOPE + MLA_ROPE
MLA_V = 64
ROT64 = HEAD_DIM // 4
ROT32 = MLA_ROPE // 4
BRANCH_W = 512
N_UNITS = 4

LANES = 128
UNITS_PER_STEP = 4
KEY_CHUNK = 256
SUM_ROWS = 16
LOG2E = math.log2(math.e)
TOKEN_TILE = 256
VMEM_LIMIT = 56 * 1024 * 1024

OFF_DQ, OFF_DK, OFF_DV, OFF_GQ, OFF_GK, OFF_GV = 0, 512, 1024, 1536, 2048, 2176
OFF_MCQ, OFF_MCKV, OFF_MKR, W1_WIDTH = 2304, 2688, 2944, 3072


def _lambda_init(layer):
    return 0.8 - 0.6 * math.exp(-0.3 * layer)


_IN_GQ, _IN_GK, _IN_MKR, _IN_GATE = 1536, 2048, 2944, 2976


def _pair_heads(a, axis):
    shape = a.shape
    a = a.reshape(shape[:axis] + (GQA_KV_HEADS, GQA_GROUP, HEAD_DIM) + shape[axis + 1:])
    return jnp.swapaxes(a, axis, axis + 1).reshape(shape)


def _prep_weights(w_in, w_mla_uq, w_mla_ukv, w_br_gqa):
    depth = w_in.shape[0]
    mkr = jnp.pad(w_in[:, :, _IN_MKR:_IN_GATE], ((0, 0), (0, 0), (MLA_NOPE, LANES - MLA_QK)))
    w1 = jnp.concatenate([w_in[:, :, :_IN_GQ], _pair_heads(w_in[:, :, _IN_GQ:_IN_GK], 2),
                          w_in[:, :, _IN_GK:_IN_MKR], mkr], axis=2).astype(BF16)
    wg = w_in[:, :, _IN_GATE:].astype(BF16)
    uq = w_mla_uq.reshape(depth, MLA_Q_LORA, MLA_HEADS, MLA_QK)
    uq = jnp.pad(uq, ((0, 0), (0, 0), (0, 0), (0, LANES - MLA_QK)))
    uq = uq.reshape(depth, MLA_Q_LORA, MLA_HEADS * LANES).astype(BF16)
    ukv = w_mla_ukv.reshape(depth, MLA_KV_LORA, MLA_HEADS, MLA_NOPE + MLA_V)
    ukn = jnp.pad(ukv[..., :MLA_NOPE], ((0, 0), (0, 0), (0, 0), (0, LANES - MLA_NOPE)))
    ukn = ukn.reshape(depth, MLA_KV_LORA, -1)
    uv = ukv[..., MLA_NOPE:].reshape(depth, MLA_KV_LORA, -1)
    ukv2 = jnp.concatenate([ukn, uv], axis=2).astype(BF16)
    wbg = _pair_heads(w_br_gqa, 1).astype(BF16)
    return w1, wg, uq, ukv2, wbg


def _rope_tables(seq, n_ctx):
    rows = seq // GRID_W
    row = jnp.repeat(jnp.arange(rows, dtype=jnp.int32), GRID_W).astype(F32)
    col = jnp.tile(jnp.arange(GRID_W, dtype=jnp.int32), rows).astype(F32)

    def table(dim):
        a = dim // 2
        freqs = ROPE_THETA ** (-jnp.arange(0, a, 2, dtype=F32) / a)
        ang_r, ang_c = row[:, None] * freqs, col[:, None] * freqs
        cos = jnp.concatenate([jnp.cos(ang_r), jnp.cos(ang_r), jnp.cos(ang_c), jnp.cos(ang_c)], axis=-1)
        sin = jnp.concatenate([-jnp.sin(ang_r), jnp.sin(ang_r), -jnp.sin(ang_c), jnp.sin(ang_c)], axis=-1)
        return cos, sin

    c64, s64 = table(HEAD_DIM)
    c64, s64 = jnp.tile(c64, (1, 2)), jnp.tile(s64, (1, 2))
    c32, s32 = table(MLA_ROPE)
    pad = ((0, 0), (MLA_NOPE, LANES - MLA_QK))
    c32 = jnp.pad(c32, pad, constant_values=1.0)
    s32 = jnp.pad(s32, pad)
    ctx_pad = ((n_ctx, 0), (0, 0))
    return (jnp.pad(c64, ctx_pad, constant_values=1.0), jnp.pad(s64, ctx_pad),
            jnp.pad(c32, ctx_pad, constant_values=1.0), jnp.pad(s32, ctx_pad))


def _rms(x):
    return x * lax.rsqrt(jnp.mean(x * x, axis=-1, keepdims=True) + EPS)


def _adaln(x, g, shift, scale):
    return (_rms(x) * g) * (1.0 + scale) + shift


def _rope(z, cos, sin, is_x1, shift):
    zb = z.astype(BF16)
    partner = jnp.where(is_x1, pltpu.roll(zb, LANES - shift, 1), pltpu.roll(zb, shift, 1))
    return z * cos + partner.astype(F32) * sin


def _half_rms(z, lo, g):
    sq = z * z
    zero = jnp.zeros_like(sq)
    s_lo = jnp.sum(jnp.where(lo, sq, zero), axis=-1, keepdims=True)
    s_hi = jnp.sum(jnp.where(lo, zero, sq), axis=-1, keepdims=True)
    ms = jnp.where(lo, s_lo, s_hi) * (1.0 / HEAD_DIM)
    return z * lax.rsqrt(ms + EPS) * g


def _dot(a, b):
    return jnp.dot(a, b, preferred_element_type=F32)


def _const_spec(shape):
    return pl.BlockSpec(shape, lambda *_: (0,) * len(shape))


def _layer_spec(stacked, layer):
    shape = stacked.shape[1:]
    return pl.BlockSpec((None,) + shape, lambda *_: (layer,) + (0,) * len(shape))


def _mod_spec(modarr, layer, pos_off, n_ctx_tiles):
    return pl.BlockSpec((None, None, None) + modarr.shape[3:],
                        lambda bi, ti: (layer, bi, (ti + pos_off >= n_ctx_tiles).astype(jnp.int32), 0, 0))


def _token_specs(tokens, tile_off, n_ctx_tiles):
    tm, d = TOKEN_TILE, tokens[0].shape[-1]
    if len(tokens) == 1:
        return [pl.BlockSpec((None, tm, d), lambda bi, ti: (bi, ti + tile_off, 0))]
    assert tile_off == 0
    return [pl.BlockSpec((None, tm, d), lambda bi, ti: (bi, jnp.minimum(ti, n_ctx_tiles - 1), 0)),
            pl.BlockSpec((None, tm, d), lambda bi, ti: (bi, jnp.maximum(ti - n_ctx_tiles, 0), 0))]


def _read_tokens(tok_refs, n_ctx_tiles):
    if len(tok_refs) == 1:
        return tok_refs[0][...]
    ctx_ref, x_ref = tok_refs
    return jnp.where(pl.program_id(1) < n_ctx_tiles, ctx_ref[...], x_ref[...])


def _mod_kernel(c_ref, w_ref, b_ref, o_ref):
    cc = c_ref[...]
    a = (cc * jax.nn.sigmoid(cc)).astype(BF16)
    o_ref[...] = _dot(a, w_ref[...].astype(BF16)) + b_ref[...]


def _mod_call(cc, w_mod, b_mod):
    depth, d, n = w_mod.shape
    tn = 1536
    rows = cc.shape[0]
    return pl.pallas_call(
        _mod_kernel,
        grid=(depth, n // tn),
        in_specs=[
            pl.BlockSpec((rows, d), lambda l, j: (0, 0)),
            pl.BlockSpec((None, d, tn), lambda l, j: (l, 0, j)),
            pl.BlockSpec((None, 1, tn), lambda l, j: (l, 0, j)),
        ],
        out_specs=pl.BlockSpec((None, rows, tn), lambda l, j: (l, 0, j)),
        out_shape=jax.ShapeDtypeStruct((depth, rows, n), F32),
        compiler_params=pltpu.CompilerParams(vmem_limit_bytes=VMEM_LIMIT),
        name="mod_vectors",
    )(cc, w_mod, b_mod.reshape(depth, 1, n))


def _proj_kernel(*refs, n_tok, n_ctx_tiles):
    (mod_ref, g1_ref, w1_ref, wuq_ref, wukv_ref, gq_g_ref, gk_g_ref, gmq_ref, gmkv_ref,
     c64_ref, s64_ref, c32_ref, s32_ref,
     dq_o, dk_o, dv_o, gq_o, gk_o, gv_o, mq_o, mk_o, mv_o) = refs[n_tok:]
    x = _read_tokens(refs[:n_tok], n_ctx_tiles)
    hb = _adaln(x, g1_ref[...], mod_ref[0:1, :], mod_ref[1:2, :]).astype(BF16)
    tm = x.shape[0]
    lane = lax.broadcasted_iota(jnp.int32, (tm, LANES), 1)
    lo = lane < HEAD_DIM
    x1_64 = (lane & ROT64) == 0
    x1_32 = (lane & ROT32) == 0
    c64, s64 = c64_ref[...], s64_ref[...]
    c32, s32 = c32_ref[...], s32_ref[...]
    qk_scale = HEAD_DIM ** -0.5 * LOG2E

    def seg(off, width):
        return _dot(hb, w1_ref[:, off:off + width])

    def blocks(z):
        return [(slice(i * LANES, (i + 1) * LANES), z[:, i * LANES:(i + 1) * LANES]) for i in range(z.shape[1] // LANES)]

    zm = seg(OFF_MCQ, W1_WIDTH - OFF_MCQ)
    mcq = (_rms(zm[:, :MLA_Q_LORA]) * gmq_ref[...]).astype(BF16)
    mckv = (_rms(zm[:, MLA_Q_LORA:MLA_Q_LORA + MLA_KV_LORA]) * gmkv_ref[...]).astype(BF16)
    mkr = _rope(zm[:, MLA_Q_LORA + MLA_KV_LORA:], c32, s32, x1_32, ROT32)

    for sl, z in blocks(seg(OFF_DQ, BRANCH_W)):
        dq_o[:, sl] = (_rope(z, c64, s64, x1_64, ROT64) * qk_scale).astype(BF16)
    for sl, z in blocks(seg(OFF_DK, BRANCH_W)):
        dk_o[:, sl] = _rope(z, c64, s64, x1_64, ROT64).astype(BF16)
    dv_o[...] = seg(OFF_DV, BRANCH_W).astype(BF16)
    for sl, z in blocks(seg(OFF_GQ, BRANCH_W)):
        gq = _half_rms(z, lo, gq_g_ref[...])
        gq_o[:, sl] = (_rope(gq, c64, s64, x1_64, ROT64) * qk_scale).astype(BF16)
    gkv = seg(OFF_GK, 2 * LANES)
    gk = _half_rms(gkv[:, :LANES], lo, gk_g_ref[...])
    gk_o[...] = _rope(gk, c64, s64, x1_64, ROT64).astype(BF16)
    gv_o[...] = gkv[:, LANES:].astype(BF16)

    mla_scale = MLA_QK ** -0.5 * LOG2E
    for sl, z in blocks(_dot(mcq, wuq_ref[...])):
        mq_o[:, sl] = (_rope(z, c32, s32, x1_32, ROT32) * mla_scale).astype(BF16)
    mkv = _dot(mckv, wukv_ref[...])
    for sl, z in blocks(mkv[:, :MLA_HEADS * LANES]):
        mk_o[:, sl] = (z + mkr).astype(BF16)
    mv_o[...] = mkv[:, MLA_HEADS * LANES:].astype(BF16)


def _proj_call(layer, tokens, modarr, g1, w1, wuq, wukv, gq_g, gk_g, gmq, gmkv, tables, n_ctx_tiles):
    b = tokens[0].shape[0]
    t = sum(a.shape[1] for a in tokens)
    tm = TOKEN_TILE
    tok = lambda w: pl.BlockSpec((None, tm, w), lambda bi, ti: (bi, ti, 0))
    tab = pl.BlockSpec((tm, LANES), lambda bi, ti: (ti, 0))
    widths = (BRANCH_W, BRANCH_W, BRANCH_W, BRANCH_W, LANES, LANES, MLA_HEADS * LANES, MLA_HEADS * LANES, BRANCH_W)
    return pl.pallas_call(
        functools.partial(_proj_kernel, n_tok=len(tokens), n_ctx_tiles=n_ctx_tiles),
        grid=(b, t // tm),
        in_specs=_token_specs(tokens, 0, n_ctx_tiles) + [_mod_spec(modarr, layer, 0, n_ctx_tiles)] + [
            _layer_spec(a, layer) for a in (g1, w1, wuq, wukv, gq_g, gk_g, gmq, gmkv)] + [tab, tab, tab, tab],
        out_specs=[tok(w) for w in widths],
        out_shape=[jax.ShapeDtypeStruct((b, t, w), BF16) for w in widths],
        compiler_params=pltpu.CompilerParams(
            dimension_semantics=("parallel", "arbitrary"), vmem_limit_bytes=VMEM_LIMIT),
        name="input_projection",
    )(*tokens, modarr, g1, w1, wuq, wukv, gq_g, gk_g, gmq, gmkv, *tables)


def _attn_kernel(*refs, kind, n_ctx, lam_init, first_tile_is_ctx, upb):
    if kind == "diff":
        q_ref, k_ref, v_ref, lam_ref, gdo_ref, o_ref, vt_ref = refs
    else:
        q_ref, k_ref, v_ref, o_ref, vt_ref = refs
    tq = q_ref.shape[0]
    lane = lax.broadcasted_iota(jnp.int32, (tq, LANES), 1)
    lo = lane < HEAD_DIM
    n_keys = k_ref.shape[0]
    n_kv = vt_ref.shape[0]
    qw = q_ref.shape[1] // upb
    kw = k_ref.shape[1] // n_kv

    @pl.when(pl.program_id(2) == 0)
    def _():
        for i in range(n_kv):
            vt_ref[i, 0:LANES, :] = v_ref[:, i * LANES:(i + 1) * LANES].astype(F32).T.astype(BF16)
            vt_ref[i, LANES:, :] = jnp.ones((SUM_ROWS, n_keys), BF16)

    def key_chunks(nk):
        step = KEY_CHUNK if nk % KEY_CHUNK == 0 else nk
        return [(c, c + step) for c in range(0, nk, step)]

    def unit_queries(u):
        q = q_ref[:, u * qw:(u + 1) * qw]
        if kind == "mla":
            return q[:, :LANES], q[:, LANES:]
        zero = jnp.zeros_like(q)
        return jnp.where(lo, q, zero), jnp.where(lo, zero, q)

    def chunk_scores(u, qa, qb, c0, c1):
        kv = u % n_kv
        k = k_ref[c0:c1, kv * kw:(kv + 1) * kw]
        ka, kb = (k[:, :LANES], k[:, LANES:]) if kind == "mla" else (k, k)
        nt = (((1,), (1,)), ((), ()))
        return (lax.dot_general(ka, qa, nt, preferred_element_type=F32),
                lax.dot_general(kb, qb, nt, preferred_element_type=F32))

    def chunk_softmax_pv(u, c0, c1, st):
        m = jnp.max(st, axis=0, keepdims=True)
        p = jnp.exp2((st - m).astype(BF16))
        return m, _dot(vt_ref[u % n_kv, :, c0:c1], p)

    def merge_chunks(parts):
        if len(parts) == 1:
            ot = parts[0][1]
        else:
            m_all = functools.reduce(jnp.maximum, [m for m, _ in parts])
            ot = sum(acc * jnp.exp2(m - m_all) for m, acc in parts)
        return ot[0:LANES] / ot[LANES:LANES + 1]

    def write_unit(u, oa, ob):
        if kind == "diff":
            o = (oa - lam_ref[0:1, 0:1] * ob).T
            o = _rms(o) * gdo_ref[...] * (1.0 - lam_init)
        else:
            row = lax.broadcasted_iota(jnp.int32, oa.shape, 0)
            o = jnp.where(row < HEAD_DIM, oa, ob).T
        o_ref[:, u * LANES:(u + 1) * LANES] = o.astype(BF16)

    def compute(nk):
        chunks = key_chunks(nk)
        queries = [unit_queries(u) for u in range(upb)]
        sa, sb = chunk_scores(0, *queries[0], 0, nk)
        cur = [(sa[c0:c1], sb[c0:c1]) for c0, c1 in chunks]
        for u in range(upb):
            nxt, parts_a, parts_b = [], [], []
            for ci, (c0, c1) in enumerate(chunks):
                if u + 1 < upb:
                    nxt.append(chunk_scores(u + 1, *queries[u + 1], c0, c1))
                parts_a.append(chunk_softmax_pv(u, c0, c1, cur[ci][0]))
                parts_b.append(chunk_softmax_pv(u, c0, c1, cur[ci][1]))
            write_unit(u, merge_chunks(parts_a), merge_chunks(parts_b))
            cur = nxt

    if first_tile_is_ctx:
        j = pl.program_id(2)
        pl.when(j == 0)(lambda: compute(n_ctx))
        pl.when(j > 0)(lambda: compute(n_keys))
    else:
        compute(n_keys)


def _attn_call(kind, layer, q, k, v, extra, *, n_ctx, lam_init, include_ctx):
    b, t, _ = k.shape
    tq = TOKEN_TILE
    upb = UNITS_PER_STEP
    n_ctx_tiles = n_ctx // tq
    q_off = 0 if include_ctx else n_ctx_tiles
    nq = t // tq - q_off
    qw = q.shape[-1] // N_UNITS
    shared_kv = kind == "gqa"
    n_kv = 1 if shared_kv else upb
    kw = LANES if shared_kv else k.shape[-1] // N_UNITS
    kv_map = (lambda bi, g, j: (bi, 0, 0)) if shared_kv else (lambda bi, g, j: (bi, 0, g))
    in_specs = [
        pl.BlockSpec((None, tq, upb * qw), lambda bi, g, j: (bi, j + q_off, g)),
        pl.BlockSpec((None, t, n_kv * kw), kv_map),
        pl.BlockSpec((None, t, n_kv * LANES), kv_map),
    ]
    in_specs += [_layer_spec(a, layer) for a in extra]
    return pl.pallas_call(
        functools.partial(_attn_kernel, kind=kind, n_ctx=n_ctx, lam_init=lam_init,
                          first_tile_is_ctx=include_ctx, upb=upb),
        grid=(b, N_UNITS // upb, nq),
        in_specs=in_specs,
        out_specs=pl.BlockSpec((None, tq, upb * LANES), lambda bi, g, j: (bi, j, g)),
        out_shape=jax.ShapeDtypeStruct((b, nq * tq, BRANCH_W), BF16),
        scratch_shapes=[pltpu.VMEM((n_kv, LANES + SUM_ROWS, t), BF16)],
        compiler_params=pltpu.CompilerParams(
            dimension_semantics=("parallel", "arbitrary", "arbitrary"), vmem_limit_bytes=VMEM_LIMIT),
        name="attention_" + kind,
    )(q, k, v, *extra)


def _merge_kernel(*refs, n_tok, n_ctx_tiles):
    (mod_ref, g1_ref, wg_ref, bg_ref, od_ref, og_ref, om_ref,
     wbd_ref, wbg_ref, wbm_ref, wout_ref, xo_ref) = refs[n_tok:]
    x = _read_tokens(refs[:n_tok], n_ctx_tiles)
    d = x.shape[-1]
    hb = _adaln(x, g1_ref[...], mod_ref[0:1, :], mod_ref[1:2, :]).astype(BF16)
    y = None
    for i, (o_ref, wb_ref) in enumerate(((od_ref, wbd_ref), (og_ref, wbg_ref), (om_ref, wbm_ref))):
        sl = slice(i * d, (i + 1) * d)
        gate = jax.nn.sigmoid(_dot(hb, wg_ref[:, sl]) + bg_ref[:, sl])
        term = gate * _dot(o_ref[...], wb_ref[...])
        y = term if y is None else y + term
    out = _dot(y.astype(BF16), wout_ref[...])
    xo_ref[...] = x + mod_ref[2:3, :] * out


def _merge_call(layer, tokens, modarr, g1, wg, bg, od, og, om, wbd, wbg, wbm, wout, *, x_off, n_ctx_tiles):
    b, d = tokens[0].shape[0], tokens[0].shape[-1]
    tm = TOKEN_TILE
    nt = od.shape[1] // tm
    tok = lambda w: pl.BlockSpec((None, tm, w), lambda bi, ti: (bi, ti, 0))
    return pl.pallas_call(
        functools.partial(_merge_kernel, n_tok=len(tokens), n_ctx_tiles=n_ctx_tiles),
        grid=(b, nt),
        in_specs=_token_specs(tokens, x_off, n_ctx_tiles) + [_mod_spec(modarr, layer, x_off, n_ctx_tiles)] + [
            _layer_spec(a, layer) for a in (g1, wg, bg)] + [tok(BRANCH_W)] * 3 + [
            _layer_spec(a, layer) for a in (wbd, wbg, wbm, wout)],
        out_specs=tok(d),
        out_shape=jax.ShapeDtypeStruct((b, nt * tm, d), F32),
        compiler_params=pltpu.CompilerParams(
            dimension_semantics=("parallel", "arbitrary"), vmem_limit_bytes=VMEM_LIMIT),
        name="branch_merge",
    )(*tokens, modarr, g1, wg, bg, od, og, om, wbd, wbg, wbm, wout)


def _ffn_kernel(*refs, final):
    if final:
        x_ref, mod_ref, g2_ref, win_ref, wout_ref, gf_ref, o_ref = refs
    else:
        x_ref, mod_ref, g2_ref, win_ref, wout_ref, o_ref = refs
    x = x_ref[...]
    hidden = wout_ref.shape[0]
    hb = _adaln(x, g2_ref[...], mod_ref[3:4, :], mod_ref[4:5, :]).astype(BF16)
    g = _dot(hb, win_ref[:, :hidden])
    u = _dot(hb, win_ref[:, hidden:])
    a = (g * jax.nn.sigmoid(g) * u).astype(BF16)
    xn = x + mod_ref[5:6, :] * _dot(a, wout_ref[...])
    if final:
        xn = _rms(xn) * gf_ref[...]
    o_ref[...] = xn


def _ffn_call(layer, xin, modarr, g2, win, wout, gf, *, pos_off, n_ctx_tiles):
    b, t, d = xin.shape
    tm = TOKEN_TILE
    tok = pl.BlockSpec((None, tm, d), lambda bi, ti: (bi, ti, 0))
    final = gf is not None
    in_specs = [tok, _mod_spec(modarr, layer, pos_off, n_ctx_tiles)] + [
        _layer_spec(a, layer) for a in (g2, win, wout)]
    args = [xin, modarr, g2, win, wout]
    if final:
        in_specs.append(_const_spec((1, d)))
        args.append(gf)
    return pl.pallas_call(
        functools.partial(_ffn_kernel, final=final),
        grid=(b, t // tm),
        in_specs=in_specs,
        out_specs=tok,
        out_shape=jax.ShapeDtypeStruct((b, t, d), F32),
        compiler_params=pltpu.CompilerParams(
            dimension_semantics=("parallel", "arbitrary"), vmem_limit_bytes=VMEM_LIMIT),
        name="swiglu",
    )(*args)


def kernel(x, c, ctx, c_ctx, w_mod, b_mod, g_norm1, w_in, b_gate, lam_q1, lam_k1, lam_q2, lam_k2, g_diff_out, g_gqa_q, g_gqa_k, g_mla_q, w_mla_uq, g_mla_kv, w_mla_ukv, w_br_diff, w_br_gqa, w_br_mla, w_out, g_norm2, w_ffn_in, w_ffn_out, g_final):
    b, seq, d = x.shape
    n_ctx = ctx.shape[1]
    depth = w_in.shape[0]
    assert n_ctx % TOKEN_TILE == 0 and seq % TOKEN_TILE == 0 and seq % GRID_W == 0
    n_ctx_tiles = n_ctx // TOKEN_TILE

    mod_rows = 32
    cc = jnp.zeros((mod_rows, d), F32).at[:b].set(c).at[b].set(c_ctx)
    mods = _mod_call(cc, w_mod, b_mod)
    mod_lat = mods[:, :b].reshape(depth, b, N_MOD, d)
    mod_ctx = jnp.broadcast_to(mods[:, b].reshape(depth, 1, N_MOD, d), (depth, b, N_MOD, d))
    modarr = jnp.stack([mod_ctx, mod_lat], axis=2)
    tables = _rope_tables(seq, n_ctx)

    lam_inits = [_lambda_init(l) for l in range(depth)]
    lam = (jnp.exp(jnp.sum(lam_q1 * lam_k1, axis=-1)) - jnp.exp(jnp.sum(lam_q2 * lam_k2, axis=-1))
           + jnp.asarray(lam_inits, F32))
    lam_vec = jnp.broadcast_to(lam.reshape(depth, 1, 1), (depth, 1, LANES))

    def row(a):
        return a.reshape(depth, 1, -1)

    w1, wg, wuq, wukv, wbg = _prep_weights(w_in, w_mla_uq, w_mla_ukv, w_br_gqa)
    wbd, wbm, wout = w_br_diff.astype(BF16), w_br_mla.astype(BF16), w_out.astype(BF16)
    wfi, wfo = w_ffn_in.astype(BF16), w_ffn_out.astype(BF16)
    g1, g2, bg = row(g_norm1), row(g_norm2), row(b_gate)
    gq_g, gk_g = row(jnp.tile(g_gqa_q, (1, 2))), row(jnp.tile(g_gqa_k, (1, 2)))
    gmq, gmkv, gdo = row(g_mla_q), row(g_mla_kv), row(g_diff_out)

    tokens = (ctx, x)
    for l in range(depth):
        last = l == depth - 1
        dq, dk, dv, gq, gk, gv, mq, mk, mv = _proj_call(
            l, tokens, modarr, g1, w1, wuq, wukv, gq_g, gk_g, gmq, gmkv, tables, n_ctx_tiles)

        attn = functools.partial(_attn_call, n_ctx=n_ctx, lam_init=lam_inits[l], include_ctx=not last)
        od = attn("diff", l, dq, dk, dv, (lam_vec, gdo))
        og = attn("gqa", l, gq, gk, gv, ())
        om = attn("mla", l, mq, mk, mv, ())

        x_off = n_ctx_tiles if last else 0
        xmid = _merge_call(l, tokens, modarr, g1, wg, bg, od, og, om, wbd, wbg, wbm, wout,
                           x_off=x_off, n_ctx_tiles=n_ctx_tiles)
        xout = _ffn_call(l, xmid, modarr, g2, wfi, wfo, g_final.reshape(1, d) if last else None,
                         pos_off=x_off, n_ctx_tiles=n_ctx_tiles)
        tokens = (xout,)
    return xout
```

```python
import functools
import math

import jax
import jax.numpy as jnp
from jax import lax
from jax.experimental import pallas as pl
from jax.experimental.pallas import tpu as pltpu

F32 = jnp.float32
BF16 = jnp.bfloat16

GRID_W = 64
ROPE_THETA = 10000.0
EPS = 1e-6
N_MOD = 6

DIFF_HEADS = 4
HEAD_DIM = 64
GQA_KV_HEADS = 2
GQA_GROUP = 4
MLA_HEADS = 8
MLA_Q_LORA = 384
MLA_KV_LORA = 256
MLA_NOPE = 64
MLA_ROPE = 32
MLA_QK = MLA_NOPE + MLA_ROPE
MLA_V = 64
ROT64 = HEAD_DIM // 4
ROT32 = MLA_ROPE // 4
BRANCH_W = 512
N_UNITS = 4

LANES = 128
UNITS_PER_STEP = 4
KEY_CHUNK = 256
SUM_ROWS = 16
LOG2E = math.log2(math.e)
TOKEN_TILE = 256
VMEM_LIMIT = 56 * 1024 * 1024

OFF_DQ, OFF_DK, OFF_DV, OFF_GQ, OFF_GK, OFF_GV = 0, 512, 1024, 1536, 2048, 2176
OFF_MCQ, OFF_MCKV, OFF_MKR, W1_WIDTH = 2304, 2688, 2944, 3072


def _lambda_init(layer):
    return 0.8 - 0.6 * math.exp(-0.3 * layer)


_IN_GQ, _IN_GK, _IN_MKR, _IN_GATE = 1536, 2048, 2944, 2976


def _pair_heads(a, axis):
    shape = a.shape
    a = a.reshape(shape[:axis] + (GQA_KV_HEADS, GQA_GROUP, HEAD_DIM) + shape[axis + 1:])
    return jnp.swapaxes(a, axis, axis + 1).reshape(shape)


def _prep_weights(w_in, w_mla_uq, w_mla_ukv, w_br_gqa):
    depth = w_in.shape[0]
    mkr = jnp.pad(w_in[:, :, _IN_MKR:_IN_GATE], ((0, 0), (0, 0), (MLA_NOPE, LANES - MLA_QK)))
    w1 = jnp.concatenate([w_in[:, :, :_IN_GQ], _pair_heads(w_in[:, :, _IN_GQ:_IN_GK], 2),
                          w_in[:, :, _IN_GK:_IN_MKR], mkr], axis=2).astype(BF16)
    wg = w_in[:, :, _IN_GATE:].astype(BF16)
    uq = w_mla_uq.reshape(depth, MLA_Q_LORA, MLA_HEADS, MLA_QK)
    uq = jnp.pad(uq, ((0, 0), (0, 0), (0, 0), (0, LANES - MLA_QK)))
    uq = uq.reshape(depth, MLA_Q_LORA, MLA_HEADS * LANES).astype(BF16)
    ukv = w_mla_ukv.reshape(depth, MLA_KV_LORA, MLA_HEADS, MLA_NOPE + MLA_V)
    ukn = jnp.pad(ukv[..., :MLA_NOPE], ((0, 0), (0, 0), (0, 0), (0, LANES - MLA_NOPE)))
    ukn = ukn.reshape(depth, MLA_KV_LORA, -1)
    uv = ukv[..., MLA_NOPE:].reshape(depth, MLA_KV_LORA, -1)
    ukv2 = jnp.concatenate([ukn, uv], axis=2).astype(BF16)
    wbg = _pair_heads(w_br_gqa, 1).astype(BF16)
    return w1, wg, uq, ukv2, wbg


def _rope_tables(seq, n_ctx):
    rows = seq // GRID_W
    row = jnp.repeat(jnp.arange(rows, dtype=jnp.int32), GRID_W).astype(F32)
    col = jnp.tile(jnp.arange(GRID_W, dtype=jnp.int32), rows).astype(F32)

    def table(dim):
        a = dim // 2
        freqs = ROPE_THETA ** (-jnp.arange(0, a, 2, dtype=F32) / a)
        ang_r, ang_c = row[:, None] * freqs, col[:, None] * freqs
        cos = jnp.concatenate([jnp.cos(ang_r), jnp.cos(ang_r), jnp.cos(ang_c), jnp.cos(ang_c)], axis=-1)
        sin = jnp.concatenate([-jnp.sin(ang_r), jnp.sin(ang_r), -jnp.sin(ang_c), jnp.sin(ang_c)], axis=-1)
        return cos, sin

    c64, s64 = table(HEAD_DIM)
    c64, s64 = jnp.tile(c64, (1, 2)), jnp.tile(s64, (1, 2))
    c32, s32 = table(MLA_ROPE)
    pad = ((0, 0), (MLA_NOPE, LANES - MLA_QK))
    c32 = jnp.pad(c32, pad, constant_values=1.0)
    s32 = jnp.pad(s32, pad)
    ctx_pad = ((n_ctx, 0), (0, 0))
    return (jnp.pad(c64, ctx_pad, constant_values=1.0), jnp.pad(s64, ctx_pad),
            jnp.pad(c32, ctx_pad, constant_values=1.0), jnp.pad(s32, ctx_pad))


def _rms(x):
    return x * lax.rsqrt(jnp.mean(x * x, axis=-1, keepdims=True) + EPS)


def _adaln(x, g, shift, scale):
    return (_rms(x) * g) * (1.0 + scale) + shift


def _rope(z, cos, sin, is_x1, shift):
    zb = z.astype(BF16)
    partner = jnp.where(is_x1, pltpu.roll(zb, LANES - shift, 1), pltpu.roll(zb, shift, 1))
    return z * cos + partner.astype(F32) * sin


def _half_rms(z, lo, g):
    sq = z * z
    zero = jnp.zeros_like(sq)
    s_lo = jnp.sum(jnp.where(lo, sq, zero), axis=-1, keepdims=True)
    s_hi = jnp.sum(jnp.where(lo, zero, sq), axis=-1, keepdims=True)
    ms = jnp.where(lo, s_lo, s_hi) * (1.0 / HEAD_DIM)
    return z * lax.rsqrt(ms + EPS) * g


def _dot(a, b):
    return jnp.dot(a, b, preferred_element_type=F32)


def _const_spec(shape):
    return pl.BlockSpec(shape, lambda *_: (0,) * len(shape))


def _layer_spec(stacked, layer):
    shape = stacked.shape[1:]
    return pl.BlockSpec((None,) + shape, lambda *_: (layer,) + (0,) * len(shape))


def _mod_spec(modarr, layer, pos_off, n_ctx_tiles):
    return pl.BlockSpec((None, None, None) + modarr.shape[3:],
                        lambda bi, ti: (layer, bi, (ti + pos_off >= n_ctx_tiles).astype(jnp.int32), 0, 0))


def _token_specs(tokens, tile_off, n_ctx_tiles):
    tm, d = TOKEN_TILE, tokens[0].shape[-1]
    if len(tokens) == 1:
        return [pl.BlockSpec((None, tm, d), lambda bi, ti: (bi, ti + tile_off, 0))]
    assert tile_off == 0
    return [pl.BlockSpec((None, tm, d), lambda bi, ti: (bi, jnp.minimum(ti, n_ctx_tiles - 1), 0)),
            pl.BlockSpec((None, tm, d), lambda bi, ti: (bi, jnp.maximum(ti - n_ctx_tiles, 0), 0))]


def _read_tokens(tok_refs, n_ctx_tiles):
    if len(tok_refs) == 1:
        return tok_refs[0][...]
    ctx_ref, x_ref = tok_refs
    return jnp.where(pl.program_id(1) < n_ctx_tiles, ctx_ref[...], x_ref[...])


def _mod_kernel(c_ref, w_ref, b_ref, o_ref):
    cc = c_ref[...]
    a = (cc * jax.nn.sigmoid(cc)).astype(BF16)
    o_ref[...] = _dot(a, w_ref[...].astype(BF16)) + b_ref[...]


def _mod_call(cc, w_mod, b_mod):
    depth, d, n = w_mod.shape
    tn = 1536
    rows = cc.shape[0]
    return pl.pallas_call(
        _mod_kernel,
        grid=(depth, n // tn),
        in_specs=[
            pl.BlockSpec((rows, d), lambda l, j: (0, 0)),
            pl.BlockSpec((None, d, tn), lambda l, j: (l, 0, j)),
            pl.BlockSpec((None, 1, tn), lambda l, j: (l, 0, j)),
        ],
        out_specs=pl.BlockSpec((None, rows, tn), lambda l, j: (l, 0, j)),
        out_shape=jax.ShapeDtypeStruct((depth, rows, n), F32),
        compiler_params=pltpu.CompilerParams(vmem_limit_bytes=VMEM_LIMIT),
        name="mod_vectors",
    )(cc, w_mod, b_mod.reshape(depth, 1, n))


def _proj_kernel(*refs, n_tok, n_ctx_tiles):
    (mod_ref, g1_ref, w1_ref, wuq_ref, wukv_ref, gq_g_ref, gk_g_ref, gmq_ref, gmkv_ref,
     c64_ref, s64_ref, c32_ref, s32_ref,
     dq_o, dk_o, dv_o, gq_o, gk_o, gv_o, mq_o, mk_o, mv_o) = refs[n_tok:]
    x = _read_tokens(refs[:n_tok], n_ctx_tiles)
    hb = _adaln(x, g1_ref[...], mod_ref[0:1, :], mod_ref[1:2, :]).astype(BF16)
    tm = x.shape[0]
    lane = lax.broadcasted_iota(jnp.int32, (tm, LANES), 1)
    lo = lane < HEAD_DIM
    x1_64 = (lane & ROT64) == 0
    x1_32 = (lane & ROT32) == 0
    c64, s64 = c64_ref[...], s64_ref[...]
    c32, s32 = c32_ref[...], s32_ref[...]
    qk_scale = HEAD_DIM ** -0.5 * LOG2E

    def seg(off, width):
        return _dot(hb, w1_ref[:, off:off + width])

    def blocks(z):
        return [(slice(i * LANES, (i + 1) * LANES), z[:, i * LANES:(i + 1) * LANES]) for i in range(z.shape[1] // LANES)]

    zm = seg(OFF_MCQ, W1_WIDTH - OFF_MCQ)
    mcq = (_rms(zm[:, :MLA_Q_LORA]) * gmq_ref[...]).astype(BF16)
    mckv = (_rms(zm[:, MLA_Q_LORA:MLA_Q_LORA + MLA_KV_LORA]) * gmkv_ref[...]).astype(BF16)
    mkr = _rope(zm[:, MLA_Q_LORA + MLA_KV_LORA:], c32, s32, x1_32, ROT32)

    for sl, z in blocks(seg(OFF_DQ, BRANCH_W)):
        dq_o[:, sl] = (_rope(z, c64, s64, x1_64, ROT64) * qk_scale).astype(BF16)
    for sl, z in blocks(seg(OFF_DK, BRANCH_W)):
        dk_o[:, sl] = _rope(z, c64, s64, x1_64, ROT64).astype(BF16)
    dv_o[...] = seg(OFF_DV, BRANCH_W).astype(BF16)
    for sl, z in blocks(seg(OFF_GQ, BRANCH_W)):
        gq = _half_rms(z, lo, gq_g_ref[...])
        gq_o[:, sl] = (_rope(gq, c64, s64, x1_64, ROT64) * qk_scale).astype(BF16)
    gkv = seg(OFF_GK, 2 * LANES)
    gk = _half_rms(gkv[:, :LANES], lo, gk_g_ref[...])
    gk_o[...] = _rope(gk, c64, s64, x1_64, ROT64).astype(BF16)
    gv_o[...] = gkv[:, LANES:].astype(BF16)

    mla_scale = MLA_QK ** -0.5 * LOG2E
    for sl, z in blocks(_dot(mcq, wuq_ref[...])):
        mq_o[:, sl] = (_rope(z, c32, s32, x1_32, ROT32) * mla_scale).astype(BF16)
    mkv = _dot(mckv, wukv_ref[...])
    for sl, z in blocks(mkv[:, :MLA_HEADS * LANES]):
        mk_o[:, sl] = (z + mkr).astype(BF16)
    mv_o[...] = mkv[:, MLA_HEADS * LANES:].astype(BF16)


def _proj_call(layer, tokens, modarr, g1, w1, wuq, wukv, gq_g, gk_g, gmq, gmkv, tables, n_ctx_tiles):
    b = tokens[0].shape[0]
    t = sum(a.shape[1] for a in tokens)
    tm = TOKEN_TILE
    tok = lambda w: pl.BlockSpec((None, tm, w), lambda bi, ti: (bi, ti, 0))
    tab = pl.BlockSpec((tm, LANES), lambda bi, ti: (ti, 0))
    widths = (BRANCH_W, BRANCH_W, BRANCH_W, BRANCH_W, LANES, LANES, MLA_HEADS * LANES, MLA_HEADS * LANES, BRANCH_W)
    return pl.pallas_call(
        functools.partial(_proj_kernel, n_tok=len(tokens), n_ctx_tiles=n_ctx_tiles),
        grid=(b, t // tm),
        in_specs=_token_specs(tokens, 0, n_ctx_tiles) + [_mod_spec(modarr, layer, 0, n_ctx_tiles)] + [
            _layer_spec(a, layer) for a in (g1, w1, wuq, wukv, gq_g, gk_g, gmq, gmkv)] + [tab, tab, tab, tab],
        out_specs=[tok(w) for w in widths],
        out_shape=[jax.ShapeDtypeStruct((b, t, w), BF16) for w in widths],
        compiler_params=pltpu.CompilerParams(
            dimension_semantics=("parallel", "arbitrary"), vmem_limit_bytes=VMEM_LIMIT),
        name="input_projection",
    )(*tokens, modarr, g1, w1, wuq, wukv, gq_g, gk_g, gmq, gmkv, *tables)


def _attn_kernel(*refs, kind, n_ctx, lam_init, first_tile_is_ctx, upb):
    if kind == "diff":
        q_ref, k_ref, v_ref, lam_ref, gdo_ref, o_ref, vt_ref = refs
    else:
        q_ref, k_ref, v_ref, o_ref, vt_ref = refs
    tq = q_ref.shape[0]
    lane = lax.broadcasted_iota(jnp.int32, (tq, LANES), 1)
    lo = lane < HEAD_DIM
    n_keys = k_ref.shape[0]
    n_kv = vt_ref.shape[0]
    qw = q_ref.shape[1] // upb
    kw = k_ref.shape[1] // n_kv

    @pl.when(pl.program_id(2) == 0)
    def _():
        for i in range(n_kv):
            vt_ref[i, 0:LANES, :] = v_ref[:, i * LANES:(i + 1) * LANES].astype(F32).T.astype(BF16)
            vt_ref[i, LANES:, :] = jnp.ones((SUM_ROWS, n_keys), BF16)

    def key_chunks(nk):
        step = KEY_CHUNK if nk % KEY_CHUNK == 0 else nk
        return [(c, c + step) for c in range(0, nk, step)]

    def unit_queries(u):
        q = q_ref[:, u * qw:(u + 1) * qw]
        if kind == "mla":
            return q[:, :LANES], q[:, LANES:]
        zero = jnp.zeros_like(q)
        return jnp.where(lo, q, zero), jnp.where(lo, zero, q)

    def map_scores(u, which, q, nk):
        kv = u % n_kv
        k = k_ref[0:nk, kv * kw:(kv + 1) * kw]
        if kind == "mla":
            k = k[:, :LANES] if which == 0 else k[:, LANES:]
        return lax.dot_general(k, q, (((1,), (1,)), ((), ())), preferred_element_type=F32)

    def chunk_softmax_pv(u, c0, c1, st):
        m = jnp.max(st, axis=0, keepdims=True)
        p = jnp.exp2((st - m).astype(BF16))
        return m, _dot(vt_ref[u % n_kv, :, c0:c1], p)

    def merge_chunks(parts):
        if len(parts) == 1:
            ot = parts[0][1]
        else:
            m_all = functools.reduce(jnp.maximum, [m for m, _ in parts])
            ot = sum(acc * jnp.exp2(m - m_all) for m, acc in parts)
        return ot[0:LANES] / ot[LANES:LANES + 1]

    def write_unit(u, oa, ob):
        if kind == "diff":
            o = (oa - lam_ref[0:1, 0:1] * ob).T
            o = _rms(o) * gdo_ref[...] * (1.0 - lam_init)
        else:
            row = lax.broadcasted_iota(jnp.int32, oa.shape, 0)
            o = jnp.where(row < HEAD_DIM, oa, ob).T
        o_ref[:, u * LANES:(u + 1) * LANES] = o.astype(BF16)

    def compute(nk):
        chunks = key_chunks(nk)
        queries = [unit_queries(u) for u in range(upb)]

        def soft(u, st):
            return merge_chunks([chunk_softmax_pv(u, c0, c1, st[c0:c1]) for c0, c1 in chunks])

        sa = map_scores(0, 0, queries[0][0], nk)
        for u in range(upb):
            sb = map_scores(u, 1, queries[u][1], nk)
            oa = soft(u, sa)
            if u + 1 < upb:
                sa = map_scores(u + 1, 0, queries[u + 1][0], nk)
            ob = soft(u, sb)
            write_unit(u, oa, ob)

    if first_tile_is_ctx:
        j = pl.program_id(2)
        pl.when(j == 0)(lambda: compute(n_ctx))
        pl.when(j > 0)(lambda: compute(n_keys))
    else:
        compute(n_keys)


def _attn_call(kind, layer, q, k, v, extra, *, n_ctx, lam_init, include_ctx):
    b, t, _ = k.shape
    tq = TOKEN_TILE
    upb = UNITS_PER_STEP
    n_ctx_tiles = n_ctx // tq
    q_off = 0 if include_ctx else n_ctx_tiles
    nq = t // tq - q_off
    qw = q.shape[-1] // N_UNITS
    shared_kv = kind == "gqa"
    n_kv = 1 if shared_kv else upb
    kw = LANES if shared_kv else k.shape[-1] // N_UNITS
    kv_map = (lambda bi, g, j: (bi, 0, 0)) if shared_kv else (lambda bi, g, j: (bi, 0, g))
    in_specs = [
        pl.BlockSpec((None, tq, upb * qw), lambda bi, g, j: (bi, j + q_off, g)),
        pl.BlockSpec((None, t, n_kv * kw), kv_map),
        pl.BlockSpec((None, t, n_kv * LANES), kv_map),
    ]
    in_specs += [_layer_spec(a, layer) for a in extra]
    return pl.pallas_call(
        functools.partial(_attn_kernel, kind=kind, n_ctx=n_ctx, lam_init=lam_init,
                          first_tile_is_ctx=include_ctx, upb=upb),
        grid=(b, N_UNITS // upb, nq),
        in_specs=in_specs,
        out_specs=pl.BlockSpec((None, tq, upb * LANES), lambda bi, g, j: (bi, j, g)),
        out_shape=jax.ShapeDtypeStruct((b, nq * tq, BRANCH_W), BF16),
        scratch_shapes=[pltpu.VMEM((n_kv, LANES + SUM_ROWS, t), BF16)],
        compiler_params=pltpu.CompilerParams(
            dimension_semantics=("parallel", "arbitrary", "arbitrary"), vmem_limit_bytes=VMEM_LIMIT),
        name="attention_" + kind,
    )(q, k, v, *extra)


def _merge_kernel(*refs, n_tok, n_ctx_tiles):
    (mod_ref, g1_ref, wg_ref, bg_ref, od_ref, og_ref, om_ref,
     wbd_ref, wbg_ref, wbm_ref, wout_ref, xo_ref) = refs[n_tok:]
    x = _read_tokens(refs[:n_tok], n_ctx_tiles)
    d = x.shape[-1]
    hb = _adaln(x, g1_ref[...], mod_ref[0:1, :], mod_ref[1:2, :]).astype(BF16)
    y = None
    for i, (o_ref, wb_ref) in enumerate(((od_ref, wbd_ref), (og_ref, wbg_ref), (om_ref, wbm_ref))):
        sl = slice(i * d, (i + 1) * d)
        gate = jax.nn.sigmoid(_dot(hb, wg_ref[:, sl]) + bg_ref[:, sl])
        term = gate * _dot(o_ref[...], wb_ref[...])
        y = term if y is None else y + term
    out = _dot(y.astype(BF16), wout_ref[...])
    xo_ref[...] = x + mod_ref[2:3, :] * out


def _merge_call(layer, tokens, modarr, g1, wg, bg, od, og, om, wbd, wbg, wbm, wout, *, x_off, n_ctx_tiles):
    b, d = tokens[0].shape[0], tokens[0].shape[-1]
    tm = TOKEN_TILE
    nt = od.shape[1] // tm
    tok = lambda w: pl.BlockSpec((None, tm, w), lambda bi, ti: (bi, ti, 0))
    return pl.pallas_call(
        functools.partial(_merge_kernel, n_tok=len(tokens), n_ctx_tiles=n_ctx_tiles),
        grid=(b, nt),
        in_specs=_token_specs(tokens, x_off, n_ctx_tiles) + [_mod_spec(modarr, layer, x_off, n_ctx_tiles)] + [
            _layer_spec(a, layer) for a in (g1, wg, bg)] + [tok(BRANCH_W)] * 3 + [
            _layer_spec(a, layer) for a in (wbd, wbg, wbm, wout)],
        out_specs=tok(d),
        out_shape=jax.ShapeDtypeStruct((b, nt * tm, d), F32),
        compiler_params=pltpu.CompilerParams(
            dimension_semantics=("parallel", "arbitrary"), vmem_limit_bytes=VMEM_LIMIT),
        name="branch_merge",
    )(*tokens, modarr, g1, wg, bg, od, og, om, wbd, wbg, wbm, wout)


def _ffn_kernel(*refs, final):
    if final:
        x_ref, mod_ref, g2_ref, win_ref, wout_ref, gf_ref, o_ref = refs
    else:
        x_ref, mod_ref, g2_ref, win_ref, wout_ref, o_ref = refs
    x = x_ref[...]
    hidden = wout_ref.shape[0]
    hb = _adaln(x, g2_ref[...], mod_ref[3:4, :], mod_ref[4:5, :]).astype(BF16)
    g = _dot(hb, win_ref[:, :hidden])
    u = _dot(hb, win_ref[:, hidden:])
    a = (g * jax.nn.sigmoid(g) * u).astype(BF16)
    xn = x + mod_ref[5:6, :] * _dot(a, wout_ref[...])
    if final:
        xn = _rms(xn) * gf_ref[...]
    o_ref[...] = xn


def _ffn_call(layer, xin, modarr, g2, win, wout, gf, *, pos_off, n_ctx_tiles):
    b, t, d = xin.shape
    tm = TOKEN_TILE
    tok = pl.BlockSpec((None, tm, d), lambda bi, ti: (bi, ti, 0))
    final = gf is not None
    in_specs = [tok, _mod_spec(modarr, layer, pos_off, n_ctx_tiles)] + [
        _layer_spec(a, layer) for a in (g2, win, wout)]
    args = [xin, modarr, g2, win, wout]
    if final:
        in_specs.append(_const_spec((1, d)))
        args.append(gf)
    return pl.pallas_call(
        functools.partial(_ffn_kernel, final=final),
        grid=(b, t // tm),
        in_specs=in_specs,
        out_specs=tok,
        out_shape=jax.ShapeDtypeStruct((b, t, d), F32),
        compiler_params=pltpu.CompilerParams(
            dimension_semantics=("parallel", "arbitrary"), vmem_limit_bytes=VMEM_LIMIT),
        name="swiglu",
    )(*args)


def kernel(x, c, ctx, c_ctx, w_mod, b_mod, g_norm1, w_in, b_gate, lam_q1, lam_k1, lam_q2, lam_k2, g_diff_out, g_gqa_q, g_gqa_k, g_mla_q, w_mla_uq, g_mla_kv, w_mla_ukv, w_br_diff, w_br_gqa, w_br_mla, w_out, g_norm2, w_ffn_in, w_ffn_out, g_final):
    b, seq, d = x.shape
    n_ctx = ctx.shape[1]
    depth = w_in.shape[0]
    assert n_ctx % TOKEN_TILE == 0 and seq % TOKEN_TILE == 0 and seq % GRID_W == 0
    n_ctx_tiles = n_ctx // TOKEN_TILE

    mod_rows = 32
    cc = jnp.zeros((mod_rows, d), F32).at[:b].set(c).at[b].set(c_ctx)
    mods = _mod_call(cc, w_mod, b_mod)
    mod_lat = mods[:, :b].reshape(depth, b, N_MOD, d)
    mod_ctx = jnp.broadcast_to(mods[:, b].reshape(depth, 1, N_MOD, d), (depth, b, N_MOD, d))
    modarr = jnp.stack([mod_ctx, mod_lat], axis=2)
    tables = _rope_tables(seq, n_ctx)

    lam_inits = [_lambda_init(l) for l in range(depth)]
    lam = (jnp.exp(jnp.sum(lam_q1 * lam_k1, axis=-1)) - jnp.exp(jnp.sum(lam_q2 * lam_k2, axis=-1))
           + jnp.asarray(lam_inits, F32))
    lam_vec = jnp.broadcast_to(lam.reshape(depth, 1, 1), (depth, 1, LANES))

    def row(a):
        return a.reshape(depth, 1, -1)

    w1, wg, wuq, wukv, wbg = _prep_weights(w_in, w_mla_uq, w_mla_ukv, w_br_gqa)
    wbd, wbm, wout = w_br_diff.astype(BF16), w_br_mla.astype(BF16), w_out.astype(BF16)
    wfi, wfo = w_ffn_in.astype(BF16), w_ffn_out.astype(BF16)
    g1, g2, bg = row(g_norm1), row(g_norm2), row(b_gate)
    gq_g, gk_g = row(jnp.tile(g_gqa_q, (1, 2))), row(jnp.tile(g_gqa_k, (1, 2)))
    gmq, gmkv, gdo = row(g_mla_q), row(g_mla_kv), row(g_diff_out)

    tokens = (ctx, x)
    for l in range(depth):
        last = l == depth - 1
        dq, dk, dv, gq, gk, gv, mq, mk, mv = _proj_call(
            l, tokens, modarr, g1, w1, wuq, wukv, gq_g, gk_g, gmq, gmkv, tables, n_ctx_tiles)

        attn = functools.partial(_attn_call, n_ctx=n_ctx, lam_init=lam_inits[l], include_ctx=not last)
        od = attn("diff", l, dq, dk, dv, (lam_vec, gdo))
        og = attn("gqa", l, gq, gk, gv, ())
        om = attn("mla", l, mq, mk, mv, ())

        x_off = n_ctx_tiles if last else 0
        xmid = _merge_call(l, tokens, modarr, g1, wg, bg, od, og, om, wbd, wbg, wbm, wout,
                           x_off=x_off, n_ctx_tiles=n_ctx_tiles)
        xout = _ffn_call(l, xmid, modarr, g2, wfi, wfo, g_final.reshape(1, d) if last else None,
                         pos_off=x_off, n_ctx_tiles=n_ctx_tiles)
        tokens = (xout,)
    return xout
```

```python
import collections
import functools
import math

import jax
import jax.numpy as jnp
from jax import lax
from jax.experimental import pallas as pl
from jax.experimental.pallas import tpu as pltpu

F32 = jnp.float32
BF16 = jnp.bfloat16

GRID_W = 64
ROPE_THETA = 10000.0
EPS = 1e-6
N_MOD = 6

DIFF_HEADS = 4
HEAD_DIM = 64
GQA_KV_HEADS = 2
GQA_GROUP = 4
MLA_HEADS = 8
MLA_Q_LORA = 384
MLA_KV_LORA = 256
MLA_NOPE = 64
MLA_ROPE = 32
MLA_QK = MLA_NOPE + MLA_ROPE
MLA_V = 64
ROT64 = HEAD_DIM // 4
ROT32 = MLA_ROPE // 4
BRANCH_W = 512
N_UNITS = 4

LANES = 128
KEY_CHUNK = 256
SUM_ROWS = 16
LOG2E = math.log2(math.e)
TOKEN_TILE = 256
VMEM_LIMIT = 56 * 1024 * 1024

OFF_DQ, OFF_DK, OFF_DV, OFF_GQ, OFF_GK, OFF_GV = 0, 512, 1024, 1536, 2048, 2176
OFF_MCQ, OFF_MCKV, OFF_MKR, W1_WIDTH = 2304, 2688, 2944, 3072


def _lambda_init(layer):
    return 0.8 - 0.6 * math.exp(-0.3 * layer)


_IN_GQ, _IN_GK, _IN_MKR, _IN_GATE = 1536, 2048, 2944, 2976


def _pair_heads(a, axis):
    shape = a.shape
    a = a.reshape(shape[:axis] + (GQA_KV_HEADS, GQA_GROUP, HEAD_DIM) + shape[axis + 1:])
    return jnp.swapaxes(a, axis, axis + 1).reshape(shape)


def _prep_weights(w_in, w_mla_uq, w_mla_ukv, w_br_gqa):
    depth = w_in.shape[0]
    mkr = jnp.pad(w_in[:, :, _IN_MKR:_IN_GATE], ((0, 0), (0, 0), (MLA_NOPE, LANES - MLA_QK)))
    w1 = jnp.concatenate([w_in[:, :, :_IN_GQ], _pair_heads(w_in[:, :, _IN_GQ:_IN_GK], 2),
                          w_in[:, :, _IN_GK:_IN_MKR], mkr], axis=2).astype(BF16)
    wg = w_in[:, :, _IN_GATE:].astype(BF16)
    uq = w_mla_uq.reshape(depth, MLA_Q_LORA, MLA_HEADS, MLA_QK)
    uq = jnp.pad(uq, ((0, 0), (0, 0), (0, 0), (0, LANES - MLA_QK)))
    uq = uq.reshape(depth, MLA_Q_LORA, MLA_HEADS * LANES).astype(BF16)
    ukv = w_mla_ukv.reshape(depth, MLA_KV_LORA, MLA_HEADS, MLA_NOPE + MLA_V)
    ukn = jnp.pad(ukv[..., :MLA_NOPE], ((0, 0), (0, 0), (0, 0), (0, LANES - MLA_NOPE)))
    ukn = ukn.reshape(depth, MLA_KV_LORA, -1)
    uv = ukv[..., MLA_NOPE:].reshape(depth, MLA_KV_LORA, -1)
    ukv2 = jnp.concatenate([ukn, uv], axis=2).astype(BF16)
    wbg = _pair_heads(w_br_gqa, 1).astype(BF16)
    return w1, wg, uq, ukv2, wbg


def _rope_tables(seq, n_ctx):
    rows = seq // GRID_W
    row = jnp.repeat(jnp.arange(rows, dtype=jnp.int32), GRID_W).astype(F32)
    col = jnp.tile(jnp.arange(GRID_W, dtype=jnp.int32), rows).astype(F32)

    def table(dim):
        a = dim // 2
        freqs = ROPE_THETA ** (-jnp.arange(0, a, 2, dtype=F32) / a)
        ang_r, ang_c = row[:, None] * freqs, col[:, None] * freqs
        cos = jnp.concatenate([jnp.cos(ang_r), jnp.cos(ang_r), jnp.cos(ang_c), jnp.cos(ang_c)], axis=-1)
        sin = jnp.concatenate([-jnp.sin(ang_r), jnp.sin(ang_r), -jnp.sin(ang_c), jnp.sin(ang_c)], axis=-1)
        return cos, sin

    c64, s64 = table(HEAD_DIM)
    c64, s64 = jnp.tile(c64, (1, 2)), jnp.tile(s64, (1, 2))
    c32, s32 = table(MLA_ROPE)
    pad = ((0, 0), (MLA_NOPE, LANES - MLA_QK))
    c32 = jnp.pad(c32, pad, constant_values=1.0)
    s32 = jnp.pad(s32, pad)
    ctx_pad = ((n_ctx, 0), (0, 0))
    return (jnp.pad(c64, ctx_pad, constant_values=1.0), jnp.pad(s64, ctx_pad),
            jnp.pad(c32, ctx_pad, constant_values=1.0), jnp.pad(s32, ctx_pad))


def _rms(x):
    return x * lax.rsqrt(jnp.mean(x * x, axis=-1, keepdims=True) + EPS)


def _adaln(x, g, shift, scale):
    return (_rms(x) * g) * (1.0 + scale) + shift


def _rope(z, cos, sin, is_x1, shift):
    zb = z.astype(BF16)
    partner = jnp.where(is_x1, pltpu.roll(zb, LANES - shift, 1), pltpu.roll(zb, shift, 1))
    return z * cos + partner.astype(F32) * sin


def _half_rms(z, lo, g):
    sq = z * z
    zero = jnp.zeros_like(sq)
    s_lo = jnp.sum(jnp.where(lo, sq, zero), axis=-1, keepdims=True)
    s_hi = jnp.sum(jnp.where(lo, zero, sq), axis=-1, keepdims=True)
    ms = jnp.where(lo, s_lo, s_hi) * (1.0 / HEAD_DIM)
    return z * lax.rsqrt(ms + EPS) * g


def _dot(a, b):
    return jnp.dot(a, b, preferred_element_type=F32)


def _const_spec(shape):
    return pl.BlockSpec(shape, lambda *_: (0,) * len(shape))


def _layer_spec(stacked, layer):
    shape = stacked.shape[1:]
    return pl.BlockSpec((None,) + shape, lambda *_: (layer,) + (0,) * len(shape))


def _mod_spec(modarr, layer, pos_off, n_ctx_tiles):
    return pl.BlockSpec((None, None, None) + modarr.shape[3:],
                        lambda bi, ti: (layer, bi, (ti + pos_off >= n_ctx_tiles).astype(jnp.int32), 0, 0))


def _token_specs(tokens, tile_off, n_ctx_tiles):
    tm, d = TOKEN_TILE, tokens[0].shape[-1]
    if len(tokens) == 1:
        return [pl.BlockSpec((None, tm, d), lambda bi, ti: (bi, ti + tile_off, 0))]
    assert tile_off == 0
    return [pl.BlockSpec((None, tm, d), lambda bi, ti: (bi, jnp.minimum(ti, n_ctx_tiles - 1), 0)),
            pl.BlockSpec((None, tm, d), lambda bi, ti: (bi, jnp.maximum(ti - n_ctx_tiles, 0), 0))]


def _read_tokens(tok_refs, n_ctx_tiles):
    if len(tok_refs) == 1:
        return tok_refs[0][...]
    ctx_ref, x_ref = tok_refs
    return jnp.where(pl.program_id(1) < n_ctx_tiles, ctx_ref[...], x_ref[...])


def _mod_kernel(c_ref, w_ref, b_ref, o_ref):
    cc = c_ref[...]
    a = (cc * jax.nn.sigmoid(cc)).astype(BF16)
    o_ref[...] = _dot(a, w_ref[...].astype(BF16)) + b_ref[...]


def _mod_call(cc, w_mod, b_mod):
    depth, d, n = w_mod.shape
    tn = 1536
    rows = cc.shape[0]
    return pl.pallas_call(
        _mod_kernel,
        grid=(depth, n // tn),
        in_specs=[
            pl.BlockSpec((rows, d), lambda l, j: (0, 0)),
            pl.BlockSpec((None, d, tn), lambda l, j: (l, 0, j)),
            pl.BlockSpec((None, 1, tn), lambda l, j: (l, 0, j)),
        ],
        out_specs=pl.BlockSpec((None, rows, tn), lambda l, j: (l, 0, j)),
        out_shape=jax.ShapeDtypeStruct((depth, rows, n), F32),
        compiler_params=pltpu.CompilerParams(vmem_limit_bytes=VMEM_LIMIT),
        name="mod_vectors",
    )(cc, w_mod, b_mod.reshape(depth, 1, n))


def _proj_kernel(*refs, n_tok, n_ctx_tiles):
    (mod_ref, g1_ref, w1_ref, wuq_ref, wukv_ref, gq_g_ref, gk_g_ref, gmq_ref, gmkv_ref,
     c64_ref, s64_ref, c32_ref, s32_ref,
     dq_o, dk_o, dv_o, gq_o, gk_o, gv_o, mq_o, mk_o, mv_o) = refs[n_tok:]
    x = _read_tokens(refs[:n_tok], n_ctx_tiles)
    hb = _adaln(x, g1_ref[...], mod_ref[0:1, :], mod_ref[1:2, :]).astype(BF16)
    tm = x.shape[0]
    lane = lax.broadcasted_iota(jnp.int32, (tm, LANES), 1)
    lo = lane < HEAD_DIM
    x1_64 = (lane & ROT64) == 0
    x1_32 = (lane & ROT32) == 0
    c64, s64 = c64_ref[...], s64_ref[...]
    c32, s32 = c32_ref[...], s32_ref[...]
    qk_scale = HEAD_DIM ** -0.5 * LOG2E

    def seg(off, width):
        return _dot(hb, w1_ref[:, off:off + width])

    def blocks(z):
        return [(slice(i * LANES, (i + 1) * LANES), z[:, i * LANES:(i + 1) * LANES]) for i in range(z.shape[1] // LANES)]

    zm = seg(OFF_MCQ, W1_WIDTH - OFF_MCQ)
    mcq = (_rms(zm[:, :MLA_Q_LORA]) * gmq_ref[...]).astype(BF16)
    mckv = (_rms(zm[:, MLA_Q_LORA:MLA_Q_LORA + MLA_KV_LORA]) * gmkv_ref[...]).astype(BF16)
    mkr = _rope(zm[:, MLA_Q_LORA + MLA_KV_LORA:], c32, s32, x1_32, ROT32)

    for sl, z in blocks(seg(OFF_DQ, BRANCH_W)):
        dq_o[:, sl] = (_rope(z, c64, s64, x1_64, ROT64) * qk_scale).astype(BF16)
    for sl, z in blocks(seg(OFF_DK, BRANCH_W)):
        dk_o[:, sl] = _rope(z, c64, s64, x1_64, ROT64).astype(BF16)
    dv_o[...] = seg(OFF_DV, BRANCH_W).astype(BF16)
    for sl, z in blocks(seg(OFF_GQ, BRANCH_W)):
        gq = _half_rms(z, lo, gq_g_ref[...])
        gq_o[:, sl] = (_rope(gq, c64, s64, x1_64, ROT64) * qk_scale).astype(BF16)
    gkv = seg(OFF_GK, 2 * LANES)
    gk = _half_rms(gkv[:, :LANES], lo, gk_g_ref[...])
    gk_o[...] = _rope(gk, c64, s64, x1_64, ROT64).astype(BF16)
    gv_o[...] = gkv[:, LANES:].astype(BF16)

    mla_scale = MLA_QK ** -0.5 * LOG2E
    for sl, z in blocks(_dot(mcq, wuq_ref[...])):
        mq_o[:, sl] = (_rope(z, c32, s32, x1_32, ROT32) * mla_scale).astype(BF16)
    mkv = _dot(mckv, wukv_ref[...])
    for sl, z in blocks(mkv[:, :MLA_HEADS * LANES]):
        mk_o[:, sl] = (z + mkr).astype(BF16)
    mv_o[...] = mkv[:, MLA_HEADS * LANES:].astype(BF16)


def _proj_call(layer, tokens, modarr, g1, w1, wuq, wukv, gq_g, gk_g, gmq, gmkv, tables, n_ctx_tiles):
    b = tokens[0].shape[0]
    t = sum(a.shape[1] for a in tokens)
    tm = TOKEN_TILE
    tok = lambda w: pl.BlockSpec((None, tm, w), lambda bi, ti: (bi, ti, 0))
    tab = pl.BlockSpec((tm, LANES), lambda bi, ti: (ti, 0))
    widths = (BRANCH_W, BRANCH_W, BRANCH_W, BRANCH_W, LANES, LANES, MLA_HEADS * LANES, MLA_HEADS * LANES, BRANCH_W)
    return pl.pallas_call(
        functools.partial(_proj_kernel, n_tok=len(tokens), n_ctx_tiles=n_ctx_tiles),
        grid=(b, t // tm),
        in_specs=_token_specs(tokens, 0, n_ctx_tiles) + [_mod_spec(modarr, layer, 0, n_ctx_tiles)] + [
            _layer_spec(a, layer) for a in (g1, w1, wuq, wukv, gq_g, gk_g, gmq, gmkv)] + [tab, tab, tab, tab],
        out_specs=[tok(w) for w in widths],
        out_shape=[jax.ShapeDtypeStruct((b, t, w), BF16) for w in widths],
        compiler_params=pltpu.CompilerParams(
            dimension_semantics=("parallel", "arbitrary"), vmem_limit_bytes=VMEM_LIMIT),
        name="input_projection",
    )(*tokens, modarr, g1, w1, wuq, wukv, gq_g, gk_g, gmq, gmkv, *tables)


_Branch = collections.namedtuple("_Branch", "kind q_ref q_width k_ref k_width v_ref n_kv vt_slot o_ref")


def _attn_kernel(dq_ref, dk_ref, dv_ref, lam_ref, gdo_ref, gq_ref, gk_ref, gv_ref, mq_ref, mk_ref, mv_ref,
                 od_ref, og_ref, om_ref, vt_ref, *, n_ctx, lam_init, first_tile_is_ctx):
    tq = dq_ref.shape[0]
    n_keys = dk_ref.shape[0]
    lane = lax.broadcasted_iota(jnp.int32, (tq, LANES), 1)
    lo = lane < HEAD_DIM
    branches = (_Branch("diff", dq_ref, LANES, dk_ref, LANES, dv_ref, N_UNITS, 0, od_ref),
                _Branch("gqa", gq_ref, LANES, gk_ref, LANES, gv_ref, 1, N_UNITS, og_ref),
                _Branch("mla", mq_ref, 2 * LANES, mk_ref, 2 * LANES, mv_ref, N_UNITS, N_UNITS + 1, om_ref))

    @pl.when(pl.program_id(1) == 0)
    def _():
        for br in branches:
            for i in range(br.n_kv):
                vt_ref[br.vt_slot + i, 0:LANES, :] = br.v_ref[:, i * LANES:(i + 1) * LANES].astype(F32).T.astype(BF16)
                vt_ref[br.vt_slot + i, LANES:, :] = jnp.ones((SUM_ROWS, n_keys), BF16)

    def key_chunks(nk):
        step = KEY_CHUNK if nk % KEY_CHUNK == 0 else nk
        return [(c, c + step) for c in range(0, nk, step)]

    def unit_queries(br, u):
        q = br.q_ref[:, u * br.q_width:(u + 1) * br.q_width]
        if br.kind == "mla":
            return q[:, :LANES], q[:, LANES:]
        zero = jnp.zeros_like(q)
        return jnp.where(lo, q, zero), jnp.where(lo, zero, q)

    def map_scores(br, u, which, q, nk):
        kv = u % br.n_kv
        k = br.k_ref[0:nk, kv * br.k_width:(kv + 1) * br.k_width]
        if br.kind == "mla":
            k = k[:, :LANES] if which == 0 else k[:, LANES:]
        return lax.dot_general(k, q, (((1,), (1,)), ((), ())), preferred_element_type=F32)

    def chunk_softmax_pv(br, u, c0, c1, st):
        m = jnp.max(st, axis=0, keepdims=True)
        p = jnp.exp2((st - m).astype(BF16))
        return m, _dot(vt_ref[br.vt_slot + u % br.n_kv, :, c0:c1], p)

    def merge_chunks(parts):
        if len(parts) == 1:
            ot = parts[0][1]
        else:
            m_all = functools.reduce(jnp.maximum, [m for m, _ in parts])
            ot = sum(acc * jnp.exp2(m - m_all) for m, acc in parts)
        return ot[0:LANES] / ot[LANES:LANES + 1]

    def write_unit(br, u, oa, ob):
        if br.kind == "diff":
            o = (oa - lam_ref[0:1, 0:1] * ob).T
            o = _rms(o) * gdo_ref[...] * (1.0 - lam_init)
        else:
            row = lax.broadcasted_iota(jnp.int32, oa.shape, 0)
            o = jnp.where(row < HEAD_DIM, oa, ob).T
        br.o_ref[:, u * LANES:(u + 1) * LANES] = o.astype(BF16)

    def compute(nk):
        chunks = key_chunks(nk)
        maps = [(br, u, w) for br in branches for u in range(N_UNITS) for w in (0, 1)]
        queries = {}

        def scores(br, u, w):
            if (br.kind, u) not in queries:
                queries[(br.kind, u)] = unit_queries(br, u)
            return map_scores(br, u, w, queries[(br.kind, u)][w], nk)

        st = scores(*maps[0])
        first = None
        for t, (br, u, w) in enumerate(maps):
            nxt = scores(*maps[t + 1]) if t + 1 < len(maps) else None
            o = merge_chunks([chunk_softmax_pv(br, u, c0, c1, st[c0:c1]) for c0, c1 in chunks])
            if w == 0:
                first = o
            else:
                write_unit(br, u, first, o)
            st = nxt

    if first_tile_is_ctx:
        j = pl.program_id(1)
        pl.when(j == 0)(lambda: compute(n_ctx))
        pl.when(j > 0)(lambda: compute(n_keys))
    else:
        compute(n_keys)


def _attn_call(layer, qkv, lam_vec, gdo, *, n_ctx, lam_init, include_ctx):
    dq, dk, dv, gq, gk, gv, mq, mk, mv = qkv
    b, t, _ = dk.shape
    tq = TOKEN_TILE
    q_off = 0 if include_ctx else n_ctx // tq
    nq = t // tq - q_off
    qspec = lambda a: pl.BlockSpec((None, tq, a.shape[-1]), lambda bi, j: (bi, j + q_off, 0))
    kvspec = lambda a: pl.BlockSpec((None, t, a.shape[-1]), lambda bi, j: (bi, 0, 0))
    ospec = pl.BlockSpec((None, tq, BRANCH_W), lambda bi, j: (bi, j, 0))
    oshape = jax.ShapeDtypeStruct((b, nq * tq, BRANCH_W), BF16)
    return pl.pallas_call(
        functools.partial(_attn_kernel, n_ctx=n_ctx, lam_init=lam_init, first_tile_is_ctx=include_ctx),
        grid=(b, nq),
        in_specs=[qspec(dq), kvspec(dk), kvspec(dv), _layer_spec(lam_vec, layer), _layer_spec(gdo, layer),
                  qspec(gq), kvspec(gk), kvspec(gv), qspec(mq), kvspec(mk), kvspec(mv)],
        out_specs=[ospec] * 3,
        out_shape=[oshape] * 3,
        scratch_shapes=[pltpu.VMEM((2 * N_UNITS + 1, LANES + SUM_ROWS, t), BF16)],
        compiler_params=pltpu.CompilerParams(
            dimension_semantics=("parallel", "arbitrary"), vmem_limit_bytes=VMEM_LIMIT),
        name="attention",
    )(dq, dk, dv, lam_vec, gdo, gq, gk, gv, mq, mk, mv)


def _merge_kernel(*refs, n_tok, n_ctx_tiles):
    (mod_ref, g1_ref, wg_ref, bg_ref, od_ref, og_ref, om_ref,
     wbd_ref, wbg_ref, wbm_ref, wout_ref, xo_ref) = refs[n_tok:]
    x = _read_tokens(refs[:n_tok], n_ctx_tiles)
    d = x.shape[-1]
    hb = _adaln(x, g1_ref[...], mod_ref[0:1, :], mod_ref[1:2, :]).astype(BF16)
    y = None
    for i, (o_ref, wb_ref) in enumerate(((od_ref, wbd_ref), (og_ref, wbg_ref), (om_ref, wbm_ref))):
        sl = slice(i * d, (i + 1) * d)
        gate = jax.nn.sigmoid(_dot(hb, wg_ref[:, sl]) + bg_ref[:, sl])
        term = gate * _dot(o_ref[...], wb_ref[...])
        y = term if y is None else y + term
    out = _dot(y.astype(BF16), wout_ref[...])
    xo_ref[...] = x + mod_ref[2:3, :] * out


def _merge_call(layer, tokens, modarr, g1, wg, bg, od, og, om, wbd, wbg, wbm, wout, *, x_off, n_ctx_tiles):
    b, d = tokens[0].shape[0], tokens[0].shape[-1]
    tm = TOKEN_TILE
    nt = od.shape[1] // tm
    tok = lambda w: pl.BlockSpec((None, tm, w), lambda bi, ti: (bi, ti, 0))
    return pl.pallas_call(
        functools.partial(_merge_kernel, n_tok=len(tokens), n_ctx_tiles=n_ctx_tiles),
        grid=(b, nt),
        in_specs=_token_specs(tokens, x_off, n_ctx_tiles) + [_mod_spec(modarr, layer, x_off, n_ctx_tiles)] + [
            _layer_spec(a, layer) for a in (g1, wg, bg)] + [tok(BRANCH_W)] * 3 + [
            _layer_spec(a, layer) for a in (wbd, wbg, wbm, wout)],
        out_specs=tok(d),
        out_shape=jax.ShapeDtypeStruct((b, nt * tm, d), F32),
        compiler_params=pltpu.CompilerParams(
            dimension_semantics=("parallel", "arbitrary"), vmem_limit_bytes=VMEM_LIMIT),
        name="branch_merge",
    )(*tokens, modarr, g1, wg, bg, od, og, om, wbd, wbg, wbm, wout)


def _ffn_kernel(*refs, final):
    if final:
        x_ref, mod_ref, g2_ref, win_ref, wout_ref, gf_ref, o_ref = refs
    else:
        x_ref, mod_ref, g2_ref, win_ref, wout_ref, o_ref = refs
    x = x_ref[...]
    hidden = wout_ref.shape[0]
    hb = _adaln(x, g2_ref[...], mod_ref[3:4, :], mod_ref[4:5, :]).astype(BF16)
    g = _dot(hb, win_ref[:, :hidden])
    u = _dot(hb, win_ref[:, hidden:])
    a = (g * jax.nn.sigmoid(g) * u).astype(BF16)
    xn = x + mod_ref[5:6, :] * _dot(a, wout_ref[...])
    if final:
        xn = _rms(xn) * gf_ref[...]
    o_ref[...] = xn


def _ffn_call(layer, xin, modarr, g2, win, wout, gf, *, pos_off, n_ctx_tiles):
    b, t, d = xin.shape
    tm = TOKEN_TILE
    tok = pl.BlockSpec((None, tm, d), lambda bi, ti: (bi, ti, 0))
    final = gf is not None
    in_specs = [tok, _mod_spec(modarr, layer, pos_off, n_ctx_tiles)] + [
        _layer_spec(a, layer) for a in (g2, win, wout)]
    args = [xin, modarr, g2, win, wout]
    if final:
        in_specs.append(_const_spec((1, d)))
        args.append(gf)
    return pl.pallas_call(
        functools.partial(_ffn_kernel, final=final),
        grid=(b, t // tm),
        in_specs=in_specs,
        out_specs=tok,
        out_shape=jax.ShapeDtypeStruct((b, t, d), F32),
        compiler_params=pltpu.CompilerParams(
            dimension_semantics=("parallel", "arbitrary"), vmem_limit_bytes=VMEM_LIMIT),
        name="swiglu",
    )(*args)


def kernel(x, c, ctx, c_ctx, w_mod, b_mod, g_norm1, w_in, b_gate, lam_q1, lam_k1, lam_q2, lam_k2, g_diff_out, g_gqa_q, g_gqa_k, g_mla_q, w_mla_uq, g_mla_kv, w_mla_ukv, w_br_diff, w_br_gqa, w_br_mla, w_out, g_norm2, w_ffn_in, w_ffn_out, g_final):
    b, seq, d = x.shape
    n_ctx = ctx.shape[1]
    depth = w_in.shape[0]
    assert n_ctx % TOKEN_TILE == 0 and seq % TOKEN_TILE == 0 and seq % GRID_W == 0
    n_ctx_tiles = n_ctx // TOKEN_TILE

    mod_rows = 32
    cc = jnp.zeros((mod_rows, d), F32).at[:b].set(c).at[b].set(c_ctx)
    mods = _mod_call(cc, w_mod, b_mod)
    mod_lat = mods[:, :b].reshape(depth, b, N_MOD, d)
    mod_ctx = jnp.broadcast_to(mods[:, b].reshape(depth, 1, N_MOD, d), (depth, b, N_MOD, d))
    modarr = jnp.stack([mod_ctx, mod_lat], axis=2)
    tables = _rope_tables(seq, n_ctx)

    lam_inits = [_lambda_init(l) for l in range(depth)]
    lam = (jnp.exp(jnp.sum(lam_q1 * lam_k1, axis=-1)) - jnp.exp(jnp.sum(lam_q2 * lam_k2, axis=-1))
           + jnp.asarray(lam_inits, F32))
    lam_vec = jnp.broadcast_to(lam.reshape(depth, 1, 1), (depth, 1, LANES))

    def row(a):
        return a.reshape(depth, 1, -1)

    w1, wg, wuq, wukv, wbg = _prep_weights(w_in, w_mla_uq, w_mla_ukv, w_br_gqa)
    wbd, wbm, wout = w_br_diff.astype(BF16), w_br_mla.astype(BF16), w_out.astype(BF16)
    wfi, wfo = w_ffn_in.astype(BF16), w_ffn_out.astype(BF16)
    g1, g2, bg = row(g_norm1), row(g_norm2), row(b_gate)
    gq_g, gk_g = row(jnp.tile(g_gqa_q, (1, 2))), row(jnp.tile(g_gqa_k, (1, 2)))
    gmq, gmkv, gdo = row(g_mla_q), row(g_mla_kv), row(g_diff_out)

    tokens = (ctx, x)
    for l in range(depth):
        last = l == depth - 1
        qkv = _proj_call(l, tokens, modarr, g1, w1, wuq, wukv, gq_g, gk_g, gmq, gmkv, tables, n_ctx_tiles)
        od, og, om = _attn_call(l, qkv, lam_vec, gdo, n_ctx=n_ctx, lam_init=lam_inits[l], include_ctx=not last)

        x_off = n_ctx_tiles if last else 0
        xmid = _merge_call(l, tokens, modarr, g1, wg, bg, od, og, om, wbd, wbg, wbm, wout,
                           x_off=x_off, n_ctx_tiles=n_ctx_tiles)
        xout = _ffn_call(l, xmid, modarr, g2, wfi, wfo, g_final.reshape(1, d) if last else None,
                         pos_off=x_off, n_ctx_tiles=n_ctx_tiles)
        tokens = (xout,)
    return xout
```

```python
import collections
import functools
import math

import jax
import jax.numpy as jnp
from jax import lax
from jax.experimental import pallas as pl
from jax.experimental.pallas import tpu as pltpu

F32 = jnp.float32
BF16 = jnp.bfloat16

GRID_W = 64
ROPE_THETA = 10000.0
EPS = 1e-6
N_MOD = 6

DIFF_HEADS = 4
HEAD_DIM = 64
GQA_KV_HEADS = 2
GQA_GROUP = 4
MLA_HEADS = 8
MLA_Q_LORA = 384
MLA_KV_LORA = 256
MLA_NOPE = 64
MLA_ROPE = 32
MLA_QK = MLA_NOPE + MLA_ROPE
MLA_V = 64
ROT64 = HEAD_DIM // 4
ROT32 = MLA_ROPE // 4
BRANCH_W = 512
N_UNITS = 4

LANES = 128
KEY_CHUNK = 256
SUM_ROWS = 16
LOG2E = math.log2(math.e)
TOKEN_TILE = 256
VMEM_LIMIT = 56 * 1024 * 1024

OFF_DQ, OFF_DK, OFF_DV, OFF_GQ, OFF_GK, OFF_GV = 0, 512, 1024, 1536, 2048, 2176
OFF_MCQ, OFF_MCKV, OFF_MKR, W1_WIDTH = 2304, 2688, 2944, 3072


def _lambda_init(layer):
    return 0.8 - 0.6 * math.exp(-0.3 * layer)


_IN_GQ, _IN_GK, _IN_MKR, _IN_GATE = 1536, 2048, 2944, 2976


def _pair_heads(a, axis):
    shape = a.shape
    a = a.reshape(shape[:axis] + (GQA_KV_HEADS, GQA_GROUP, HEAD_DIM) + shape[axis + 1:])
    return jnp.swapaxes(a, axis, axis + 1).reshape(shape)


def _prep_weights(w_in, w_mla_uq, w_mla_ukv, w_br_gqa):
    depth = w_in.shape[0]
    mkr = jnp.pad(w_in[:, :, _IN_MKR:_IN_GATE], ((0, 0), (0, 0), (MLA_NOPE, LANES - MLA_QK)))
    w1 = jnp.concatenate([w_in[:, :, :_IN_GQ], _pair_heads(w_in[:, :, _IN_GQ:_IN_GK], 2),
                          w_in[:, :, _IN_GK:_IN_MKR], mkr], axis=2).astype(BF16)
    wg = w_in[:, :, _IN_GATE:].astype(BF16)
    uq = w_mla_uq.reshape(depth, MLA_Q_LORA, MLA_HEADS, MLA_QK)
    uq = jnp.pad(uq, ((0, 0), (0, 0), (0, 0), (0, LANES - MLA_QK)))
    uq = uq.reshape(depth, MLA_Q_LORA, MLA_HEADS * LANES).astype(BF16)
    ukv = w_mla_ukv.reshape(depth, MLA_KV_LORA, MLA_HEADS, MLA_NOPE + MLA_V)
    ukn = jnp.pad(ukv[..., :MLA_NOPE], ((0, 0), (0, 0), (0, 0), (0, LANES - MLA_NOPE)))
    ukn = ukn.reshape(depth, MLA_KV_LORA, -1)
    uv = ukv[..., MLA_NOPE:].reshape(depth, MLA_KV_LORA, -1)
    ukv2 = jnp.concatenate([ukn, uv], axis=2).astype(BF16)
    wbg = _pair_heads(w_br_gqa, 1).astype(BF16)
    return w1, wg, uq, ukv2, wbg


def _rope_tables(seq, n_ctx):
    rows = seq // GRID_W
    row = jnp.repeat(jnp.arange(rows, dtype=jnp.int32), GRID_W).astype(F32)
    col = jnp.tile(jnp.arange(GRID_W, dtype=jnp.int32), rows).astype(F32)

    def table(dim):
        a = dim // 2
        freqs = ROPE_THETA ** (-jnp.arange(0, a, 2, dtype=F32) / a)
        ang_r, ang_c = row[:, None] * freqs, col[:, None] * freqs
        cos = jnp.concatenate([jnp.cos(ang_r), jnp.cos(ang_r), jnp.cos(ang_c), jnp.cos(ang_c)], axis=-1)
        sin = jnp.concatenate([-jnp.sin(ang_r), jnp.sin(ang_r), -jnp.sin(ang_c), jnp.sin(ang_c)], axis=-1)
        return cos, sin

    c64, s64 = table(HEAD_DIM)
    c64, s64 = jnp.tile(c64, (1, 2)), jnp.tile(s64, (1, 2))
    c32, s32 = table(MLA_ROPE)
    pad = ((0, 0), (MLA_NOPE, LANES - MLA_QK))
    c32 = jnp.pad(c32, pad, constant_values=1.0)
    s32 = jnp.pad(s32, pad)
    ctx_pad = ((n_ctx, 0), (0, 0))
    return (jnp.pad(c64, ctx_pad, constant_values=1.0), jnp.pad(s64, ctx_pad),
            jnp.pad(c32, ctx_pad, constant_values=1.0), jnp.pad(s32, ctx_pad))


def _rms(x):
    return x * lax.rsqrt(jnp.mean(x * x, axis=-1, keepdims=True) + EPS)


def _adaln(x, g, shift, scale):
    return (_rms(x) * g) * (1.0 + scale) + shift


def _rope(z, cos, sin, is_x1, shift):
    zb = z.astype(BF16)
    partner = jnp.where(is_x1, pltpu.roll(zb, LANES - shift, 1), pltpu.roll(zb, shift, 1))
    return z * cos + partner.astype(F32) * sin


def _half_rms(z, lo, g):
    sq = z * z
    zero = jnp.zeros_like(sq)
    s_lo = jnp.sum(jnp.where(lo, sq, zero), axis=-1, keepdims=True)
    s_hi = jnp.sum(jnp.where(lo, zero, sq), axis=-1, keepdims=True)
    ms = jnp.where(lo, s_lo, s_hi) * (1.0 / HEAD_DIM)
    return z * lax.rsqrt(ms + EPS) * g


def _dot(a, b):
    return jnp.dot(a, b, preferred_element_type=F32)


def _const_spec(shape):
    return pl.BlockSpec(shape, lambda *_: (0,) * len(shape))


def _layer_spec(stacked, layer):
    shape = stacked.shape[1:]
    return pl.BlockSpec((None,) + shape, lambda *_: (layer,) + (0,) * len(shape))


def _mod_spec(modarr, layer, pos_off, n_ctx_tiles):
    return pl.BlockSpec((None, None, None) + modarr.shape[3:],
                        lambda bi, ti: (layer, bi, (ti + pos_off >= n_ctx_tiles).astype(jnp.int32), 0, 0))


def _token_specs(tokens, tile_off, n_ctx_tiles):
    tm, d = TOKEN_TILE, tokens[0].shape[-1]
    if len(tokens) == 1:
        return [pl.BlockSpec((None, tm, d), lambda bi, ti: (bi, ti + tile_off, 0))]
    assert tile_off == 0
    return [pl.BlockSpec((None, tm, d), lambda bi, ti: (bi, jnp.minimum(ti, n_ctx_tiles - 1), 0)),
            pl.BlockSpec((None, tm, d), lambda bi, ti: (bi, jnp.maximum(ti - n_ctx_tiles, 0), 0))]


def _read_tokens(tok_refs, n_ctx_tiles):
    if len(tok_refs) == 1:
        return tok_refs[0][...]
    ctx_ref, x_ref = tok_refs
    return jnp.where(pl.program_id(1) < n_ctx_tiles, ctx_ref[...], x_ref[...])


def _mod_kernel(c_ref, w_ref, b_ref, o_ref):
    cc = c_ref[...]
    a = (cc * jax.nn.sigmoid(cc)).astype(BF16)
    o_ref[...] = _dot(a, w_ref[...].astype(BF16)) + b_ref[...]


def _mod_call(cc, w_mod, b_mod):
    depth, d, n = w_mod.shape
    tn = 1536
    rows = cc.shape[0]
    return pl.pallas_call(
        _mod_kernel,
        grid=(depth, n // tn),
        in_specs=[
            pl.BlockSpec((rows, d), lambda l, j: (0, 0)),
            pl.BlockSpec((None, d, tn), lambda l, j: (l, 0, j)),
            pl.BlockSpec((None, 1, tn), lambda l, j: (l, 0, j)),
        ],
        out_specs=pl.BlockSpec((None, rows, tn), lambda l, j: (l, 0, j)),
        out_shape=jax.ShapeDtypeStruct((depth, rows, n), F32),
        compiler_params=pltpu.CompilerParams(vmem_limit_bytes=VMEM_LIMIT),
        name="mod_vectors",
    )(cc, w_mod, b_mod.reshape(depth, 1, n))


def _proj_kernel(*refs, n_tok, n_ctx_tiles):
    (mod_ref, g1_ref, w1_ref, wuq_ref, wukv_ref, gq_g_ref, gk_g_ref, gmq_ref, gmkv_ref,
     c64_ref, s64_ref, c32_ref, s32_ref,
     dq_o, dk_o, dv_o, gq_o, gk_o, gv_o, mq_o, mk_o, mv_o) = refs[n_tok:]
    x = _read_tokens(refs[:n_tok], n_ctx_tiles)
    hb = _adaln(x, g1_ref[...], mod_ref[0:1, :], mod_ref[1:2, :]).astype(BF16)
    tm = x.shape[0]
    lane = lax.broadcasted_iota(jnp.int32, (tm, LANES), 1)
    lo = lane < HEAD_DIM
    x1_64 = (lane & ROT64) == 0
    x1_32 = (lane & ROT32) == 0
    c64, s64 = c64_ref[...], s64_ref[...]
    c32, s32 = c32_ref[...], s32_ref[...]
    qk_scale = HEAD_DIM ** -0.5 * LOG2E

    def seg(off, width):
        return _dot(hb, w1_ref[:, off:off + width])

    def blocks(z):
        return [(slice(i * LANES, (i + 1) * LANES), z[:, i * LANES:(i + 1) * LANES]) for i in range(z.shape[1] // LANES)]

    zm = seg(OFF_MCQ, W1_WIDTH - OFF_MCQ)
    mcq = (_rms(zm[:, :MLA_Q_LORA]) * gmq_ref[...]).astype(BF16)
    mckv = (_rms(zm[:, MLA_Q_LORA:MLA_Q_LORA + MLA_KV_LORA]) * gmkv_ref[...]).astype(BF16)
    mkr = _rope(zm[:, MLA_Q_LORA + MLA_KV_LORA:], c32, s32, x1_32, ROT32)

    for sl, z in blocks(seg(OFF_DQ, BRANCH_W)):
        dq_o[:, sl] = (_rope(z, c64, s64, x1_64, ROT64) * qk_scale).astype(BF16)
    for sl, z in blocks(seg(OFF_DK, BRANCH_W)):
        dk_o[:, sl] = _rope(z, c64, s64, x1_64, ROT64).astype(BF16)
    dv_o[...] = seg(OFF_DV, BRANCH_W).astype(BF16)
    for sl, z in blocks(seg(OFF_GQ, BRANCH_W)):
        gq = _half_rms(z, lo, gq_g_ref[...])
        gq_o[:, sl] = (_rope(gq, c64, s64, x1_64, ROT64) * qk_scale).astype(BF16)
    gkv = seg(OFF_GK, 2 * LANES)
    gk = _half_rms(gkv[:, :LANES], lo, gk_g_ref[...])
    gk_o[...] = _rope(gk, c64, s64, x1_64, ROT64).astype(BF16)
    gv_o[...] = gkv[:, LANES:].astype(BF16)

    mla_scale = MLA_QK ** -0.5 * LOG2E
    for sl, z in blocks(_dot(mcq, wuq_ref[...])):
        mq_o[:, sl] = (_rope(z, c32, s32, x1_32, ROT32) * mla_scale).astype(BF16)
    mkv = _dot(mckv, wukv_ref[...])
    for sl, z in blocks(mkv[:, :MLA_HEADS * LANES]):
        mk_o[:, sl] = (z + mkr).astype(BF16)
    mv_o[...] = mkv[:, MLA_HEADS * LANES:].astype(BF16)


def _proj_call(layer, tokens, modarr, g1, w1, wuq, wukv, gq_g, gk_g, gmq, gmkv, tables, n_ctx_tiles):
    b = tokens[0].shape[0]
    t = sum(a.shape[1] for a in tokens)
    tm = TOKEN_TILE
    tok = lambda w: pl.BlockSpec((None, tm, w), lambda bi, ti: (bi, ti, 0))
    tab = pl.BlockSpec((tm, LANES), lambda bi, ti: (ti, 0))
    widths = (BRANCH_W, BRANCH_W, BRANCH_W, BRANCH_W, LANES, LANES, MLA_HEADS * LANES, MLA_HEADS * LANES, BRANCH_W)
    return pl.pallas_call(
        functools.partial(_proj_kernel, n_tok=len(tokens), n_ctx_tiles=n_ctx_tiles),
        grid=(b, t // tm),
        in_specs=_token_specs(tokens, 0, n_ctx_tiles) + [_mod_spec(modarr, layer, 0, n_ctx_tiles)] + [
            _layer_spec(a, layer) for a in (g1, w1, wuq, wukv, gq_g, gk_g, gmq, gmkv)] + [tab, tab, tab, tab],
        out_specs=[tok(w) for w in widths],
        out_shape=[jax.ShapeDtypeStruct((b, t, w), BF16) for w in widths],
        compiler_params=pltpu.CompilerParams(
            dimension_semantics=("parallel", "arbitrary"), vmem_limit_bytes=VMEM_LIMIT),
        name="input_projection",
    )(*tokens, modarr, g1, w1, wuq, wukv, gq_g, gk_g, gmq, gmkv, *tables)


_Branch = collections.namedtuple("_Branch", "kind q_ref q_width k_ref k_width v_ref n_kv vt_slot o_ref")
HALF_ROWS = HEAD_DIM + SUM_ROWS


def _attn_kernel(dq_ref, dk_ref, dv_ref, lam_ref, gdo_ref, gq_ref, gk_ref, gv_ref, mq_ref, mk_ref, mv_ref,
                 od_ref, og_ref, om_ref, vt_ref, vth_ref, *, n_ctx, lam_init, first_tile_is_ctx):
    tq = dq_ref.shape[0]
    n_keys = dk_ref.shape[0]
    lane = lax.broadcasted_iota(jnp.int32, (tq, LANES), 1)
    lo = lane < HEAD_DIM
    branches = (_Branch("diff", dq_ref, LANES, dk_ref, LANES, dv_ref, N_UNITS, 0, od_ref),
                _Branch("gqa", gq_ref, LANES, gk_ref, LANES, gv_ref, 1, 0, og_ref),
                _Branch("mla", mq_ref, 2 * LANES, mk_ref, 2 * LANES, mv_ref, N_UNITS, 2, om_ref))

    @pl.when(pl.program_id(1) == 0)
    def _():
        for br in branches:
            for i in range(br.n_kv):
                v_t = br.v_ref[:, i * LANES:(i + 1) * LANES].astype(F32).T.astype(BF16)
                if br.kind == "diff":
                    vt_ref[br.vt_slot + i, 0:LANES, :] = v_t
                    vt_ref[br.vt_slot + i, LANES:, :] = jnp.ones((SUM_ROWS, n_keys), BF16)
                else:
                    for h in range(2):
                        slot = br.vt_slot + 2 * i + h
                        vth_ref[slot, 0:HEAD_DIM, :] = v_t[h * HEAD_DIM:(h + 1) * HEAD_DIM]
                        vth_ref[slot, HEAD_DIM:, :] = jnp.ones((SUM_ROWS, n_keys), BF16)

    def key_chunks(nk):
        step = KEY_CHUNK if nk % KEY_CHUNK == 0 else nk
        return [(c, c + step) for c in range(0, nk, step)]

    def unit_queries(br, u):
        q = br.q_ref[:, u * br.q_width:(u + 1) * br.q_width]
        if br.kind == "mla":
            return q[:, :LANES], q[:, LANES:]
        zero = jnp.zeros_like(q)
        return jnp.where(lo, q, zero), jnp.where(lo, zero, q)

    def map_scores(br, u, which, q, nk):
        kv = u % br.n_kv
        k = br.k_ref[0:nk, kv * br.k_width:(kv + 1) * br.k_width]
        if br.kind == "mla":
            k = k[:, :LANES] if which == 0 else k[:, LANES:]
        return lax.dot_general(k, q, (((1,), (1,)), ((), ())), preferred_element_type=F32)

    def chunk_softmax_pv(br, u, w, c0, c1, st):
        m = jnp.max(st, axis=0, keepdims=True)
        p = jnp.exp2((st - m).astype(BF16))
        kv = u % br.n_kv
        vt = vt_ref[br.vt_slot + kv, :, c0:c1] if br.kind == "diff" else vth_ref[br.vt_slot + 2 * kv + w, :, c0:c1]
        return m, _dot(vt, p)

    def merge_chunks(parts):
        if len(parts) == 1:
            ot = parts[0][1]
        else:
            m_all = functools.reduce(jnp.maximum, [m for m, _ in parts])
            ot = sum(acc * jnp.exp2(m - m_all) for m, acc in parts)
        rows = ot.shape[0] - SUM_ROWS
        return ot[0:rows] / ot[rows:rows + 1]

    def write_unit(br, u, oa, ob):
        if br.kind == "diff":
            o = (oa - lam_ref[0:1, 0:1] * ob).T
            o = _rms(o) * gdo_ref[...] * (1.0 - lam_init)
        else:
            o = jnp.concatenate([oa, ob], axis=0).T
        br.o_ref[:, u * LANES:(u + 1) * LANES] = o.astype(BF16)

    def compute(nk):
        chunks = key_chunks(nk)
        maps = [(br, u, w) for br in branches for u in range(N_UNITS) for w in (0, 1)]
        queries = {}

        def scores(br, u, w):
            if (br.kind, u) not in queries:
                queries[(br.kind, u)] = unit_queries(br, u)
            return map_scores(br, u, w, queries[(br.kind, u)][w], nk)

        st = scores(*maps[0])
        first = None
        for t, (br, u, w) in enumerate(maps):
            nxt = scores(*maps[t + 1]) if t + 1 < len(maps) else None
            o = merge_chunks([chunk_softmax_pv(br, u, w, c0, c1, st[c0:c1]) for c0, c1 in chunks])
            if w == 0:
                first = o
            else:
                write_unit(br, u, first, o)
            st = nxt

    if first_tile_is_ctx:
        j = pl.program_id(1)
        pl.when(j == 0)(lambda: compute(n_ctx))
        pl.when(j > 0)(lambda: compute(n_keys))
    else:
        compute(n_keys)


def _attn_call(layer, qkv, lam_vec, gdo, *, n_ctx, lam_init, include_ctx):
    dq, dk, dv, gq, gk, gv, mq, mk, mv = qkv
    b, t, _ = dk.shape
    tq = TOKEN_TILE
    q_off = 0 if include_ctx else n_ctx // tq
    nq = t // tq - q_off
    qspec = lambda a: pl.BlockSpec((None, tq, a.shape[-1]), lambda bi, j: (bi, j + q_off, 0))
    kvspec = lambda a: pl.BlockSpec((None, t, a.shape[-1]), lambda bi, j: (bi, 0, 0))
    ospec = pl.BlockSpec((None, tq, BRANCH_W), lambda bi, j: (bi, j, 0))
    oshape = jax.ShapeDtypeStruct((b, nq * tq, BRANCH_W), BF16)
    return pl.pallas_call(
        functools.partial(_attn_kernel, n_ctx=n_ctx, lam_init=lam_init, first_tile_is_ctx=include_ctx),
        grid=(b, nq),
        in_specs=[qspec(dq), kvspec(dk), kvspec(dv), _layer_spec(lam_vec, layer), _layer_spec(gdo, layer),
                  qspec(gq), kvspec(gk), kvspec(gv), qspec(mq), kvspec(mk), kvspec(mv)],
        out_specs=[ospec] * 3,
        out_shape=[oshape] * 3,
        scratch_shapes=[pltpu.VMEM((N_UNITS, LANES + SUM_ROWS, t), BF16),
                        pltpu.VMEM((2 + 2 * N_UNITS, HALF_ROWS, t), BF16)],
        compiler_params=pltpu.CompilerParams(
            dimension_semantics=("parallel", "arbitrary"), vmem_limit_bytes=VMEM_LIMIT),
        name="attention",
    )(dq, dk, dv, lam_vec, gdo, gq, gk, gv, mq, mk, mv)


def _merge_kernel(*refs, n_tok, n_ctx_tiles):
    (mod_ref, g1_ref, wg_ref, bg_ref, od_ref, og_ref, om_ref,
     wbd_ref, wbg_ref, wbm_ref, wout_ref, xo_ref) = refs[n_tok:]
    x = _read_tokens(refs[:n_tok], n_ctx_tiles)
    d = x.shape[-1]
    hb = _adaln(x, g1_ref[...], mod_ref[0:1, :], mod_ref[1:2, :]).astype(BF16)
    y = None
    for i, (o_ref, wb_ref) in enumerate(((od_ref, wbd_ref), (og_ref, wbg_ref), (om_ref, wbm_ref))):
        sl = slice(i * d, (i + 1) * d)
        gate = jax.nn.sigmoid(_dot(hb, wg_ref[:, sl]) + bg_ref[:, sl])
        term = gate * _dot(o_ref[...], wb_ref[...])
        y = term if y is None else y + term
    out = _dot(y.astype(BF16), wout_ref[...])
    xo_ref[...] = x + mod_ref[2:3, :] * out


def _merge_call(layer, tokens, modarr, g1, wg, bg, od, og, om, wbd, wbg, wbm, wout, *, x_off, n_ctx_tiles):
    b, d = tokens[0].shape[0], tokens[0].shape[-1]
    tm = TOKEN_TILE
    nt = od.shape[1] // tm
    tok = lambda w: pl.BlockSpec((None, tm, w), lambda bi, ti: (bi, ti, 0))
    return pl.pallas_call(
        functools.partial(_merge_kernel, n_tok=len(tokens), n_ctx_tiles=n_ctx_tiles),
        grid=(b, nt),
        in_specs=_token_specs(tokens, x_off, n_ctx_tiles) + [_mod_spec(modarr, layer, x_off, n_ctx_tiles)] + [
            _layer_spec(a, layer) for a in (g1, wg, bg)] + [tok(BRANCH_W)] * 3 + [
            _layer_spec(a, layer) for a in (wbd, wbg, wbm, wout)],
        out_specs=tok(d),
        out_shape=jax.ShapeDtypeStruct((b, nt * tm, d), F32),
        compiler_params=pltpu.CompilerParams(
            dimension_semantics=("parallel", "arbitrary"), vmem_limit_bytes=VMEM_LIMIT),
        name="branch_merge",
    )(*tokens, modarr, g1, wg, bg, od, og, om, wbd, wbg, wbm, wout)


def _ffn_kernel(*refs, final):
    if final:
        x_ref, mod_ref, g2_ref, win_ref, wout_ref, gf_ref, o_ref = refs
    else:
        x_ref, mod_ref, g2_ref, win_ref, wout_ref, o_ref = refs
    x = x_ref[...]
    hidden = wout_ref.shape[0]
    hb = _adaln(x, g2_ref[...], mod_ref[3:4, :], mod_ref[4:5, :]).astype(BF16)
    g = _dot(hb, win_ref[:, :hidden])
    u = _dot(hb, win_ref[:, hidden:])
    a = (g * jax.nn.sigmoid(g) * u).astype(BF16)
    xn = x + mod_ref[5:6, :] * _dot(a, wout_ref[...])
    if final:
        xn = _rms(xn) * gf_ref[...]
    o_ref[...] = xn


def _ffn_call(layer, xin, modarr, g2, win, wout, gf, *, pos_off, n_ctx_tiles):
    b, t, d = xin.shape
    tm = TOKEN_TILE
    tok = pl.BlockSpec((None, tm, d), lambda bi, ti: (bi, ti, 0))
    final = gf is not None
    in_specs = [tok, _mod_spec(modarr, layer, pos_off, n_ctx_tiles)] + [
        _layer_spec(a, layer) for a in (g2, win, wout)]
    args = [xin, modarr, g2, win, wout]
    if final:
        in_specs.append(_const_spec((1, d)))
        args.append(gf)
    return pl.pallas_call(
        functools.partial(_ffn_kernel, final=final),
        grid=(b, t // tm),
        in_specs=in_specs,
        out_specs=tok,
        out_shape=jax.ShapeDtypeStruct((b, t, d), F32),
        compiler_params=pltpu.CompilerParams(
            dimension_semantics=("parallel", "arbitrary"), vmem_limit_bytes=VMEM_LIMIT),
        name="swiglu",
    )(*args)


def kernel(x, c, ctx, c_ctx, w_mod, b_mod, g_norm1, w_in, b_gate, lam_q1, lam_k1, lam_q2, lam_k2, g_diff_out, g_gqa_q, g_gqa_k, g_mla_q, w_mla_uq, g_mla_kv, w_mla_ukv, w_br_diff, w_br_gqa, w_br_mla, w_out, g_norm2, w_ffn_in, w_ffn_out, g_final):
    b, seq, d = x.shape
    n_ctx = ctx.shape[1]
    depth = w_in.shape[0]
    assert n_ctx % TOKEN_TILE == 0 and seq % TOKEN_TILE == 0 and seq % GRID_W == 0
    n_ctx_tiles = n_ctx // TOKEN_TILE

    mod_rows = 32
    cc = jnp.zeros((mod_rows, d), F32).at[:b].set(c).at[b].set(c_ctx)
    mods = _mod_call(cc, w_mod, b_mod)
    mod_lat = mods[:, :b].reshape(depth, b, N_MOD, d)
    mod_ctx = jnp.broadcast_to(mods[:, b].reshape(depth, 1, N_MOD, d), (depth, b, N_MOD, d))
    modarr = jnp.stack([mod_ctx, mod_lat], axis=2)
    tables = _rope_tables(seq, n_ctx)

    lam_inits = [_lambda_init(l) for l in range(depth)]
    lam = (jnp.exp(jnp.sum(lam_q1 * lam_k1, axis=-1)) - jnp.exp(jnp.sum(lam_q2 * lam_k2, axis=-1))
           + jnp.asarray(lam_inits, F32))
    lam_vec = jnp.broadcast_to(lam.reshape(depth, 1, 1), (depth, 1, LANES))

    def row(a):
        return a.reshape(depth, 1, -1)

    w1, wg, wuq, wukv, wbg = _prep_weights(w_in, w_mla_uq, w_mla_ukv, w_br_gqa)
    wbd, wbm, wout = w_br_diff.astype(BF16), w_br_mla.astype(BF16), w_out.astype(BF16)
    wfi, wfo = w_ffn_in.astype(BF16), w_ffn_out.astype(BF16)
    g1, g2, bg = row(g_norm1), row(g_norm2), row(b_gate)
    gq_g, gk_g = row(jnp.tile(g_gqa_q, (1, 2))), row(jnp.tile(g_gqa_k, (1, 2)))
    gmq, gmkv, gdo = row(g_mla_q), row(g_mla_kv), row(g_diff_out)

    tokens = (ctx, x)
    for l in range(depth):
        last = l == depth - 1
        qkv = _proj_call(l, tokens, modarr, g1, w1, wuq, wukv, gq_g, gk_g, gmq, gmkv, tables, n_ctx_tiles)
        od, og, om = _attn_call(l, qkv, lam_vec, gdo, n_ctx=n_ctx, lam_init=lam_inits[l], include_ctx=not last)

        x_off = n_ctx_tiles if last else 0
        xmid = _merge_call(l, tokens, modarr, g1, wg, bg, od, og, om, wbd, wbg, wbm, wout,
                           x_off=x_off, n_ctx_tiles=n_ctx_tiles)
        xout = _ffn_call(l, xmid, modarr, g2, wfi, wfo, g_final.reshape(1, d) if last else None,
                         pos_off=x_off, n_ctx_tiles=n_ctx_tiles)
        tokens = (xout,)
    return xout
```

```python
import collections
import functools
import math

import jax
import jax.numpy as jnp
from jax import lax
from jax.experimental import pallas as pl
from jax.experimental.pallas import tpu as pltpu

F32 = jnp.float32
BF16 = jnp.bfloat16

GRID_W = 64
ROPE_THETA = 10000.0
EPS = 1e-6
N_MOD = 6

DIFF_HEADS = 4
HEAD_DIM = 64
GQA_KV_HEADS = 2
GQA_GROUP = 4
MLA_HEADS = 8
MLA_Q_LORA = 384
MLA_KV_LORA = 256
MLA_NOPE = 64
MLA_ROPE = 32
MLA_QK = MLA_NOPE + MLA_ROPE
MLA_V = 64
ROT64 = HEAD_DIM // 4
ROT32 = MLA_ROPE // 4
BRANCH_W = 512
N_UNITS = 4

LANES = 128
KEY_CHUNK = 256
SUM_ROWS = 16
SCORE_LOOKAHEAD = 3
LOG2E = math.log2(math.e)
TOKEN_TILE = 256
VMEM_LIMIT = 56 * 1024 * 1024

OFF_DQ, OFF_DK, OFF_DV, OFF_GQ, OFF_GK, OFF_GV = 0, 512, 1024, 1536, 2048, 2176
OFF_MCQ, OFF_MCKV, OFF_MKR, W1_WIDTH = 2304, 2688, 2944, 3072


def _lambda_init(layer):
    return 0.8 - 0.6 * math.exp(-0.3 * layer)


_IN_GQ, _IN_GK, _IN_MKR, _IN_GATE = 1536, 2048, 2944, 2976


def _pair_heads(a, axis):
    shape = a.shape
    a = a.reshape(shape[:axis] + (GQA_KV_HEADS, GQA_GROUP, HEAD_DIM) + shape[axis + 1:])
    return jnp.swapaxes(a, axis, axis + 1).reshape(shape)


def _prep_weights(w_in, w_mla_uq, w_mla_ukv, w_br_gqa):
    depth = w_in.shape[0]
    mkr = jnp.pad(w_in[:, :, _IN_MKR:_IN_GATE], ((0, 0), (0, 0), (MLA_NOPE, LANES - MLA_QK)))
    w1 = jnp.concatenate([w_in[:, :, :_IN_GQ], _pair_heads(w_in[:, :, _IN_GQ:_IN_GK], 2),
                          w_in[:, :, _IN_GK:_IN_MKR], mkr], axis=2).astype(BF16)
    wg = w_in[:, :, _IN_GATE:].astype(BF16)
    uq = w_mla_uq.reshape(depth, MLA_Q_LORA, MLA_HEADS, MLA_QK)
    uq = jnp.pad(uq, ((0, 0), (0, 0), (0, 0), (0, LANES - MLA_QK)))
    uq = uq.reshape(depth, MLA_Q_LORA, MLA_HEADS * LANES).astype(BF16)
    ukv = w_mla_ukv.reshape(depth, MLA_KV_LORA, MLA_HEADS, MLA_NOPE + MLA_V)
    ukn = jnp.pad(ukv[..., :MLA_NOPE], ((0, 0), (0, 0), (0, 0), (0, LANES - MLA_NOPE)))
    ukn = ukn.reshape(depth, MLA_KV_LORA, -1)
    uv = ukv[..., MLA_NOPE:].reshape(depth, MLA_KV_LORA, -1)
    ukv2 = jnp.concatenate([ukn, uv], axis=2).astype(BF16)
    wbg = _pair_heads(w_br_gqa, 1).astype(BF16)
    return w1, wg, uq, ukv2, wbg


def _rope_tables(seq, n_ctx):
    rows = seq // GRID_W
    row = jnp.repeat(jnp.arange(rows, dtype=jnp.int32), GRID_W).astype(F32)
    col = jnp.tile(jnp.arange(GRID_W, dtype=jnp.int32), rows).astype(F32)

    def table(dim):
        a = dim // 2
        freqs = ROPE_THETA ** (-jnp.arange(0, a, 2, dtype=F32) / a)
        ang_r, ang_c = row[:, None] * freqs, col[:, None] * freqs
        cos = jnp.concatenate([jnp.cos(ang_r), jnp.cos(ang_r), jnp.cos(ang_c), jnp.cos(ang_c)], axis=-1)
        sin = jnp.concatenate([-jnp.sin(ang_r), jnp.sin(ang_r), -jnp.sin(ang_c), jnp.sin(ang_c)], axis=-1)
        return cos, sin

    c64, s64 = table(HEAD_DIM)
    c64, s64 = jnp.tile(c64, (1, 2)), jnp.tile(s64, (1, 2))
    c32, s32 = table(MLA_ROPE)
    pad = ((0, 0), (MLA_NOPE, LANES - MLA_QK))
    c32 = jnp.pad(c32, pad, constant_values=1.0)
    s32 = jnp.pad(s32, pad)
    ctx_pad = ((n_ctx, 0), (0, 0))
    return (jnp.pad(c64, ctx_pad, constant_values=1.0), jnp.pad(s64, ctx_pad),
            jnp.pad(c32, ctx_pad, constant_values=1.0), jnp.pad(s32, ctx_pad))


def _rms(x):
    return x * lax.rsqrt(jnp.mean(x * x, axis=-1, keepdims=True) + EPS)


def _adaln(x, g, shift, scale):
    return (_rms(x) * g) * (1.0 + scale) + shift


def _rope(z, cos, sin, is_x1, shift):
    zb = z.astype(BF16)
    partner = jnp.where(is_x1, pltpu.roll(zb, LANES - shift, 1), pltpu.roll(zb, shift, 1))
    return z * cos + partner.astype(F32) * sin


def _half_rms(z, lo, g):
    sq = z * z
    zero = jnp.zeros_like(sq)
    s_lo = jnp.sum(jnp.where(lo, sq, zero), axis=-1, keepdims=True)
    s_hi = jnp.sum(jnp.where(lo, zero, sq), axis=-1, keepdims=True)
    ms = jnp.where(lo, s_lo, s_hi) * (1.0 / HEAD_DIM)
    return z * lax.rsqrt(ms + EPS) * g


def _dot(a, b):
    return jnp.dot(a, b, preferred_element_type=F32)


def _const_spec(shape):
    return pl.BlockSpec(shape, lambda *_: (0,) * len(shape))


def _layer_spec(stacked, layer):
    shape = stacked.shape[1:]
    return pl.BlockSpec((None,) + shape, lambda *_: (layer,) + (0,) * len(shape))


def _mod_spec(modarr, layer, pos_off, n_ctx_tiles):
    return pl.BlockSpec((None, None, None) + modarr.shape[3:],
                        lambda bi, ti: (layer, bi, (ti + pos_off >= n_ctx_tiles).astype(jnp.int32), 0, 0))


def _token_specs(tokens, tile_off, n_ctx_tiles):
    tm, d = TOKEN_TILE, tokens[0].shape[-1]
    if len(tokens) == 1:
        return [pl.BlockSpec((None, tm, d), lambda bi, ti: (bi, ti + tile_off, 0))]
    assert tile_off == 0
    return [pl.BlockSpec((None, tm, d), lambda bi, ti: (bi, jnp.minimum(ti, n_ctx_tiles - 1), 0)),
            pl.BlockSpec((None, tm, d), lambda bi, ti: (bi, jnp.maximum(ti - n_ctx_tiles, 0), 0))]


def _read_tokens(tok_refs, n_ctx_tiles):
    if len(tok_refs) == 1:
        return tok_refs[0][...]
    ctx_ref, x_ref = tok_refs
    return jnp.where(pl.program_id(1) < n_ctx_tiles, ctx_ref[...], x_ref[...])


def _mod_kernel(c_ref, w_ref, b_ref, o_ref):
    cc = c_ref[...]
    a = (cc * jax.nn.sigmoid(cc)).astype(BF16)
    o_ref[...] = _dot(a, w_ref[...].astype(BF16)) + b_ref[...]


def _mod_call(cc, w_mod, b_mod):
    depth, d, n = w_mod.shape
    tn = 1536
    rows = cc.shape[0]
    return pl.pallas_call(
        _mod_kernel,
        grid=(depth, n // tn),
        in_specs=[
            pl.BlockSpec((rows, d), lambda l, j: (0, 0)),
            pl.BlockSpec((None, d, tn), lambda l, j: (l, 0, j)),
            pl.BlockSpec((None, 1, tn), lambda l, j: (l, 0, j)),
        ],
        out_specs=pl.BlockSpec((None, rows, tn), lambda l, j: (l, 0, j)),
        out_shape=jax.ShapeDtypeStruct((depth, rows, n), F32),
        compiler_params=pltpu.CompilerParams(vmem_limit_bytes=VMEM_LIMIT),
        name="mod_vectors",
    )(cc, w_mod, b_mod.reshape(depth, 1, n))


def _proj_kernel(*refs, n_tok, n_ctx_tiles):
    (mod_ref, g1_ref, w1_ref, wuq_ref, wukv_ref, gq_g_ref, gk_g_ref, gmq_ref, gmkv_ref,
     c64_ref, s64_ref, c32_ref, s32_ref,
     dq_o, dk_o, dv_o, gq_o, gk_o, gv_o, mq_o, mk_o, mv_o) = refs[n_tok:]
    x = _read_tokens(refs[:n_tok], n_ctx_tiles)
    hb = _adaln(x, g1_ref[...], mod_ref[0:1, :], mod_ref[1:2, :]).astype(BF16)
    tm = x.shape[0]
    lane = lax.broadcasted_iota(jnp.int32, (tm, LANES), 1)
    lo = lane < HEAD_DIM
    x1_64 = (lane & ROT64) == 0
    x1_32 = (lane & ROT32) == 0
    c64, s64 = c64_ref[...], s64_ref[...]
    c32, s32 = c32_ref[...], s32_ref[...]
    qk_scale = HEAD_DIM ** -0.5 * LOG2E

    def seg(off, width):
        return _dot(hb, w1_ref[:, off:off + width])

    def blocks(z):
        return [(slice(i * LANES, (i + 1) * LANES), z[:, i * LANES:(i + 1) * LANES]) for i in range(z.shape[1] // LANES)]

    zm = seg(OFF_MCQ, W1_WIDTH - OFF_MCQ)
    mcq = (_rms(zm[:, :MLA_Q_LORA]) * gmq_ref[...]).astype(BF16)
    mckv = (_rms(zm[:, MLA_Q_LORA:MLA_Q_LORA + MLA_KV_LORA]) * gmkv_ref[...]).astype(BF16)
    mkr = _rope(zm[:, MLA_Q_LORA + MLA_KV_LORA:], c32, s32, x1_32, ROT32)

    for sl, z in blocks(seg(OFF_DQ, BRANCH_W)):
        dq_o[:, sl] = (_rope(z, c64, s64, x1_64, ROT64) * qk_scale).astype(BF16)
    for sl, z in blocks(seg(OFF_DK, BRANCH_W)):
        dk_o[:, sl] = _rope(z, c64, s64, x1_64, ROT64).astype(BF16)
    dv_o[...] = seg(OFF_DV, BRANCH_W).astype(BF16)
    for sl, z in blocks(seg(OFF_GQ, BRANCH_W)):
        gq = _half_rms(z, lo, gq_g_ref[...])
        gq_o[:, sl] = (_rope(gq, c64, s64, x1_64, ROT64) * qk_scale).astype(BF16)
    gkv = seg(OFF_GK, 2 * LANES)
    gk = _half_rms(gkv[:, :LANES], lo, gk_g_ref[...])
    gk_o[...] = _rope(gk, c64, s64, x1_64, ROT64).astype(BF16)
    gv_o[...] = gkv[:, LANES:].astype(BF16)

    mla_scale = MLA_QK ** -0.5 * LOG2E
    for sl, z in blocks(_dot(mcq, wuq_ref[...])):
        mq_o[:, sl] = (_rope(z, c32, s32, x1_32, ROT32) * mla_scale).astype(BF16)
    mkv = _dot(mckv, wukv_ref[...])
    for sl, z in blocks(mkv[:, :MLA_HEADS * LANES]):
        mk_o[:, sl] = (z + mkr).astype(BF16)
    mv_o[...] = mkv[:, MLA_HEADS * LANES:].astype(BF16)


def _proj_call(layer, tokens, modarr, g1, w1, wuq, wukv, gq_g, gk_g, gmq, gmkv, tables, n_ctx_tiles):
    b = tokens[0].shape[0]
    t = sum(a.shape[1] for a in tokens)
    tm = TOKEN_TILE
    tok = lambda w: pl.BlockSpec((None, tm, w), lambda bi, ti: (bi, ti, 0))
    tab = pl.BlockSpec((tm, LANES), lambda bi, ti: (ti, 0))
    widths = (BRANCH_W, BRANCH_W, BRANCH_W, BRANCH_W, LANES, LANES, MLA_HEADS * LANES, MLA_HEADS * LANES, BRANCH_W)
    return pl.pallas_call(
        functools.partial(_proj_kernel, n_tok=len(tokens), n_ctx_tiles=n_ctx_tiles),
        grid=(b, t // tm),
        in_specs=_token_specs(tokens, 0, n_ctx_tiles) + [_mod_spec(modarr, layer, 0, n_ctx_tiles)] + [
            _layer_spec(a, layer) for a in (g1, w1, wuq, wukv, gq_g, gk_g, gmq, gmkv)] + [tab, tab, tab, tab],
        out_specs=[tok(w) for w in widths],
        out_shape=[jax.ShapeDtypeStruct((b, t, w), BF16) for w in widths],
        compiler_params=pltpu.CompilerParams(
            dimension_semantics=("parallel", "arbitrary"), vmem_limit_bytes=VMEM_LIMIT),
        name="input_projection",
    )(*tokens, modarr, g1, w1, wuq, wukv, gq_g, gk_g, gmq, gmkv, *tables)


_Branch = collections.namedtuple("_Branch", "kind q_ref q_width k_ref k_width v_ref n_kv vt_slot o_ref")


def _attn_kernel(dq_ref, dk_ref, dv_ref, lam_ref, gdo_ref, gq_ref, gk_ref, gv_ref, mq_ref, mk_ref, mv_ref,
                 od_ref, og_ref, om_ref, vt_ref, *, n_ctx, lam_init, first_tile_is_ctx):
    tq = dq_ref.shape[0]
    n_keys = dk_ref.shape[0]
    lane = lax.broadcasted_iota(jnp.int32, (tq, LANES), 1)
    lo = lane < HEAD_DIM
    branches = (_Branch("diff", dq_ref, LANES, dk_ref, LANES, dv_ref, N_UNITS, 0, od_ref),
                _Branch("gqa", gq_ref, LANES, gk_ref, LANES, gv_ref, 1, N_UNITS, og_ref),
                _Branch("mla", mq_ref, 2 * LANES, mk_ref, 2 * LANES, mv_ref, N_UNITS, N_UNITS + 1, om_ref))

    @pl.when(pl.program_id(1) == 0)
    def _():
        for br in branches:
            for i in range(br.n_kv):
                vt_ref[br.vt_slot + i, 0:LANES, :] = br.v_ref[:, i * LANES:(i + 1) * LANES].astype(F32).T.astype(BF16)
                vt_ref[br.vt_slot + i, LANES:, :] = jnp.ones((SUM_ROWS, n_keys), BF16)

    def key_chunks(nk):
        step = KEY_CHUNK if nk % KEY_CHUNK == 0 else nk
        return [(c, c + step) for c in range(0, nk, step)]

    def unit_queries(br, u):
        q = br.q_ref[:, u * br.q_width:(u + 1) * br.q_width]
        if br.kind == "mla":
            return q[:, :LANES], q[:, LANES:]
        zero = jnp.zeros_like(q)
        return jnp.where(lo, q, zero), jnp.where(lo, zero, q)

    def map_scores(br, u, which, q, nk):
        kv = u % br.n_kv
        k = br.k_ref[0:nk, kv * br.k_width:(kv + 1) * br.k_width]
        if br.kind == "mla":
            k = k[:, :LANES] if which == 0 else k[:, LANES:]
        return lax.dot_general(k, q, (((1,), (1,)), ((), ())), preferred_element_type=F32)

    def chunk_softmax_pv(br, u, c0, c1, st):
        m = jnp.max(st, axis=0, keepdims=True)
        p = jnp.exp2((st - m).astype(BF16))
        return m, _dot(vt_ref[br.vt_slot + u % br.n_kv, :, c0:c1], p)

    def merge_chunks(parts):
        if len(parts) == 1:
            ot = parts[0][1]
        else:
            m_all = functools.reduce(jnp.maximum, [m for m, _ in parts])
            ot = sum(acc * jnp.exp2(m - m_all) for m, acc in parts)
        return ot[0:LANES] / ot[LANES:LANES + 1]

    def write_unit(br, u, oa, ob):
        if br.kind == "diff":
            o = (oa - lam_ref[0:1, 0:1] * ob).T
            o = _rms(o) * gdo_ref[...] * (1.0 - lam_init)
        else:
            row = lax.broadcasted_iota(jnp.int32, oa.shape, 0)
            o = jnp.where(row < HEAD_DIM, oa, ob).T
        br.o_ref[:, u * LANES:(u + 1) * LANES] = o.astype(BF16)

    def compute(nk):
        chunks = key_chunks(nk)
        maps = [(br, u, w) for br in branches for u in range(N_UNITS) for w in (0, 1)]
        queries = {}

        def scores(br, u, w):
            if (br.kind, u) not in queries:
                queries[(br.kind, u)] = unit_queries(br, u)
            return map_scores(br, u, w, queries[(br.kind, u)][w], nk)

        pend = [scores(*maps[i]) for i in range(min(SCORE_LOOKAHEAD, len(maps)))]
        first = None
        for t, (br, u, w) in enumerate(maps):
            if t + SCORE_LOOKAHEAD < len(maps):
                pend.append(scores(*maps[t + SCORE_LOOKAHEAD]))
            st = pend.pop(0)
            o = merge_chunks([chunk_softmax_pv(br, u, c0, c1, st[c0:c1]) for c0, c1 in chunks])
            if w == 0:
                first = o
            else:
                write_unit(br, u, first, o)

    if first_tile_is_ctx:
        j = pl.program_id(1)
        pl.when(j == 0)(lambda: compute(n_ctx))
        pl.when(j > 0)(lambda: compute(n_keys))
    else:
        compute(n_keys)


def _attn_call(layer, qkv, lam_vec, gdo, *, n_ctx, lam_init, include_ctx):
    dq, dk, dv, gq, gk, gv, mq, mk, mv = qkv
    b, t, _ = dk.shape
    tq = TOKEN_TILE
    q_off = 0 if include_ctx else n_ctx // tq
    nq = t // tq - q_off
    qspec = lambda a: pl.BlockSpec((None, tq, a.shape[-1]), lambda bi, j: (bi, j + q_off, 0))
    kvspec = lambda a: pl.BlockSpec((None, t, a.shape[-1]), lambda bi, j: (bi, 0, 0))
    ospec = pl.BlockSpec((None, tq, BRANCH_W), lambda bi, j: (bi, j, 0))
    oshape = jax.ShapeDtypeStruct((b, nq * tq, BRANCH_W), BF16)
    return pl.pallas_call(
        functools.partial(_attn_kernel, n_ctx=n_ctx, lam_init=lam_init, first_tile_is_ctx=include_ctx),
        grid=(b, nq),
        in_specs=[qspec(dq), kvspec(dk), kvspec(dv), _layer_spec(lam_vec, layer), _layer_spec(gdo, layer),
                  qspec(gq), kvspec(gk), kvspec(gv), qspec(mq), kvspec(mk), kvspec(mv)],
        out_specs=[ospec] * 3,
        out_shape=[oshape] * 3,
        scratch_shapes=[pltpu.VMEM((2 * N_UNITS + 1, LANES + SUM_ROWS, t), BF16)],
        compiler_params=pltpu.CompilerParams(
            dimension_semantics=("parallel", "arbitrary"), vmem_limit_bytes=VMEM_LIMIT),
        name="attention",
    )(dq, dk, dv, lam_vec, gdo, gq, gk, gv, mq, mk, mv)


def _merge_kernel(*refs, n_tok, n_ctx_tiles):
    (mod_ref, g1_ref, wg_ref, bg_ref, od_ref, og_ref, om_ref,
     wbd_ref, wbg_ref, wbm_ref, wout_ref, xo_ref) = refs[n_tok:]
    x = _read_tokens(refs[:n_tok], n_ctx_tiles)
    d = x.shape[-1]
    hb = _adaln(x, g1_ref[...], mod_ref[0:1, :], mod_ref[1:2, :]).astype(BF16)
    y = None
    for i, (o_ref, wb_ref) in enumerate(((od_ref, wbd_ref), (og_ref, wbg_ref), (om_ref, wbm_ref))):
        sl = slice(i * d, (i + 1) * d)
        gate = jax.nn.sigmoid(_dot(hb, wg_ref[:, sl]) + bg_ref[:, sl])
        term = gate * _dot(o_ref[...], wb_ref[...])
        y = term if y is None else y + term
    out = _dot(y.astype(BF16), wout_ref[...])
    xo_ref[...] = x + mod_ref[2:3, :] * out


def _merge_call(layer, tokens, modarr, g1, wg, bg, od, og, om, wbd, wbg, wbm, wout, *, x_off, n_ctx_tiles):
    b, d = tokens[0].shape[0], tokens[0].shape[-1]
    tm = TOKEN_TILE
    nt = od.shape[1] // tm
    tok = lambda w: pl.BlockSpec((None, tm, w), lambda bi, ti: (bi, ti, 0))
    return pl.pallas_call(
        functools.partial(_merge_kernel, n_tok=len(tokens), n_ctx_tiles=n_ctx_tiles),
        grid=(b, nt),
        in_specs=_token_specs(tokens, x_off, n_ctx_tiles) + [_mod_spec(modarr, layer, x_off, n_ctx_tiles)] + [
            _layer_spec(a, layer) for a in (g1, wg, bg)] + [tok(BRANCH_W)] * 3 + [
            _layer_spec(a, layer) for a in (wbd, wbg, wbm, wout)],
        out_specs=tok(d),
        out_shape=jax.ShapeDtypeStruct((b, nt * tm, d), F32),
        compiler_params=pltpu.CompilerParams(
            dimension_semantics=("parallel", "arbitrary"), vmem_limit_bytes=VMEM_LIMIT),
        name="branch_merge",
    )(*tokens, modarr, g1, wg, bg, od, og, om, wbd, wbg, wbm, wout)


def _ffn_kernel(*refs, final):
    if final:
        x_ref, mod_ref, g2_ref, win_ref, wout_ref, gf_ref, o_ref = refs
    else:
        x_ref, mod_ref, g2_ref, win_ref, wout_ref, o_ref = refs
    x = x_ref[...]
    hidden = wout_ref.shape[0]
    hb = _adaln(x, g2_ref[...], mod_ref[3:4, :], mod_ref[4:5, :]).astype(BF16)
    g = _dot(hb, win_ref[:, :hidden])
    u = _dot(hb, win_ref[:, hidden:])
    a = (g * jax.nn.sigmoid(g) * u).astype(BF16)
    xn = x + mod_ref[5:6, :] * _dot(a, wout_ref[...])
    if final:
        xn = _rms(xn) * gf_ref[...]
    o_ref[...] = xn


def _ffn_call(layer, xin, modarr, g2, win, wout, gf, *, pos_off, n_ctx_tiles):
    b, t, d = xin.shape
    tm = TOKEN_TILE
    tok = pl.BlockSpec((None, tm, d), lambda bi, ti: (bi, ti, 0))
    final = gf is not None
    in_specs = [tok, _mod_spec(modarr, layer, pos_off, n_ctx_tiles)] + [
        _layer_spec(a, layer) for a in (g2, win, wout)]
    args = [xin, modarr, g2, win, wout]
    if final:
        in_specs.append(_const_spec((1, d)))
        args.append(gf)
    return pl.pallas_call(
        functools.partial(_ffn_kernel, final=final),
        grid=(b, t // tm),
        in_specs=in_specs,
        out_specs=tok,
        out_shape=jax.ShapeDtypeStruct((b, t, d), F32),
        compiler_params=pltpu.CompilerParams(
            dimension_semantics=("parallel", "arbitrary"), vmem_limit_bytes=VMEM_LIMIT),
        name="swiglu",
    )(*args)


def kernel(x, c, ctx, c_ctx, w_mod, b_mod, g_norm1, w_in, b_gate, lam_q1, lam_k1, lam_q2, lam_k2, g_diff_out, g_gqa_q, g_gqa_k, g_mla_q, w_mla_uq, g_mla_kv, w_mla_ukv, w_br_diff, w_br_gqa, w_br_mla, w_out, g_norm2, w_ffn_in, w_ffn_out, g_final):
    b, seq, d = x.shape
    n_ctx = ctx.shape[1]
    depth = w_in.shape[0]
    assert n_ctx % TOKEN_TILE == 0 and seq % TOKEN_TILE == 0 and seq % GRID_W == 0
    n_ctx_tiles = n_ctx // TOKEN_TILE

    mod_rows = 32
    cc = jnp.zeros((mod_rows, d), F32).at[:b].set(c).at[b].set(c_ctx)
    mods = _mod_call(cc, w_mod, b_mod)
    mod_lat = mods[:, :b].reshape(depth, b, N_MOD, d)
    mod_ctx = jnp.broadcast_to(mods[:, b].reshape(depth, 1, N_MOD, d), (depth, b, N_MOD, d))
    modarr = jnp.stack([mod_ctx, mod_lat], axis=2)
    tables = _rope_tables(seq, n_ctx)

    lam_inits = [_lambda_init(l) for l in range(depth)]
    lam = (jnp.exp(jnp.sum(lam_q1 * lam_k1, axis=-1)) - jnp.exp(jnp.sum(lam_q2 * lam_k2, axis=-1))
           + jnp.asarray(lam_inits, F32))
    lam_vec = jnp.broadcast_to(lam.reshape(depth, 1, 1), (depth, 1, LANES))

    def row(a):
        return a.reshape(depth, 1, -1)

    w1, wg, wuq, wukv, wbg = _prep_weights(w_in, w_mla_uq, w_mla_ukv, w_br_gqa)
    wbd, wbm, wout = w_br_diff.astype(BF16), w_br_mla.astype(BF16), w_out.astype(BF16)
    wfi, wfo = w_ffn_in.astype(BF16), w_ffn_out.astype(BF16)
    g1, g2, bg = row(g_norm1), row(g_norm2), row(b_gate)
    gq_g, gk_g = row(jnp.tile(g_gqa_q, (1, 2))), row(jnp.tile(g_gqa_k, (1, 2)))
    gmq, gmkv, gdo = row(g_mla_q), row(g_mla_kv), row(g_diff_out)

    tokens = (ctx, x)
    for l in range(depth):
        last = l == depth - 1
        qkv = _proj_call(l, tokens, modarr, g1, w1, wuq, wukv, gq_g, gk_g, gmq, gmkv, tables, n_ctx_tiles)
        od, og, om = _attn_call(l, qkv, lam_vec, gdo, n_ctx=n_ctx, lam_init=lam_inits[l], include_ctx=not last)

        x_off = n_ctx_tiles if last else 0
        xmid = _merge_call(l, tokens, modarr, g1, wg, bg, od, og, om, wbd, wbg, wbm, wout,
                           x_off=x_off, n_ctx_tiles=n_ctx_tiles)
        xout = _ffn_call(l, xmid, modarr, g2, wfi, wfo, g_final.reshape(1, d) if last else None,
                         pos_off=x_off, n_ctx_tiles=n_ctx_tiles)
        tokens = (xout,)
    return xout
```

```python
import collections
import functools
import math

import jax
import jax.numpy as jnp
from jax import lax
from jax.experimental import pallas as pl
from jax.experimental.pallas import tpu as pltpu

F32 = jnp.float32
BF16 = jnp.bfloat16

GRID_W = 64
ROPE_THETA = 10000.0
EPS = 1e-6
N_MOD = 6

DIFF_HEADS = 4
HEAD_DIM = 64
GQA_KV_HEADS = 2
GQA_GROUP = 4
MLA_HEADS = 8
MLA_Q_LORA = 384
MLA_KV_LORA = 256
MLA_NOPE = 64
MLA_ROPE = 32
MLA_QK = MLA_NOPE + MLA_ROPE
MLA_V = 64
ROT64 = HEAD_DIM // 4
ROT32 = MLA_ROPE // 4
BRANCH_W = 512
N_UNITS = 4

LANES = 128
KEY_CHUNK = 256
SUM_ROWS = 16
SCORE_LOOKAHEAD = 3
LOG2E = math.log2(math.e)
TOKEN_TILE = 256
VMEM_LIMIT = 56 * 1024 * 1024

OFF_DQ, OFF_DK, OFF_DV, OFF_GQ, OFF_GK, OFF_GV = 0, 512, 1024, 1536, 2048, 2176
OFF_MCQ, OFF_MCKV, OFF_MKR, W1_WIDTH = 2304, 2688, 2944, 3072


def _lambda_init(layer):
    return 0.8 - 0.6 * math.exp(-0.3 * layer)


_IN_GQ, _IN_GK, _IN_MKR, _IN_GATE = 1536, 2048, 2944, 2976


def _pair_heads(a, axis):
    shape = a.shape
    a = a.reshape(shape[:axis] + (GQA_KV_HEADS, GQA_GROUP, HEAD_DIM) + shape[axis + 1:])
    return jnp.swapaxes(a, axis, axis + 1).reshape(shape)


def _prep_weights(w_in, w_mla_uq, w_mla_ukv, w_br_gqa):
    depth = w_in.shape[0]
    mkr = jnp.pad(w_in[:, :, _IN_MKR:_IN_GATE], ((0, 0), (0, 0), (MLA_NOPE, LANES - MLA_QK)))
    w1 = jnp.concatenate([w_in[:, :, :_IN_GQ], _pair_heads(w_in[:, :, _IN_GQ:_IN_GK], 2),
                          w_in[:, :, _IN_GK:_IN_MKR], mkr], axis=2).astype(BF16)
    wg = w_in[:, :, _IN_GATE:].astype(BF16)
    uq = w_mla_uq.reshape(depth, MLA_Q_LORA, MLA_HEADS, MLA_QK)
    uq = jnp.pad(uq, ((0, 0), (0, 0), (0, 0), (0, LANES - MLA_QK)))
    uq = uq.reshape(depth, MLA_Q_LORA, MLA_HEADS * LANES).astype(BF16)
    ukv = w_mla_ukv.reshape(depth, MLA_KV_LORA, MLA_HEADS, MLA_NOPE + MLA_V)
    ukn = jnp.pad(ukv[..., :MLA_NOPE], ((0, 0), (0, 0), (0, 0), (0, LANES - MLA_NOPE)))
    ukn = ukn.reshape(depth, MLA_KV_LORA, -1)
    uv = ukv[..., MLA_NOPE:].reshape(depth, MLA_KV_LORA, -1)
    ukv2 = jnp.concatenate([ukn, uv], axis=2).astype(BF16)
    wbg = _pair_heads(w_br_gqa, 1).astype(BF16)
    return w1, wg, uq, ukv2, wbg


def _rope_tables(seq, n_ctx):
    rows = seq // GRID_W
    row = jnp.repeat(jnp.arange(rows, dtype=jnp.int32), GRID_W).astype(F32)
    col = jnp.tile(jnp.arange(GRID_W, dtype=jnp.int32), rows).astype(F32)

    def table(dim):
        a = dim // 2
        freqs = ROPE_THETA ** (-jnp.arange(0, a, 2, dtype=F32) / a)
        ang_r, ang_c = row[:, None] * freqs, col[:, None] * freqs
        cos = jnp.concatenate([jnp.cos(ang_r), jnp.cos(ang_r), jnp.cos(ang_c), jnp.cos(ang_c)], axis=-1)
        sin = jnp.concatenate([-jnp.sin(ang_r), jnp.sin(ang_r), -jnp.sin(ang_c), jnp.sin(ang_c)], axis=-1)
        return cos, sin

    c64, s64 = table(HEAD_DIM)
    c64, s64 = jnp.tile(c64, (1, 2)), jnp.tile(s64, (1, 2))
    c32, s32 = table(MLA_ROPE)
    pad = ((0, 0), (MLA_NOPE, LANES - MLA_QK))
    c32 = jnp.pad(c32, pad, constant_values=1.0)
    s32 = jnp.pad(s32, pad)
    ctx_pad = ((n_ctx, 0), (0, 0))
    return (jnp.pad(c64, ctx_pad, constant_values=1.0), jnp.pad(s64, ctx_pad),
            jnp.pad(c32, ctx_pad, constant_values=1.0), jnp.pad(s32, ctx_pad))


def _rms(x):
    return x * lax.rsqrt(jnp.mean(x * x, axis=-1, keepdims=True) + EPS)


def _adaln(x, g, shift, scale):
    return (_rms(x) * g) * (1.0 + scale) + shift


def _rope(z, cos, sin, is_x1, shift):
    zb = z.astype(BF16)
    partner = jnp.where(is_x1, pltpu.roll(zb, LANES - shift, 1), pltpu.roll(zb, shift, 1))
    return z * cos + partner.astype(F32) * sin


def _half_rms(z, lo, g):
    sq = z * z
    zero = jnp.zeros_like(sq)
    s_lo = jnp.sum(jnp.where(lo, sq, zero), axis=-1, keepdims=True)
    s_hi = jnp.sum(jnp.where(lo, zero, sq), axis=-1, keepdims=True)
    ms = jnp.where(lo, s_lo, s_hi) * (1.0 / HEAD_DIM)
    return z * lax.rsqrt(ms + EPS) * g


def _dot(a, b):
    return jnp.dot(a, b, preferred_element_type=F32)


def _const_spec(shape):
    return pl.BlockSpec(shape, lambda *_: (0,) * len(shape))


def _layer_spec(stacked, layer, single_buffer=False):
    shape = stacked.shape[1:]
    return pl.BlockSpec((None,) + shape, lambda *_: (layer,) + (0,) * len(shape),
                        pipeline_mode=pl.Buffered(1) if single_buffer else None)


def _mod_spec(modarr, layer, pos_off, n_ctx_tiles):
    return pl.BlockSpec((None, None, None) + modarr.shape[3:],
                        lambda bi, ti: (layer, bi, (ti + pos_off >= n_ctx_tiles).astype(jnp.int32), 0, 0))


def _token_specs(tokens, tile_off, n_ctx_tiles):
    tm, d = TOKEN_TILE, tokens[0].shape[-1]
    if len(tokens) == 1:
        return [pl.BlockSpec((None, tm, d), lambda bi, ti: (bi, ti + tile_off, 0))]
    assert tile_off == 0
    return [pl.BlockSpec((None, tm, d), lambda bi, ti: (bi, jnp.minimum(ti, n_ctx_tiles - 1), 0)),
            pl.BlockSpec((None, tm, d), lambda bi, ti: (bi, jnp.maximum(ti - n_ctx_tiles, 0), 0))]


def _read_tokens(tok_refs, n_ctx_tiles):
    if len(tok_refs) == 1:
        return tok_refs[0][...]
    ctx_ref, x_ref = tok_refs
    return jnp.where(pl.program_id(1) < n_ctx_tiles, ctx_ref[...], x_ref[...])


def _mod_kernel(c_ref, w_ref, b_ref, o_ref):
    cc = c_ref[...]
    a = (cc * jax.nn.sigmoid(cc)).astype(BF16)
    o_ref[...] = _dot(a, w_ref[...].astype(BF16)) + b_ref[...]


def _mod_call(cc, w_mod, b_mod):
    depth, d, n = w_mod.shape
    tn = 1536
    rows = cc.shape[0]
    return pl.pallas_call(
        _mod_kernel,
        grid=(depth, n // tn),
        in_specs=[
            pl.BlockSpec((rows, d), lambda l, j: (0, 0)),
            pl.BlockSpec((None, d, tn), lambda l, j: (l, 0, j)),
            pl.BlockSpec((None, 1, tn), lambda l, j: (l, 0, j)),
        ],
        out_specs=pl.BlockSpec((None, rows, tn), lambda l, j: (l, 0, j)),
        out_shape=jax.ShapeDtypeStruct((depth, rows, n), F32),
        compiler_params=pltpu.CompilerParams(vmem_limit_bytes=VMEM_LIMIT),
        name="mod_vectors",
    )(cc, w_mod, b_mod.reshape(depth, 1, n))


def _proj_kernel(*refs, n_tok, n_ctx_tiles):
    (mod_ref, g1_ref, w1_ref, wuq_ref, wukv_ref, gq_g_ref, gk_g_ref, gmq_ref, gmkv_ref,
     c64_ref, s64_ref, c32_ref, s32_ref,
     dq_o, dk_o, dv_o, gq_o, gk_o, gv_o, mq_o, mk_o, mv_o) = refs[n_tok:]
    x = _read_tokens(refs[:n_tok], n_ctx_tiles)
    hb = _adaln(x, g1_ref[...], mod_ref[0:1, :], mod_ref[1:2, :]).astype(BF16)
    tm = x.shape[0]
    lane = lax.broadcasted_iota(jnp.int32, (tm, LANES), 1)
    lo = lane < HEAD_DIM
    x1_64 = (lane & ROT64) == 0
    x1_32 = (lane & ROT32) == 0
    c64, s64 = c64_ref[...], s64_ref[...]
    c32, s32 = c32_ref[...], s32_ref[...]
    qk_scale = HEAD_DIM ** -0.5 * LOG2E

    def seg(off, width):
        return _dot(hb, w1_ref[:, off:off + width])

    def blocks(z):
        return [(slice(i * LANES, (i + 1) * LANES), z[:, i * LANES:(i + 1) * LANES]) for i in range(z.shape[1] // LANES)]

    zm = seg(OFF_MCQ, W1_WIDTH - OFF_MCQ)
    mcq = (_rms(zm[:, :MLA_Q_LORA]) * gmq_ref[...]).astype(BF16)
    mckv = (_rms(zm[:, MLA_Q_LORA:MLA_Q_LORA + MLA_KV_LORA]) * gmkv_ref[...]).astype(BF16)
    mkr = _rope(zm[:, MLA_Q_LORA + MLA_KV_LORA:], c32, s32, x1_32, ROT32)

    for sl, z in blocks(seg(OFF_DQ, BRANCH_W)):
        dq_o[:, sl] = (_rope(z, c64, s64, x1_64, ROT64) * qk_scale).astype(BF16)
    for sl, z in blocks(seg(OFF_DK, BRANCH_W)):
        dk_o[:, sl] = _rope(z, c64, s64, x1_64, ROT64).astype(BF16)
    dv_o[...] = seg(OFF_DV, BRANCH_W).astype(BF16)
    for sl, z in blocks(seg(OFF_GQ, BRANCH_W)):
        gq = _half_rms(z, lo, gq_g_ref[...])
        gq_o[:, sl] = (_rope(gq, c64, s64, x1_64, ROT64) * qk_scale).astype(BF16)
    gkv = seg(OFF_GK, 2 * LANES)
    gk = _half_rms(gkv[:, :LANES], lo, gk_g_ref[...])
    gk_o[...] = _rope(gk, c64, s64, x1_64, ROT64).astype(BF16)
    gv_o[...] = gkv[:, LANES:].astype(BF16)

    mla_scale = MLA_QK ** -0.5 * LOG2E
    for sl, z in blocks(_dot(mcq, wuq_ref[...])):
        mq_o[:, sl] = (_rope(z, c32, s32, x1_32, ROT32) * mla_scale).astype(BF16)
    mkv = _dot(mckv, wukv_ref[...])
    for sl, z in blocks(mkv[:, :MLA_HEADS * LANES]):
        mk_o[:, sl] = (z + mkr).astype(BF16)
    mv_o[...] = mkv[:, MLA_HEADS * LANES:].astype(BF16)


def _proj_call(layer, tokens, modarr, g1, w1, wuq, wukv, gq_g, gk_g, gmq, gmkv, tables, n_ctx_tiles):
    b = tokens[0].shape[0]
    t = sum(a.shape[1] for a in tokens)
    tm = TOKEN_TILE
    tok = lambda w: pl.BlockSpec((None, tm, w), lambda bi, ti: (bi, ti, 0))
    tab = pl.BlockSpec((tm, LANES), lambda bi, ti: (ti, 0))
    widths = (BRANCH_W, BRANCH_W, BRANCH_W, BRANCH_W, LANES, LANES, MLA_HEADS * LANES, MLA_HEADS * LANES, BRANCH_W)
    return pl.pallas_call(
        functools.partial(_proj_kernel, n_tok=len(tokens), n_ctx_tiles=n_ctx_tiles),
        grid=(b, t // tm),
        in_specs=_token_specs(tokens, 0, n_ctx_tiles) + [_mod_spec(modarr, layer, 0, n_ctx_tiles)] + [
            _layer_spec(a, layer) for a in (g1, w1, wuq, wukv, gq_g, gk_g, gmq, gmkv)] + [tab, tab, tab, tab],
        out_specs=[tok(w) for w in widths],
        out_shape=[jax.ShapeDtypeStruct((b, t, w), BF16) for w in widths],
        compiler_params=pltpu.CompilerParams(
            dimension_semantics=("parallel", "arbitrary"), vmem_limit_bytes=VMEM_LIMIT),
        name="input_projection",
    )(*tokens, modarr, g1, w1, wuq, wukv, gq_g, gk_g, gmq, gmkv, *tables)


_Branch = collections.namedtuple("_Branch", "kind q_ref q_width k_ref k_width v_ref n_kv vt_slot o_ref")


def _attn_kernel(dq_ref, dk_ref, dv_ref, lam_ref, gdo_ref, gq_ref, gk_ref, gv_ref, mq_ref, mk_ref, mv_ref,
                 od_ref, og_ref, om_ref, vt_ref, *, n_ctx, lam_init, first_tile_is_ctx):
    tq = dq_ref.shape[0]
    n_keys = dk_ref.shape[0]
    lane = lax.broadcasted_iota(jnp.int32, (tq, LANES), 1)
    lo = lane < HEAD_DIM
    branches = (_Branch("diff", dq_ref, LANES, dk_ref, LANES, dv_ref, N_UNITS, 0, od_ref),
                _Branch("gqa", gq_ref, LANES, gk_ref, LANES, gv_ref, 1, N_UNITS, og_ref),
                _Branch("mla", mq_ref, 2 * LANES, mk_ref, 2 * LANES, mv_ref, N_UNITS, N_UNITS + 1, om_ref))

    @pl.when(pl.program_id(1) == 0)
    def _():
        for br in branches:
            for i in range(br.n_kv):
                vt_ref[br.vt_slot + i, 0:LANES, :] = br.v_ref[:, i * LANES:(i + 1) * LANES].astype(F32).T.astype(BF16)
                vt_ref[br.vt_slot + i, LANES:, :] = jnp.ones((SUM_ROWS, n_keys), BF16)

    def key_chunks(nk):
        step = KEY_CHUNK if nk % KEY_CHUNK == 0 else nk
        return [(c, c + step) for c in range(0, nk, step)]

    def unit_queries(br, u):
        q = br.q_ref[:, u * br.q_width:(u + 1) * br.q_width]
        if br.kind == "mla":
            return q[:, :LANES], q[:, LANES:]
        zero = jnp.zeros_like(q)
        return jnp.where(lo, q, zero), jnp.where(lo, zero, q)

    def map_scores(br, u, which, q, nk):
        kv = u % br.n_kv
        k = br.k_ref[0:nk, kv * br.k_width:(kv + 1) * br.k_width]
        if br.kind == "mla":
            k = k[:, :LANES] if which == 0 else k[:, LANES:]
        return lax.dot_general(k, q, (((1,), (1,)), ((), ())), preferred_element_type=F32)

    def chunk_softmax_pv(br, u, c0, c1, st):
        m = jnp.max(st, axis=0, keepdims=True)
        p = jnp.exp2((st - m).astype(BF16))
        return m, _dot(vt_ref[br.vt_slot + u % br.n_kv, :, c0:c1], p)

    def merge_chunks(parts):
        if len(parts) == 1:
            ot = parts[0][1]
        else:
            m_all = functools.reduce(jnp.maximum, [m for m, _ in parts])
            ot = sum(acc * jnp.exp2(m - m_all) for m, acc in parts)
        return ot[0:LANES] / ot[LANES:LANES + 1]

    def write_unit(br, u, oa, ob):
        if br.kind == "diff":
            o = (oa - lam_ref[0:1, 0:1] * ob).T
            o = _rms(o) * gdo_ref[...] * (1.0 - lam_init)
        else:
            row = lax.broadcasted_iota(jnp.int32, oa.shape, 0)
            o = jnp.where(row < HEAD_DIM, oa, ob).T
        br.o_ref[:, u * LANES:(u + 1) * LANES] = o.astype(BF16)

    def compute(nk):
        chunks = key_chunks(nk)
        maps = [(br, u, w) for br in branches for u in range(N_UNITS) for w in (0, 1)]
        queries = {}

        def scores(br, u, w):
            if (br.kind, u) not in queries:
                queries[(br.kind, u)] = unit_queries(br, u)
            return map_scores(br, u, w, queries[(br.kind, u)][w], nk)

        pend = [scores(*maps[i]) for i in range(min(SCORE_LOOKAHEAD, len(maps)))]
        first = None
        for t, (br, u, w) in enumerate(maps):
            if t + SCORE_LOOKAHEAD < len(maps):
                pend.append(scores(*maps[t + SCORE_LOOKAHEAD]))
            st = pend.pop(0)
            o = merge_chunks([chunk_softmax_pv(br, u, c0, c1, st[c0:c1]) for c0, c1 in chunks])
            if w == 0:
                first = o
            else:
                write_unit(br, u, first, o)

    if first_tile_is_ctx:
        j = pl.program_id(1)
        pl.when(j == 0)(lambda: compute(n_ctx))
        pl.when(j > 0)(lambda: compute(n_keys))
    else:
        compute(n_keys)


def _attn_call(layer, qkv, lam_vec, gdo, *, n_ctx, lam_init, include_ctx):
    dq, dk, dv, gq, gk, gv, mq, mk, mv = qkv
    b, t, _ = dk.shape
    tq = TOKEN_TILE
    q_off = 0 if include_ctx else n_ctx // tq
    nq = t // tq - q_off
    qspec = lambda a: pl.BlockSpec((None, tq, a.shape[-1]), lambda bi, j: (bi, j + q_off, 0))
    kvspec = lambda a: pl.BlockSpec((None, t, a.shape[-1]), lambda bi, j: (bi, 0, 0))
    ospec = pl.BlockSpec((None, tq, BRANCH_W), lambda bi, j: (bi, j, 0))
    oshape = jax.ShapeDtypeStruct((b, nq * tq, BRANCH_W), BF16)
    return pl.pallas_call(
        functools.partial(_attn_kernel, n_ctx=n_ctx, lam_init=lam_init, first_tile_is_ctx=include_ctx),
        grid=(b, nq),
        in_specs=[qspec(dq), kvspec(dk), kvspec(dv), _layer_spec(lam_vec, layer), _layer_spec(gdo, layer),
                  qspec(gq), kvspec(gk), kvspec(gv), qspec(mq), kvspec(mk), kvspec(mv)],
        out_specs=[ospec] * 3,
        out_shape=[oshape] * 3,
        scratch_shapes=[pltpu.VMEM((2 * N_UNITS + 1, LANES + SUM_ROWS, t), BF16)],
        compiler_params=pltpu.CompilerParams(
            dimension_semantics=("parallel", "arbitrary"), vmem_limit_bytes=VMEM_LIMIT),
        name="attention",
    )(dq, dk, dv, lam_vec, gdo, gq, gk, gv, mq, mk, mv)


def _merge_kernel(*refs, n_tok, n_ctx_tiles):
    (mod_ref, g1_ref, wg_ref, bg_ref, od_ref, og_ref, om_ref,
     wbd_ref, wbg_ref, wbm_ref, wout_ref, xo_ref) = refs[n_tok:]
    x = _read_tokens(refs[:n_tok], n_ctx_tiles)
    d = x.shape[-1]
    hb = _adaln(x, g1_ref[...], mod_ref[0:1, :], mod_ref[1:2, :]).astype(BF16)
    y = None
    for i, (o_ref, wb_ref) in enumerate(((od_ref, wbd_ref), (og_ref, wbg_ref), (om_ref, wbm_ref))):
        sl = slice(i * d, (i + 1) * d)
        gate = jax.nn.sigmoid(_dot(hb, wg_ref[:, sl]) + bg_ref[:, sl])
        term = gate * _dot(o_ref[...], wb_ref[...])
        y = term if y is None else y + term
    out = _dot(y.astype(BF16), wout_ref[...])
    xo_ref[...] = x + mod_ref[2:3, :] * out


def _merge_call(layer, tokens, modarr, g1, wg, bg, od, og, om, wbd, wbg, wbm, wout, *, x_off, n_ctx_tiles):
    b, d = tokens[0].shape[0], tokens[0].shape[-1]
    tm = TOKEN_TILE
    nt = od.shape[1] // tm
    tok = lambda w: pl.BlockSpec((None, tm, w), lambda bi, ti: (bi, ti, 0))
    return pl.pallas_call(
        functools.partial(_merge_kernel, n_tok=len(tokens), n_ctx_tiles=n_ctx_tiles),
        grid=(b, nt),
        in_specs=_token_specs(tokens, x_off, n_ctx_tiles) + [_mod_spec(modarr, layer, x_off, n_ctx_tiles)] + [
            _layer_spec(a, layer) for a in (g1, wg, bg)] + [tok(BRANCH_W)] * 3 + [
            _layer_spec(a, layer) for a in (wbd, wbg, wbm, wout)],
        out_specs=tok(d),
        out_shape=jax.ShapeDtypeStruct((b, nt * tm, d), F32),
        compiler_params=pltpu.CompilerParams(
            dimension_semantics=("parallel", "arbitrary"), vmem_limit_bytes=VMEM_LIMIT),
        name="branch_merge",
    )(*tokens, modarr, g1, wg, bg, od, og, om, wbd, wbg, wbm, wout)


def _ffn_kernel(*refs, final):
    if final:
        x_ref, mod_ref, g2_ref, win_ref, wout_ref, gf_ref, o_ref = refs
    else:
        x_ref, mod_ref, g2_ref, win_ref, wout_ref, o_ref = refs
    x = x_ref[...]
    hidden = wout_ref.shape[0]
    hb = _adaln(x, g2_ref[...], mod_ref[3:4, :], mod_ref[4:5, :]).astype(BF16)
    g = _dot(hb, win_ref[:, :hidden])
    u = _dot(hb, win_ref[:, hidden:])
    a = (g * jax.nn.sigmoid(g) * u).astype(BF16)
    xn = x + mod_ref[5:6, :] * _dot(a, wout_ref[...])
    if final:
        xn = _rms(xn) * gf_ref[...]
    o_ref[...] = xn


def _ffn_call(layer, xin, modarr, g2, win, wout, gf, *, pos_off, n_ctx_tiles):
    b, t, d = xin.shape
    final = gf is not None
    tm = 2 * TOKEN_TILE if final else TOKEN_TILE
    tok = pl.BlockSpec((None, tm, d), lambda bi, ti: (bi, ti, 0))
    in_specs = [tok, _mod_spec(modarr, layer, pos_off, n_ctx_tiles), _layer_spec(g2, layer)] + [
        _layer_spec(a, layer, single_buffer=final) for a in (win, wout)]
    args = [xin, modarr, g2, win, wout]
    if final:
        in_specs.append(_const_spec((1, d)))
        args.append(gf)
    return pl.pallas_call(
        functools.partial(_ffn_kernel, final=final),
        grid=(b, t // tm),
        in_specs=in_specs,
        out_specs=tok,
        out_shape=jax.ShapeDtypeStruct((b, t, d), F32),
        compiler_params=pltpu.CompilerParams(
            dimension_semantics=("parallel", "arbitrary"), vmem_limit_bytes=VMEM_LIMIT),
        name="swiglu",
    )(*args)


def kernel(x, c, ctx, c_ctx, w_mod, b_mod, g_norm1, w_in, b_gate, lam_q1, lam_k1, lam_q2, lam_k2, g_diff_out, g_gqa_q, g_gqa_k, g_mla_q, w_mla_uq, g_mla_kv, w_mla_ukv, w_br_diff, w_br_gqa, w_br_mla, w_out, g_norm2, w_ffn_in, w_ffn_out, g_final):
    b, seq, d = x.shape
    n_ctx = ctx.shape[1]
    depth = w_in.shape[0]
    assert n_ctx % TOKEN_TILE == 0 and seq % TOKEN_TILE == 0 and seq % GRID_W == 0
    n_ctx_tiles = n_ctx // TOKEN_TILE

    mod_rows = 32
    cc = jnp.zeros((mod_rows, d), F32).at[:b].set(c).at[b].set(c_ctx)
    mods = _mod_call(cc, w_mod, b_mod)
    mod_lat = mods[:, :b].reshape(depth, b, N_MOD, d)
    mod_ctx = jnp.broadcast_to(mods[:, b].reshape(depth, 1, N_MOD, d), (depth, b, N_MOD, d))
    modarr = jnp.stack([mod_ctx, mod_lat], axis=2)
    tables = _rope_tables(seq, n_ctx)

    lam_inits = [_lambda_init(l) for l in range(depth)]
    lam = (jnp.exp(jnp.sum(lam_q1 * lam_k1, axis=-1)) - jnp.exp(jnp.sum(lam_q2 * lam_k2, axis=-1))
           + jnp.asarray(lam_inits, F32))
    lam_vec = jnp.broadcast_to(lam.reshape(depth, 1, 1), (depth, 1, LANES))

    def row(a):
        return a.reshape(depth, 1, -1)

    w1, wg, wuq, wukv, wbg = _prep_weights(w_in, w_mla_uq, w_mla_ukv, w_br_gqa)
    wbd, wbm, wout = w_br_diff.astype(BF16), w_br_mla.astype(BF16), w_out.astype(BF16)
    wfi, wfo = w_ffn_in.astype(BF16), w_ffn_out.astype(BF16)
    g1, g2, bg = row(g_norm1), row(g_norm2), row(b_gate)
    gq_g, gk_g = row(jnp.tile(g_gqa_q, (1, 2))), row(jnp.tile(g_gqa_k, (1, 2)))
    gmq, gmkv, gdo = row(g_mla_q), row(g_mla_kv), row(g_diff_out)

    tokens = (ctx, x)
    for l in range(depth):
        last = l == depth - 1
        qkv = _proj_call(l, tokens, modarr, g1, w1, wuq, wukv, gq_g, gk_g, gmq, gmkv, tables, n_ctx_tiles)
        od, og, om = _attn_call(l, qkv, lam_vec, gdo, n_ctx=n_ctx, lam_init=lam_inits[l], include_ctx=not last)

        x_off = n_ctx_tiles if last else 0
        xmid = _merge_call(l, tokens, modarr, g1, wg, bg, od, og, om, wbd, wbg, wbm, wout,
                           x_off=x_off, n_ctx_tiles=n_ctx_tiles)
        xout = _ffn_call(l, xmid, modarr, g2, wfi, wfo, g_final.reshape(1, d) if last else None,
                         pos_off=x_off, n_ctx_tiles=n_ctx_tiles)
        tokens = (xout,)
    return xout
```
